```python
import jax, jax.numpy as jnp
from jax import lax
import numpy as np

D_MODEL = 2048
BATCH = 8
SEQ = 2048
DEPTH = 2

HEAD_DIM = 128
MLA_HEADS = 8
SB_HEADS = 4
DSA_HEADS = 4
N_MIX_HEADS = MLA_HEADS + SB_HEADS + DSA_HEADS
MIX_WIDTH = N_MIX_HEADS * HEAD_DIM
Q_LORA = 512
KV_LORA = 256
MLA_NOPE = 128
MLA_ROPE = 64
MLA_V = HEAD_DIM
IDX_HEADS = 16
IDX_DIM = 64
IDX_SCALE = (IDX_HEADS * IDX_DIM) ** -0.5
TOPK_MAX = 256
Q_BLOCK = 128
ROPE_THETA = 10000.0
MAX_POS_OFFSET = 4096
N_EXPERTS = 64
TOP_K = 8
N_GROUPS = 8
TOPK_GROUPS = 4
D_EXPERT = 512
D_SHARED = 512
ROUTED_SCALE = 2.5
MOE_BLOCK = 128
EPS = 1e-6

IN_SIZES = (Q_LORA, KV_LORA, MLA_ROPE,
            SB_HEADS * HEAD_DIM, SB_HEADS * HEAD_DIM, SB_HEADS * HEAD_DIM,
            DSA_HEADS * HEAD_DIM, HEAD_DIM, HEAD_DIM,
            IDX_HEADS * IDX_DIM, IDX_DIM, IDX_HEADS)
IN_WIDTH = int(sum(IN_SIZES))
IN_OFFSETS = tuple(int(o) for o in np.cumsum(IN_SIZES)[:-1])

kernel_name = "hybrid_mla_stickbreak_dsa_moe_adaln"


def rms_norm(x, g):
    xf = x.astype(jnp.float32)
    y = xf * lax.rsqrt(jnp.mean(xf * xf, axis=-1, keepdims=True) + EPS)
    return (y * g.astype(jnp.float32)).astype(x.dtype)


def rope(x, positions):
    d = x.shape[-1]
    inv_freq = ROPE_THETA ** (-jnp.arange(0, d, 2, dtype=jnp.float32) / d)
    ang = positions.astype(jnp.float32)[:, :, None, None] * inv_freq
    cos, sin = jnp.cos(ang), jnp.sin(ang)
    x1, x2 = jnp.split(x.astype(jnp.float32), 2, axis=-1)
    return jnp.concatenate([x1 * cos - x2 * sin, x2 * cos + x1 * sin], axis=-1).astype(x.dtype)


def over_query_blocks(fn, *qs):
    B, S = qs[0].shape[:2]
    nb = S // Q_BLOCK
    blocks = tuple(jnp.swapaxes(q.reshape(B, nb, Q_BLOCK, *q.shape[2:]), 0, 1) for q in qs)
    out = lax.map(lambda args: fn(*args), (jnp.arange(nb), *blocks))
    return jnp.swapaxes(out, 0, 1).reshape(B, S, *out.shape[3:])


def causal_softmax_attention(q, k, v, scale):
    key_pos = jnp.arange(k.shape[1])

    def block(i, qb):
        qpos = i * Q_BLOCK + jnp.arange(Q_BLOCK)
        s = jnp.einsum('bqhd,bkhd->bhqk', qb, k, preferred_element_type=jnp.float32) * scale
        s = jnp.where(key_pos[None, :] <= qpos[:, None], s, -jnp.inf)
        p = jax.nn.softmax(s, axis=-1).astype(v.dtype)
        return jnp.einsum('bhqk,bkhd->bqhd', p, v)

    return over_query_blocks(block, q)


def stick_breaking_attention(q, k, v):
    key_pos = jnp.arange(k.shape[1])
    scale = q.shape[-1] ** -0.5

    def block(i, qb):
        qpos = i * Q_BLOCK + jnp.arange(Q_BLOCK)
        z = jnp.einsum('bqhd,bkhd->bhqk', qb, k, preferred_element_type=jnp.float32) * scale
        strict = key_pos[None, :] < qpos[:, None]
        log_keep = jnp.where(strict, jax.nn.log_sigmoid(-z), 0.0)
        log_w = jax.nn.log_sigmoid(z) + lax.cumsum(log_keep, axis=3, reverse=True) - log_keep
        a = jnp.where(strict, jnp.exp(log_w), 0.0).astype(v.dtype)
        return jnp.einsum('bhqk,bkhd->bqhd', a, v)

    return over_query_blocks(block, q)


def dsa_attention(q, k, v, q_idx, k_idx, w_idx):
    S = k.shape[1]
    n_sel = min(TOPK_MAX, S // 4)
    key_pos = jnp.arange(S)
    scale = q.shape[-1] ** -0.5
    gather = jax.vmap(lambda t, idx: t[idx])

    def block(i, qb, qib, wb):
        qpos = i * Q_BLOCK + jnp.arange(Q_BLOCK)
        rel = jax.nn.relu(jnp.einsum('bqhe,bke->bqhk', qib, k_idx, preferred_element_type=jnp.float32))
        score = jnp.einsum('bqh,bqhk->bqk', wb.astype(jnp.float32), rel) * IDX_SCALE
        score = jnp.where(key_pos[None, None, :] <= qpos[None, :, None], score, -jnp.inf)
        _, sel = lax.top_k(score, n_sel)
        valid = sel <= qpos[None, :, None]
        kg = gather(k, sel)
        vg = gather(v, sel)
        s = jnp.einsum('bqhd,bqnd->bhqn', qb, kg, preferred_element_type=jnp.float32) * scale
        s = jnp.where(valid[:, None], s, -jnp.inf)
        p = jax.nn.softmax(s, axis=-1).astype(vg.dtype)
        return jnp.einsum('bhqn,bqnd->bqhd', p, vg)

    return over_query_blocks(block, q, q_idx, w_idx)


def token_mixers(h, positions, w_in, g_cq, g_ckv, w_uq, w_ukv, g_out, w_o):
    B, S, _ = h.shape
    proj = h @ w_in
    (cq, ckv, kr, sb_q, sb_k, sb_v, ds_q, ds_k, ds_v, ix_q, ix_k, ix_w) = jnp.split(proj, IN_OFFSETS, axis=-1)

    q = (rms_norm(cq, g_cq) @ w_uq).reshape(B, S, MLA_HEADS, MLA_NOPE + MLA_ROPE)
    kv = (rms_norm(ckv, g_ckv) @ w_ukv).reshape(B, S, MLA_HEADS, MLA_NOPE + MLA_V)
    k_rope = jnp.broadcast_to(rope(kr[:, :, None, :], positions), (B, S, MLA_HEADS, MLA_ROPE))
    q_mla = jnp.concatenate([q[..., :MLA_NOPE], rope(q[..., MLA_NOPE:], positions)], axis=-1)
    k_mla = jnp.concatenate([kv[..., :MLA_NOPE], k_rope], axis=-1)
    o_mla = causal_softmax_attention(q_mla, k_mla, kv[..., MLA_NOPE:], (MLA_NOPE + MLA_ROPE) ** -0.5)

    heads = lambda t: t.reshape(B, S, SB_HEADS, HEAD_DIM)
    o_sb = stick_breaking_attention(heads(sb_q), heads(sb_k), heads(sb_v))

    q_c = rope(ds_q.reshape(B, S, DSA_HEADS, HEAD_DIM), positions)
    k_c = rope(ds_k[:, :, None, :], positions)[:, :, 0, :]
    q_ix = rope(ix_q.reshape(B, S, IDX_HEADS, IDX_DIM), positions)
    k_ix = rope(ix_k[:, :, None, :], positions)[:, :, 0, :]
    o_dsa = dsa_attention(q_c, k_c, ds_v, q_ix, k_ix, ix_w)

    o = jnp.concatenate([o_mla, o_sb, o_dsa], axis=2)
    o = rms_norm(o, g_out.reshape(N_MIX_HEADS, HEAD_DIM)).reshape(B, S, MIX_WIDTH)
    return o @ w_o


def swiglu(x, w_gate, w_up, w_down):
    return (jax.nn.silu(x @ w_gate) * (x @ w_up)) @ w_down


def moe_ffn(h, w_router, b_router, w_gate, w_up, w_down, ws_gate, ws_up, ws_down):
    B, S, D = h.shape
    T = B * S
    xt = h.reshape(T, D)
    scores = jax.nn.sigmoid((xt @ w_router).astype(jnp.float32))
    biased = scores + b_router.astype(jnp.float32)
    grp = lax.top_k(biased.reshape(T, N_GROUPS, N_EXPERTS // N_GROUPS), 2)[0].sum(-1)
    _, gsel = lax.top_k(grp, TOPK_GROUPS)
    gmask = jnp.any(gsel[:, :, None] == jnp.arange(N_GROUPS), axis=1)
    emask = jnp.repeat(gmask, N_EXPERTS // N_GROUPS, axis=1)
    _, eidx = lax.top_k(jnp.where(emask, biased, -jnp.inf), TOP_K)
    gate = jnp.take_along_axis(scores, eidx, axis=1)
    gate = gate / jnp.sum(gate, axis=-1, keepdims=True) * ROUTED_SCALE

    M = T * TOP_K
    e_flat = eidx.reshape(M)
    tok_flat = jnp.repeat(jnp.arange(T, dtype=jnp.int32), TOP_K)
    order = jnp.argsort(e_flat, stable=True)
    e_s, tok_s, g_s = e_flat[order], tok_flat[order], gate.reshape(M)[order]
    counts = jnp.bincount(e_flat, length=N_EXPERTS)
    padded = (counts + MOE_BLOCK - 1) // MOE_BLOCK * MOE_BLOCK
    start = jnp.cumsum(counts) - counts
    pend = jnp.cumsum(padded)
    pstart = pend - padded
    dest = pstart[e_s] + (jnp.arange(M) - start[e_s])
    n_blocks = -(-M // MOE_BLOCK) + N_EXPERTS
    P = n_blocks * MOE_BLOCK
    tok_buf = jnp.full((P,), T, jnp.int32).at[dest].set(tok_s)
    g_buf = jnp.zeros((P,), jnp.float32).at[dest].set(g_s)
    blk_exp = jnp.minimum(jnp.searchsorted(pend, jnp.arange(n_blocks) * MOE_BLOCK, side='right'), N_EXPERTS - 1)
    x_pad = jnp.concatenate([xt, jnp.zeros((1, D), xt.dtype)], axis=0)

    def step(acc, blk):
        tok_b, g_b, e_b = blk
        yb = swiglu(x_pad[tok_b], w_gate[e_b], w_up[e_b], w_down[e_b])
        return acc.at[tok_b].add((yb * g_b[:, None].astype(yb.dtype)).astype(acc.dtype)), None

    acc, _ = lax.scan(step, jnp.zeros((T + 1, D), xt.dtype),
                      (tok_buf.reshape(n_blocks, MOE_BLOCK), g_buf.reshape(n_blocks, MOE_BLOCK), blk_exp))
    out = acc[:T] + swiglu(xt, ws_gate, ws_up, ws_down)
    return out.reshape(B, S, D)


def setup_inputs(seed: int = 0) -> dict:
    key = jax.random.key(seed)
    ks = iter(jax.random.split(key, 32))
    nrm = lambda shape, scale: jax.random.normal(next(ks), shape, jnp.float32) * scale
    gain = lambda shape: 1.0 + nrm(shape, 0.02)
    L = DEPTH
    return {
        "x": nrm((BATCH, SEQ, D_MODEL), 1.0),
        "c": nrm((BATCH, D_MODEL), 1.0),
        "positions": (jax.random.randint(next(ks), (BATCH, 1), 0, MAX_POS_OFFSET) + jnp.arange(SEQ)[None, :]).astype(jnp.int32),
        "norm_mix_g": gain((L, D_MODEL)),
        "norm_ffn_g": gain((L, D_MODEL)),
        "w_ada": nrm((L, D_MODEL, 6 * D_MODEL), 0.5 * D_MODEL ** -0.5),
        "b_ada": nrm((L, 6 * D_MODEL), 0.02),
        "w_in": nrm((L, D_MODEL, IN_WIDTH), D_MODEL ** -0.5),
        "g_cq": gain((L, Q_LORA)),
        "g_ckv": gain((L, KV_LORA)),
        "w_uq": nrm((L, Q_LORA, MLA_HEADS * (MLA_NOPE + MLA_ROPE)), Q_LORA ** -0.5),
        "w_ukv": nrm((L, KV_LORA, MLA_HEADS * (MLA_NOPE + MLA_V)), KV_LORA ** -0.5),
        "g_out": gain((L, MIX_WIDTH)),
        "w_o": nrm((L, MIX_WIDTH, D_MODEL), MIX_WIDTH ** -0.5),
        "w_router": nrm((L, D_MODEL, N_EXPERTS), D_MODEL ** -0.5),
        "b_router": nrm((L, N_EXPERTS), 0.01),
        "w_gate": nrm((L, N_EXPERTS, D_MODEL, D_EXPERT), D_MODEL ** -0.5),
        "w_up": nrm((L, N_EXPERTS, D_MODEL, D_EXPERT), D_MODEL ** -0.5),
        "w_down": nrm((L, N_EXPERTS, D_EXPERT, D_MODEL), D_EXPERT ** -0.5),
        "ws_gate": nrm((L, D_MODEL, D_SHARED), D_MODEL ** -0.5),
        "ws_up": nrm((L, D_MODEL, D_SHARED), D_MODEL ** -0.5),
        "ws_down": nrm((L, D_SHARED, D_MODEL), D_SHARED ** -0.5),
        "final_g": gain((D_MODEL,)),
    }


def reference(x, c, positions, norm_mix_g, norm_ffn_g, w_ada, b_ada, w_in, g_cq, g_ckv, w_uq, w_ukv,
              g_out, w_o, w_router, b_router, w_gate, w_up, w_down, ws_gate, ws_up, ws_down, final_g):
    for l in range(DEPTH):
        mod = (jax.nn.silu(c) @ w_ada[l] + b_ada[l])[:, None, :]
        sh_m, sc_m, g_m, sh_f, sc_f, g_f = jnp.split(mod, 6, axis=-1)
        h = rms_norm(x, norm_mix_g[l]) * (1 + sc_m) + sh_m
        x = x + g_m * token_mixers(h, positions, w_in[l], g_cq[l], g_ckv[l], w_uq[l], w_ukv[l], g_out[l], w_o[l])
        h = rms_norm(x, norm_ffn_g[l]) * (1 + sc_f) + sh_f
        x = x + g_f * moe_ffn(h, w_router[l], b_router[l], w_gate[l], w_up[l], w_down[l],
                              ws_gate[l], ws_up[l], ws_down[l])
    return rms_norm(x, final_g)
```

```python
import functools

import jax
import jax.numpy as jnp
import numpy as np
from jax import lax
from jax.experimental import pallas as pl
from jax.experimental.pallas import tpu as pltpu

F32 = jnp.float32
BF16 = jnp.bfloat16
I32 = jnp.int32

HEAD_DIM = 128
MLA_HEADS = 8
SB_HEADS = 4
DSA_HEADS = 4
N_MIX_HEADS = MLA_HEADS + SB_HEADS + DSA_HEADS
Q_LORA = 512
KV_LORA = 256
MLA_NOPE = 128
MLA_ROPE = 64
IDX_HEADS = 16
IDX_DIM = 64
IDX_SCALE = (IDX_HEADS * IDX_DIM) ** -0.5
TOPK_MAX = 256
ROPE_THETA = 10000.0
N_EXPERTS = 64
TOP_K = 8
N_GROUPS = 8
TOPK_GROUPS = 4
ROUTED_SCALE = 2.5
EPS = 1e-6

LANES = 128
VMEM_LIMIT = 56 * 1024 * 1024
NEG_INF = float("-inf")
INT_MIN = -(2 ** 31)

C_CQ, C_SBQ, C_SBK, C_SBV, C_DSQ = 0, 512, 1024, 1536, 2048
C_CKV, C_DSK, C_DSV, C_IXQ = 2560, 2816, 2944, 3072
C_KR, C_IXK, C_IXW, IN_PAD = 4096, 4224, 4352, 4608

MOE_ROWS = 256


def _cparams(sem):
    return pltpu.CompilerParams(dimension_semantics=sem, vmem_limit_bytes=VMEM_LIMIT)


def _nt_dot(a, b):
    return lax.dot_general(a, b, (((1,), (1,)), ((), ())), preferred_element_type=F32)


def _rms(x, g):
    return x * lax.rsqrt(jnp.mean(x * x, axis=-1, keepdims=True) + EPS) * g


def _ada_kernel(c_ref, w_ref, b_ref, o_ref):
    c = c_ref[...]
    a = (c * jax.nn.sigmoid(c)).astype(BF16)
    o_ref[0] = jnp.dot(a, w_ref[0].astype(BF16), preferred_element_type=F32) + b_ref[0]


def _ada(c, w_ada, b_ada):
    L, D, N = w_ada.shape
    B = c.shape[0]
    tn = 1024
    return pl.pallas_call(
        _ada_kernel,
        grid=(L, N // tn),
        in_specs=[pl.BlockSpec((B, D), lambda l, j: (0, 0)),
                  pl.BlockSpec((1, D, tn), lambda l, j: (l, 0, j)),
                  pl.BlockSpec((1, 1, tn), lambda l, j: (l, 0, j))],
        out_specs=pl.BlockSpec((1, B, tn), lambda l, j: (l, 0, j)),
        out_shape=jax.ShapeDtypeStruct((L, B, N), F32),
        compiler_params=_cparams(("parallel", "parallel")),
        name="ada_mod",
    )(c, w_ada, b_ada.reshape(L, 1, N))


def _rope_tab_kernel(pos_ref, inv_ref, sgn_ref, cos_ref, sin_ref):
    ang = pos_ref[...] * inv_ref[...]
    cos_ref[...] = jnp.cos(ang)
    sin_ref[...] = jnp.sin(ang) * sgn_ref[...]


def _rope_tables(posf, d):
    T = posf.shape[0]
    half = d // 2
    inv = ROPE_THETA ** (-jnp.arange(0, d, 2, dtype=F32) / d)
    reps = LANES // half
    inv_t = jnp.tile(inv, reps).reshape(1, LANES)
    sgn = jnp.tile(jnp.concatenate([-jnp.ones((half,), F32), jnp.ones((half,), F32)]), LANES // d).reshape(1, LANES)
    tm = min(T, 1024)
    return pl.pallas_call(
        _rope_tab_kernel,
        grid=(T // tm,),
        in_specs=[pl.BlockSpec((tm, 1), lambda i: (i, 0)),
                  pl.BlockSpec((1, LANES), lambda i: (0, 0)),
                  pl.BlockSpec((1, LANES), lambda i: (0, 0))],
        out_specs=[pl.BlockSpec((tm, LANES), lambda i: (i, 0))] * 2,
        out_shape=[jax.ShapeDtypeStruct((T, LANES), F32)] * 2,
        compiler_params=_cparams(("parallel",)),
        name=f"rope_tab{d}",
    )(posf, inv_t, sgn)


def _rope64(x, cos, sin_s, first_half):
    rot = jnp.where(first_half, pltpu.roll(x, 96, 1), pltpu.roll(x, 32, 1))
    return x * cos + rot * sin_s


def _rope128(x, cos, sin_s):
    return x * cos + pltpu.roll(x, 64, 1) * sin_s


def _nmm_kernel(*refs, modulate):
    if modulate:
        x_ref, g_ref, sc_ref, sh_ref, w_ref, o_ref, h_s = refs
    else:
        x_ref, g_ref, w_ref, o_ref, h_s = refs

    @pl.when(pl.program_id(1) == 0)
    def _():
        h = _rms(x_ref[...], g_ref[...])
        if modulate:
            h = h * (1.0 + sc_ref[0]) + sh_ref[0]
        h_s[...] = h.astype(BF16)

    o_ref[...] = jnp.dot(h_s[...], w_ref[...], preferred_element_type=F32).astype(o_ref.dtype)


def _nmm(x, col_blk, K, g, w, S, sc=None, sh=None, tm=1024, tn=512, out_dtype=F32):
    T = x.shape[0]
    N = w.shape[1]
    tm = min(tm, S)
    tn = min(tn, N)
    modulate = sc is not None
    in_specs = [pl.BlockSpec((tm, K), lambda i, j: (i, col_blk)),
                pl.BlockSpec((1, K), lambda i, j: (0, 0))]
    args = [x, g.reshape(1, K)]
    if modulate:
        B = sc.shape[0]
        in_specs += [pl.BlockSpec((1, 1, K), lambda i, j: (i * tm // S, 0, 0))] * 2
        args += [sc.reshape(B, 1, K), sh.reshape(B, 1, K)]
    in_specs.append(pl.BlockSpec((K, tn), lambda i, j: (0, j)))
    args.append(w)
    return pl.pallas_call(
        functools.partial(_nmm_kernel, modulate=modulate),
        grid=(T // tm, N // tn),
        in_specs=in_specs,
        out_specs=pl.BlockSpec((tm, tn), lambda i, j: (i, j)),
        out_shape=jax.ShapeDtypeStruct((T, N), out_dtype),
        scratch_shapes=[pltpu.VMEM((tm, K), BF16)],
        compiler_params=_cparams(("parallel", "arbitrary")),
        name="norm_matmul",
    )(*args)


def _mm_res_kernel(a_ref, w_ref, r_ref, gt_ref, o_ref):
    acc = jnp.dot(a_ref[...], w_ref[...], preferred_element_type=F32)
    o_ref[...] = r_ref[...] + gt_ref[0] * acc


def _mm_res(a, w, res, gate, S, tm=1024, tn=512):
    T, K = a.shape
    N = w.shape[1]
    B = gate.shape[0]
    tm = min(tm, S)
    return pl.pallas_call(
        _mm_res_kernel,
        grid=(T // tm, N // tn),
        in_specs=[pl.BlockSpec((tm, K), lambda i, j: (i, 0)),
                  pl.BlockSpec((K, tn), lambda i, j: (0, j)),
                  pl.BlockSpec((tm, tn), lambda i, j: (i, j)),
                  pl.BlockSpec((1, 1, tn), lambda i, j: (i * tm // S, 0, j))],
        out_specs=pl.BlockSpec((tm, tn), lambda i, j: (i, j)),
        out_shape=jax.ShapeDtypeStruct((T, N), F32),
        compiler_params=_cparams(("parallel", "parallel")),
        name="out_proj_residual",
    )(a, w, res, gate.reshape(B, 1, N))


def _prep_kernel(p_ref, q_ref, kv_ref, c64_ref, s64_ref, c128_ref, s128_ref,
                 qm_ref, km_ref, vm_ref, sbq_ref, sbk_ref, sbv_ref,
                 dq_ref, dk_ref, dv_ref, iq_ref, ika_ref, ikb_ref, iw_ref):
    c64, s64 = c64_ref[...], s64_ref[...]
    c128, s128 = c128_ref[...], s128_ref[...]
    lane = lax.broadcasted_iota(I32, c64.shape, 1)
    first = (lane % 64) < 32

    def sl(ref, off, width=LANES):
        return ref[:, off:off + width]

    kr = _rope64(sl(p_ref, C_KR), c64, s64, first).astype(BF16)
    for h in range(MLA_HEADS):
        qm_ref[:, 256 * h:256 * h + 128] = sl(q_ref, 256 * h).astype(BF16)
        qm_ref[:, 256 * h + 128:256 * h + 256] = _rope64(sl(q_ref, 256 * h + 128), c64, s64, first).astype(BF16)
        km_ref[:, 256 * h:256 * h + 128] = sl(kv_ref, 128 * h).astype(BF16)
        km_ref[:, 256 * h + 128:256 * h + 256] = kr
    vm_ref[...] = kv_ref[:, MLA_HEADS * 128:].astype(BF16)
    sbq_ref[...] = sl(p_ref, C_SBQ, 512).astype(BF16)
    sbk_ref[...] = sl(p_ref, C_SBK, 512).astype(BF16)
    sbv_ref[...] = sl(p_ref, C_SBV, 512).astype(BF16)
    for h in range(DSA_HEADS):
        dq_ref[:, 128 * h:128 * h + 128] = _rope128(sl(p_ref, C_DSQ + 128 * h), c128, s128).astype(BF16)
    dk_ref[...] = _rope128(sl(p_ref, C_DSK), c128, s128).astype(BF16)
    dv_ref[...] = sl(p_ref, C_DSV).astype(BF16)
    for j in range(IDX_HEADS // 2):
        iq_ref[:, 128 * j:128 * j + 128] = _rope64(sl(p_ref, C_IXQ + 128 * j), c64, s64, first).astype(BF16)
    ik = _rope64(sl(p_ref, C_IXK), c64, s64, first)
    ika_ref[...] = ik.astype(BF16)
    ikb_ref[...] = pltpu.roll(ik, 64, 1).astype(BF16)
    iw_ref[...] = sl(p_ref, C_IXW)


def _prep(proj, q_raw, kv_raw, tabs64, tabs128, tm=256):
    T = proj.shape[0]
    row = lambda w: pl.BlockSpec((tm, w), lambda i: (i, 0))
    widths = [2048, 2048, 1024, 512, 512, 512, 512, 128, 128, 1024, 128, 128]
    out_shape = [jax.ShapeDtypeStruct((T, w), BF16) for w in widths] + [jax.ShapeDtypeStruct((T, LANES), F32)]
    return pl.pallas_call(
        _prep_kernel,
        grid=(T // tm,),
        in_specs=[row(IN_PAD), row(2048), row(2048), row(LANES), row(LANES), row(LANES), row(LANES)],
        out_specs=[row(w) for w in widths] + [row(LANES)],
        out_shape=out_shape,
        compiler_params=_cparams(("parallel",)),
        name="mixer_prep",
    )(proj, q_raw, kv_raw, *tabs64, *tabs128)


def _head_out(o, g):
    return _rms(o, g).astype(BF16)


def _mla_kernel(q_ref, k_ref, v_ref, g_ref, o_ref, m_s, l_s, acc_s, *, t, scale):
    i = pl.program_id(2)
    q = q_ref[...]
    m_s[...] = jnp.full(m_s.shape, NEG_INF, F32)
    l_s[...] = jnp.zeros(l_s.shape, F32)
    acc_s[...] = jnp.zeros(acc_s.shape, F32)

    def step(c, masked):
        k0 = pl.multiple_of(c * t, t)
        s = _nt_dot(q, k_ref[pl.ds(k0, t), :]) * scale
        if masked:
            row = lax.broadcasted_iota(I32, (t, t), 0)
            col = lax.broadcasted_iota(I32, (t, t), 1)
            s = jnp.where(col <= row, s, NEG_INF)
        m_prev = m_s[...]
        m_new = jnp.maximum(m_prev, jnp.max(s, axis=1, keepdims=True))
        alpha = jnp.exp(m_prev - m_new)
        p = jnp.exp(s - m_new)
        l_s[...] = alpha * l_s[...] + jnp.sum(p, axis=1, keepdims=True)
        acc_s[...] = alpha * acc_s[...] + jnp.dot(p.astype(BF16), v_ref[pl.ds(k0, t), :], preferred_element_type=F32)
        m_s[...] = m_new

    def body(c, carry):
        step(c, False)
        return carry

    lax.fori_loop(0, i, body, 0)
    step(i, True)
    o_ref[...] = _head_out(acc_s[...] / l_s[...], g_ref[...])


def _mla(qm, km, vm, g_out, B, S, t=512):
    T = qm.shape[0]
    t = min(t, S)
    nq = S // t
    H = MLA_HEADS
    return pl.pallas_call(
        functools.partial(_mla_kernel, t=t, scale=(MLA_NOPE + MLA_ROPE) ** -0.5),
        grid=(B, H, nq),
        in_specs=[pl.BlockSpec((t, 256), lambda b, h, i: (b * nq + i, h)),
                  pl.BlockSpec((S, 256), lambda b, h, i: (b, h)),
                  pl.BlockSpec((S, 128), lambda b, h, i: (b, h)),
                  pl.BlockSpec((1, 128), lambda b, h, i: (0, h))],
        out_specs=pl.BlockSpec((t, 128), lambda b, h, i: (b * nq + i, h)),
        out_shape=jax.ShapeDtypeStruct((T, H * HEAD_DIM), BF16),
        scratch_shapes=[pltpu.VMEM((t, 1), F32), pltpu.VMEM((t, 1), F32), pltpu.VMEM((t, 128), F32)],
        compiler_params=_cparams(("parallel", "parallel", "arbitrary")),
        name="mla_attention",
    )(qm, km, vm, g_out)


def _sb_kernel(q_ref, k_ref, v_ref, g_ref, tri_ref, o_ref, carry_s, acc_s, *, t, scale):
    i = pl.program_id(2)
    q = q_ref[...]
    tri = tri_ref[...]
    carry_s[...] = jnp.zeros(carry_s.shape, F32)
    acc_s[...] = jnp.zeros(acc_s.shape, F32)

    def step(c, masked):
        k0 = pl.multiple_of(c * t, t)
        z = _nt_dot(q, k_ref[pl.ds(k0, t), :]) * scale
        sp = jnp.log(1.0 + jnp.exp(-jnp.abs(z)))
        log_beta = jnp.minimum(z, 0.0) - sp
        log_keep = jnp.minimum(-z, 0.0) - sp
        if masked:
            row = lax.broadcasted_iota(I32, (t, t), 0)
            col = lax.broadcasted_iota(I32, (t, t), 1)
            strict = col < row
            log_keep = jnp.where(strict, log_keep, 0.0)
        hi = log_keep.astype(BF16)
        lo = (log_keep - hi.astype(F32)).astype(BF16)
        suffix = jnp.dot(hi, tri, preferred_element_type=F32) + jnp.dot(lo, tri, preferred_element_type=F32)
        a = jnp.exp(log_beta + suffix + carry_s[...])
        if masked:
            a = jnp.where(strict, a, 0.0)
        acc_s[...] += jnp.dot(a.astype(BF16), v_ref[pl.ds(k0, t), :], preferred_element_type=F32)
        carry_s[...] += jnp.sum(log_keep, axis=1, keepdims=True)

    step(i, True)

    def body(j, carry):
        step(i - 1 - j, False)
        return carry

    lax.fori_loop(0, i, body, 0)
    o_ref[...] = _head_out(acc_s[...], g_ref[...])


def _sb(sbq, sbk, sbv, g_out, B, S, t=256):
    T = sbq.shape[0]
    t = min(t, S)
    nq = S // t
    H = SB_HEADS
    r = np.arange(t)
    tri = jnp.asarray((r[:, None] > r[None, :]).astype(np.float32), BF16)
    return pl.pallas_call(
        functools.partial(_sb_kernel, t=t, scale=HEAD_DIM ** -0.5),
        grid=(B, H, nq),
        in_specs=[pl.BlockSpec((t, 128), lambda b, h, i: (b * nq + i, h)),
                  pl.BlockSpec((S, 128), lambda b, h, i: (b, h)),
                  pl.BlockSpec((S, 128), lambda b, h, i: (b, h)),
                  pl.BlockSpec((1, 128), lambda b, h, i: (0, MLA_HEADS + h)),
                  pl.BlockSpec((t, t), lambda b, h, i: (0, 0))],
        out_specs=pl.BlockSpec((t, 128), lambda b, h, i: (b * nq + i, h)),
        out_shape=jax.ShapeDtypeStruct((T, H * HEAD_DIM), BF16),
        scratch_shapes=[pltpu.VMEM((t, 1), F32), pltpu.VMEM((t, 128), F32)],
        compiler_params=_cparams(("parallel", "parallel", "arbitrary")),
        name="stickbreak_attention",
    )(sbq, sbk, sbv, g_out, tri)


def _dsa_kernel(iq_ref, ika_ref, ikb_ref, wt_ref, dq_ref, dk_ref, vt_ref, g_ref, o_ref,
                key_s, m_s, l_s, acc_s, *, tq, tk, n_sel, scale):
    i = pl.program_id(1)
    nch = (i * tq + tq + tk - 1) // tk
    qpos = i * tq + lax.broadcasted_iota(I32, (1, tq), 1)
    kiota = lax.broadcasted_iota(I32, (tk, 1), 0)
    wt = wt_ref[0]
    half = IDX_HEADS // 2

    def score_chunk(c, carry):
        k0 = pl.multiple_of(c * tk, tk)
        kk = jnp.concatenate([ika_ref[pl.ds(k0, tk), :], ikb_ref[pl.ds(k0, tk), :]], axis=0)
        score = jnp.zeros((tk, tq), F32)
        for j in range(half):
            r = jnp.maximum(_nt_dot(kk, iq_ref[:, 128 * j:128 * j + 128]), 0.0)
            score = score + r[:tk] * wt[2 * j:2 * j + 1, :] + r[tk:] * wt[2 * j + 1:2 * j + 2, :]
        score = score * IDX_SCALE
        score = jnp.where(k0 + kiota <= qpos, score, NEG_INF)
        bits = pltpu.bitcast(score, I32)
        key_s[pl.ds(k0, tk), :] = jnp.where(bits < 0, bits ^ jnp.int32(0x7FFFFFFF), bits)
        return carry

    lax.fori_loop(0, nch, score_chunk, 0)

    def count(pred):
        def body(c, acc):
            k0 = pl.multiple_of(c * tk, tk)
            hit = pred(key_s[pl.ds(k0, tk), :], k0 + kiota).astype(I32)
            return acc + jnp.sum(hit.reshape(tk // 8, 8, tq), axis=0)
        part = lax.fori_loop(0, nch, body, jnp.zeros((8, tq), I32))
        return jnp.sum(part, axis=0, keepdims=True)

    c0 = count(lambda k, idx: k >= 0)
    thr = jnp.where(c0 >= n_sel, jnp.int32(0), jnp.int32(INT_MIN))

    def vbit(b, thr):
        cand = thr + jnp.left_shift(jnp.int32(1), 30 - b)
        cnt = count(lambda k, idx: k >= cand)
        return jnp.where(cnt >= n_sel, cand, thr)

    thr = lax.fori_loop(0, 31, vbit, thr)
    need = n_sel - count(lambda k, idx: k > thr)

    idx_bits = max(1, int(np.ceil(np.log2(key_s.shape[0]))))

    def ibit(b, bound):
        cand = bound + jnp.left_shift(jnp.int32(1), idx_bits - 1 - b)
        cnt = count(lambda k, idx: (k == thr) & (idx < cand))
        return jnp.where(cnt < need, cand, bound)

    bound = lax.fori_loop(0, idx_bits, ibit, jnp.zeros((1, tq), I32))

    qc = jnp.concatenate([dq_ref[:, 128 * h:128 * h + 128] for h in range(DSA_HEADS)], axis=0)
    m_s[...] = jnp.full(m_s.shape, NEG_INF, F32)
    l_s[...] = jnp.zeros(l_s.shape, F32)
    acc_s[...] = jnp.zeros(acc_s.shape, F32)

    def attn_chunk(c, carry):
        k0 = pl.multiple_of(c * tk, tk)
        key = key_s[pl.ds(k0, tk), :]
        idx = k0 + kiota
        sel = ((key > thr) | ((key == thr) & (idx <= bound))) & (idx <= qpos)
        bias = jnp.where(sel, 0.0, NEG_INF)
        s = _nt_dot(dk_ref[pl.ds(k0, tk), :], qc) * scale + jnp.concatenate([bias] * DSA_HEADS, axis=1)
        m_prev = m_s[...]
        m_new = jnp.maximum(m_prev, jnp.max(s, axis=0, keepdims=True))
        m_safe = jnp.where(m_new == NEG_INF, 0.0, m_new)
        alpha = jnp.exp(m_prev - m_safe)
        p = jnp.exp(s - m_safe)
        l_s[...] = alpha * l_s[...] + jnp.sum(p, axis=0, keepdims=True)
        acc_s[...] = alpha * acc_s[...] + jnp.dot(vt_ref[:, pl.ds(k0, tk)], p.astype(BF16),
                                                  preferred_element_type=F32)
        m_s[...] = m_new
        return carry

    lax.fori_loop(0, nch, attn_chunk, 0)
    ot = acc_s[...] / l_s[...]
    for h in range(DSA_HEADS):
        o_ref[:, 128 * h:128 * h + 128] = _head_out(ot[:, h * tq:(h + 1) * tq].T, g_ref[:, 128 * h:128 * h + 128])


def _dsa(iq, ika, ikb, wt, dq, dk, vt, g_out, B, S, tq=128, tk=256):
    T = iq.shape[0]
    tk = min(tk, S)
    nq = S // tq
    H = DSA_HEADS
    n_sel = min(TOPK_MAX, S // 4)
    g_c = g_out[:, (MLA_HEADS + SB_HEADS) * HEAD_DIM:]
    return pl.pallas_call(
        functools.partial(_dsa_kernel, tq=tq, tk=tk, n_sel=n_sel, scale=HEAD_DIM ** -0.5),
        grid=(B, nq),
        in_specs=[pl.BlockSpec((tq, IDX_HEADS * IDX_DIM), lambda b, i: (b * nq + i, 0)),
                  pl.BlockSpec((S, 128), lambda b, i: (b, 0)),
                  pl.BlockSpec((S, 128), lambda b, i: (b, 0)),
                  pl.BlockSpec((1, IDX_HEADS, tq), lambda b, i: (b, 0, i)),
                  pl.BlockSpec((tq, H * 128), lambda b, i: (b * nq + i, 0)),
                  pl.BlockSpec((S, 128), lambda b, i: (b, 0)),
                  pl.BlockSpec((128, S), lambda b, i: (0, b)),
                  pl.BlockSpec((1, H * 128), lambda b, i: (0, 0))],
        out_specs=pl.BlockSpec((tq, H * 128), lambda b, i: (b * nq + i, 0)),
        out_shape=jax.ShapeDtypeStruct((T, H * HEAD_DIM), BF16),
        scratch_shapes=[pltpu.VMEM((S, tq), I32), pltpu.VMEM((1, H * tq), F32), pltpu.VMEM((1, H * tq), F32),
                        pltpu.VMEM((128, H * tq), F32)],
        compiler_params=_cparams(("parallel", "arbitrary")),
        name="dsa_attention",
    )(iq, ika, ikb, wt, dq, dk, vt, g_c)


def _route_kernel(x_ref, g_ref, sc_ref, sh_ref, wr_ref, br_ref, tri_ref,
                  h_ref, eidx_ref, gate_ref, pos_ref, cnt_ref, run_s):
    @pl.when(pl.program_id(0) == 0)
    def _():
        run_s[...] = jnp.zeros(run_s.shape, F32)

    h = _rms(x_ref[...], g_ref[...]) * (1.0 + sc_ref[0]) + sh_ref[0]
    h_ref[...] = h
    tm = h.shape[0]
    logits = _nt_dot(wr_ref[...], h.astype(BF16))
    scores = jax.nn.sigmoid(logits)
    biased = scores + br_ref[...]
    gsz = N_EXPERTS // N_GROUPS
    b3 = biased.reshape(N_GROUPS, gsz, tm)
    m1 = jnp.max(b3, axis=1, keepdims=True)
    n1 = jnp.sum((b3 == m1).astype(F32), axis=1, keepdims=True)
    m2 = jnp.max(jnp.where(b3 < m1, b3, NEG_INF), axis=1, keepdims=True)
    grp = (m1 + jnp.where(n1 >= 2.0, m1, m2)).reshape(N_GROUPS, tm)
    gi = lax.broadcasted_iota(I32, (N_GROUPS, 1), 0)
    grank = jnp.zeros((N_GROUPS, tm), F32)
    for g in range(N_GROUPS):
        rowv = grp[g:g + 1, :]
        grank = grank + jnp.where((rowv > grp) | ((rowv == grp) & (g < gi)), 1.0, 0.0)
    gmask = grank.reshape(N_GROUPS, 1, tm) < float(TOPK_GROUPS)
    masked = jnp.where(gmask, b3, NEG_INF).reshape(N_EXPERTS, tm)
    ei = lax.broadcasted_iota(I32, (N_EXPERTS, 1), 0)
    rank = jnp.zeros((N_EXPERTS, tm), F32)
    for e in range(N_EXPERTS):
        rowv = masked[e:e + 1, :]
        rank = rank + jnp.where((rowv > masked) | ((rowv == masked) & (e < ei)), 1.0, 0.0)
    sel = rank < float(TOP_K)
    selm = sel.astype(F32)
    gsum = jnp.sum(scores * selm, axis=0, keepdims=True)
    gate = scores * selm / gsum * ROUTED_SCALE
    within = jnp.dot(selm.astype(BF16), tri_ref[...], preferred_element_type=F32)
    posf = within + run_s[...]
    run_s[...] += jnp.sum(selm, axis=1, keepdims=True)
    cnt_ref[...] = jnp.broadcast_to(run_s[...], cnt_ref.shape)
    eif = ei.astype(F32)
    for k in range(TOP_K):
        onek = rank == float(k)
        eidx_ref[k:k + 1, :] = jnp.sum(jnp.where(onek, eif, 0.0), axis=0, keepdims=True).astype(I32)
        gate_ref[k:k + 1, :] = jnp.sum(jnp.where(onek, gate, 0.0), axis=0, keepdims=True)
        pos_ref[k:k + 1, :] = jnp.sum(jnp.where(onek, posf, 0.0), axis=0, keepdims=True).astype(I32)


def _route(x, g, sc, sh, w_router, b_router, S, tm=512):
    T, D = x.shape
    B = sc.shape[0]
    tm = min(tm, S)
    r = np.arange(tm)
    tri = jnp.asarray((r[:, None] < r[None, :]).astype(np.float32), BF16)
    full = lambda shape: pl.BlockSpec(shape, lambda i: (0,) * len(shape))
    return pl.pallas_call(
        _route_kernel,
        grid=(T // tm,),
        in_specs=[pl.BlockSpec((tm, D), lambda i: (i, 0)),
                  full((1, D)),
                  pl.BlockSpec((1, 1, D), lambda i: (i * tm // S, 0, 0)),
                  pl.BlockSpec((1, 1, D), lambda i: (i * tm // S, 0, 0)),
                  full((N_EXPERTS, D)),
                  full((N_EXPERTS, 1)),
                  full((tm, tm))],
        out_specs=[pl.BlockSpec((tm, D), lambda i: (i, 0)),
                   pl.BlockSpec((TOP_K, tm), lambda i: (0, i)),
                   pl.BlockSpec((TOP_K, tm), lambda i: (0, i)),
                   pl.BlockSpec((TOP_K, tm), lambda i: (0, i)),
                   full((N_EXPERTS, LANES))],
        out_shape=[jax.ShapeDtypeStruct((T, D), F32),
                   jax.ShapeDtypeStruct((TOP_K, T), I32),
                   jax.ShapeDtypeStruct((TOP_K, T), F32),
                   jax.ShapeDtypeStruct((TOP_K, T), I32),
                   jax.ShapeDtypeStruct((N_EXPERTS, LANES), F32)],
        scratch_shapes=[pltpu.VMEM((N_EXPERTS, 1), F32)],
        compiler_params=_cparams(("arbitrary",)),
        name="ffn_norm_route",
    )(x, g.reshape(1, D), sc.reshape(B, 1, D), sh.reshape(B, 1, D),
      w_router.T.astype(BF16), b_router.reshape(N_EXPERTS, 1), tri)


def _moe_kernel(bexp_ref, bval_ref, slot_hbm, h_hbm, wg_ref, wu_ref, wd_ref, y_hbm,
                xbuf, ybuf, wg_s, wu_s, wd_s, idx_s, isem, gsem, ssem, *, n_tok):
    b = pl.program_id(0)
    nb = pl.num_programs(0)
    rows = xbuf.shape[1]
    cur = b % 2

    def bval(blk):
        return bval_ref[jnp.clip(blk, 0, nb - 1)]

    def idx_copy(blk):
        return pltpu.make_async_copy(slot_hbm.at[blk], idx_s.at[blk % 3], isem.at[blk % 3])

    def gather(blk, buf):
        ring = blk % 3

        def body(r, carry):
            tok = idx_s[ring, r] & (n_tok - 1)
            pltpu.make_async_copy(h_hbm.at[pl.ds(tok, 1), :], xbuf.at[buf, pl.ds(r, 1), :], gsem.at[buf]).start()
            return carry
        lax.fori_loop(0, rows, body, 0)

    def scatter(blk, buf):
        ring = blk % 3

        def body(r, carry):
            dst = idx_s[ring, r]
            pltpu.make_async_copy(ybuf.at[buf, pl.ds(r, 1), :], y_hbm.at[pl.ds(dst, 1), :], ssem.at[buf]).start()
            return carry
        lax.fori_loop(0, bval(blk), body, 0)

    def wait_gather(buf):
        pltpu.make_async_copy(xbuf.at[buf], xbuf.at[buf], gsem.at[buf]).wait()

    def wait_scatter(blk, buf):
        n = bval(blk)
        n8 = pl.multiple_of(n // 8 * 8, 8)

        @pl.when(n8 > 0)
        def _():
            pltpu.make_async_copy(ybuf.at[buf, pl.ds(0, n8), :], ybuf.at[buf, pl.ds(0, n8), :], ssem.at[buf]).wait()

        def one_row(r, carry):
            pltpu.make_async_copy(ybuf.at[buf, pl.ds(0, 1), :], ybuf.at[buf, pl.ds(0, 1), :], ssem.at[buf]).wait()
            return carry
        lax.fori_loop(0, n - n8, one_row, 0)

    @pl.when(b == 0)
    def _():
        idx_copy(0).start()
        idx_copy(0).wait()
        gather(0, 0)

        @pl.when(nb > 1)
        def _():
            idx_copy(1).start()

    @pl.when(b + 2 < nb)
    def _():
        idx_copy(b + 2).start()

    @pl.when(b + 1 < nb)
    def _():
        idx_copy(b + 1).wait()

        @pl.when(bval(b + 1) > 0)
        def _():
            gather(b + 1, 1 - cur)

    @pl.when((b == 0) | (bexp_ref[b] != bexp_ref[jnp.maximum(b - 1, 0)]))
    def _():
        wg_s[...] = wg_ref[0, 0].astype(BF16)
        wu_s[...] = wu_ref[0, 0].astype(BF16)
        wd_s[...] = wd_ref[0, 0].astype(BF16)

    @pl.when((b >= 2) & (bval(b - 2) > 0))
    def _():
        wait_scatter(b - 2, cur)

    @pl.when((b == 0) | (bval(b) > 0))
    def _():
        wait_gather(cur)

    @pl.when(bval(b) > 0)
    def _():
        x = xbuf[cur].astype(BF16)
        gt = jnp.dot(x, wg_s[...], preferred_element_type=F32)
        up = jnp.dot(x, wu_s[...], preferred_element_type=F32)
        mid = (gt * jax.nn.sigmoid(gt) * up).astype(BF16)
        ybuf[cur] = jnp.dot(mid, wd_s[...], preferred_element_type=F32)
        scatter(b, cur)

    @pl.when(b == nb - 1)
    def _():
        @pl.when(bval(b) > 0)
        def _():
            wait_scatter(b, cur)

        @pl.when((nb > 1) & (bval(b - 1) > 0))
        def _():
            wait_scatter(b - 1, 1 - cur)


def _moe(h, slot_ids, blk_exp, blk_val, w_gate, w_up, w_down, l):
    T, D = h.shape
    assert T & (T - 1) == 0
    DE = w_gate.shape[3]
    nb = blk_exp.shape[0]
    wspec = lambda shape: pl.BlockSpec((1, 1) + shape, lambda b, bexp, bval: (l, bexp[b], 0, 0))
    grid_spec = pltpu.PrefetchScalarGridSpec(
        num_scalar_prefetch=2,
        grid=(nb,),
        in_specs=[pl.BlockSpec(memory_space=pl.ANY), pl.BlockSpec(memory_space=pl.ANY),
                  wspec((D, DE)), wspec((D, DE)), wspec((DE, D))],
        out_specs=pl.BlockSpec(memory_space=pl.ANY),
        scratch_shapes=[pltpu.VMEM((2, MOE_ROWS, D), F32), pltpu.VMEM((2, MOE_ROWS, D), F32),
                        pltpu.VMEM((D, DE), BF16), pltpu.VMEM((D, DE), BF16), pltpu.VMEM((DE, D), BF16),
                        pltpu.SMEM((3, MOE_ROWS), I32),
                        pltpu.SemaphoreType.DMA((3,)), pltpu.SemaphoreType.DMA((2,)), pltpu.SemaphoreType.DMA((2,))],
    )
    return pl.pallas_call(
        functools.partial(_moe_kernel, n_tok=T),
        grid_spec=grid_spec,
        out_shape=jax.ShapeDtypeStruct((TOP_K * T, D), F32),
        compiler_params=_cparams(("arbitrary",)),
        name="moe_experts",
    )(blk_exp, blk_val, slot_ids.reshape(nb, MOE_ROWS), h, w_gate, w_up, w_down)


def _ffn_out_kernel(y_ref, gk_ref, h_ref, x_ref, gf_ref, wsg_ref, wsu_ref, wsd_ref, fg_ref, o_ref, *, final):
    routed = y_ref[0] * gk_ref[0]
    for k in range(1, TOP_K):
        routed = routed + y_ref[k] * gk_ref[k]
    hb = h_ref[...].astype(BF16)
    gt = jnp.dot(hb, wsg_ref[...], preferred_element_type=F32)
    up = jnp.dot(hb, wsu_ref[...], preferred_element_type=F32)
    mid = (gt * jax.nn.sigmoid(gt) * up).astype(BF16)
    shared = jnp.dot(mid, wsd_ref[...], preferred_element_type=F32)
    out = x_ref[...] + gf_ref[0] * (routed + shared)
    if final:
        out = _rms(out, fg_ref[...])
    o_ref[...] = out


def _ffn_out(y3, gate3, h, x, g_f, wsg, wsu, wsd, final_g, final, S, tm=128):
    T, D = x.shape
    B = g_f.shape[0]
    DS = wsg.shape[1]
    tm = min(tm, S)
    full = lambda shape: pl.BlockSpec(shape, lambda i: (0,) * len(shape))
    return pl.pallas_call(
        functools.partial(_ffn_out_kernel, final=final),
        grid=(T // tm,),
        in_specs=[pl.BlockSpec((TOP_K, tm, D), lambda i: (0, i, 0)),
                  pl.BlockSpec((TOP_K, tm, 1), lambda i: (0, i, 0)),
                  pl.BlockSpec((tm, D), lambda i: (i, 0)),
                  pl.BlockSpec((tm, D), lambda i: (i, 0)),
                  pl.BlockSpec((1, 1, D), lambda i: (i * tm // S, 0, 0)),
                  full((D, DS)), full((D, DS)), full((DS, D)), full((1, D))],
        out_specs=pl.BlockSpec((tm, D), lambda i: (i, 0)),
        out_shape=jax.ShapeDtypeStruct((T, D), F32),
        compiler_params=_cparams(("parallel",)),
        name="ffn_combine",
    )(y3, gate3, h, x, g_f.reshape(B, 1, D), wsg, wsu, wsd, final_g.reshape(1, D))


def _layout_w_in(w):
    D = w.shape[0]
    sizes = (Q_LORA, KV_LORA, MLA_ROPE, 512, 512, 512, 512, HEAD_DIM, HEAD_DIM, IDX_HEADS * IDX_DIM, IDX_DIM, IDX_HEADS)
    offs = np.concatenate([[0], np.cumsum(sizes)])
    cq, ckv, kr, sbq, sbk, sbv, dsq, dsk, dsv, ixq, ixk, ixw = [w[:, offs[n]:offs[n + 1]] for n in range(12)]
    z = lambda n: jnp.zeros((D, n), w.dtype)
    out = jnp.concatenate([cq, sbq, sbk, sbv, dsq, ckv, dsk, dsv, ixq, kr, z(64), ixk, z(64), ixw, z(112), z(128)], axis=1)
    assert out.shape[1] == IN_PAD
    return out.astype(BF16)


def _layout_w_uq(w):
    K = w.shape[0]
    w3 = w.reshape(K, MLA_HEADS, MLA_NOPE + MLA_ROPE)
    w3 = jnp.concatenate([w3, jnp.zeros((K, MLA_HEADS, 64), w.dtype)], axis=2)
    return w3.reshape(K, MLA_HEADS * 256).astype(BF16)


def _layout_w_ukv(w):
    K = w.shape[0]
    w3 = w.reshape(K, MLA_HEADS, MLA_NOPE + HEAD_DIM)
    return jnp.concatenate([w3[:, :, :MLA_NOPE].reshape(K, -1), w3[:, :, MLA_NOPE:].reshape(K, -1)], axis=1).astype(BF16)


def _expert_blocks(eidx, pos, counts, T):
    M = T * TOP_K
    nb = M // MOE_ROWS + N_EXPERTS
    P = nb * MOE_ROWS
    padded = (counts + MOE_ROWS - 1) // MOE_ROWS * MOE_ROWS
    pend = jnp.cumsum(padded)
    pstart = pend - padded
    dest = pstart[eidx] + pos
    out_row = jnp.arange(TOP_K, dtype=I32)[:, None] * T + jnp.arange(T, dtype=I32)[None, :]
    slot_ids = jnp.zeros((P,), I32).at[dest.reshape(-1)].set(out_row.reshape(-1), unique_indices=True)
    blk_start = jnp.arange(nb, dtype=I32) * MOE_ROWS
    blk_exp = jnp.minimum(jnp.searchsorted(pend, blk_start, side="right"), N_EXPERTS - 1).astype(I32)
    blk_val = jnp.clip(pstart[blk_exp] + counts[blk_exp] - blk_start, 0, MOE_ROWS).astype(I32)
    return slot_ids, blk_exp, blk_val


def _mixers(x, S, B, positions_tabs, mod, l, w):
    sh_m, sc_m, g_m = mod[0], mod[1], mod[2]
    proj = _nmm(x, 0, x.shape[1], w["norm_mix_g"][l], w["w_in"][l], S, sc=sc_m, sh=sh_m)
    q_raw = _nmm(proj, C_CQ // Q_LORA, Q_LORA, w["g_cq"][l], w["w_uq"][l], S)
    kv_raw = _nmm(proj, C_CKV // KV_LORA, KV_LORA, w["g_ckv"][l], w["w_ukv"][l], S)
    tabs64, tabs128 = positions_tabs
    (qm, km, vm, sbq, sbk, sbv, dq, dk, dv, iq, ika, ikb, iw) = _prep(proj, q_raw, kv_raw, tabs64, tabs128)
    g_out = w["g_out"][l].reshape(1, -1)
    o_a = _mla(qm, km, vm, g_out, B, S)
    o_b = _sb(sbq, sbk, sbv, g_out, B, S)
    wt = iw[:, :IDX_HEADS].reshape(B, S, IDX_HEADS).transpose(0, 2, 1)
    o_c = _dsa(iq, ika, ikb, wt, dq, dk, dv.T, g_out, B, S)
    o = jnp.concatenate([o_a, o_b, o_c], axis=1)
    return _mm_res(o, w["w_o"][l], x, g_m, S)


def _ffn(x, S, B, mod, l, w, final_g, final):
    T, D = x.shape
    sh_f, sc_f, g_f = mod[3], mod[4], mod[5]
    h, eidx, gate, pos, cnt = _route(x, w["norm_ffn_g"][l], sc_f, sh_f, w["w_router"][l], w["b_router"][l], S)
    counts = cnt[:, 0].astype(I32)
    slot_ids, blk_exp, blk_val = _expert_blocks(eidx, pos, counts, T)
    y = _moe(h, slot_ids, blk_exp, blk_val, w["w_gate"], w["w_up"], w["w_down"], l)
    y3 = y.reshape(TOP_K, T, D)
    return _ffn_out(y3, gate.reshape(TOP_K, T, 1), h, x, g_f, w["ws_gate"][l], w["ws_up"][l], w["ws_down"][l],
                    final_g, final, S)


def kernel(x, c, positions, norm_mix_g, norm_ffn_g, w_ada, b_ada, w_in, g_cq, g_ckv, w_uq, w_ukv, g_out, w_o,
           w_router, b_router, w_gate, w_up, w_down, ws_gate, ws_up, ws_down, final_g):
    B, S, D = x.shape
    L = w_in.shape[0]
    T = B * S
    w = dict(norm_mix_g=norm_mix_g, norm_ffn_g=norm_ffn_g, g_cq=g_cq, g_ckv=g_ckv, g_out=g_out,
             w_in=jax.vmap(_layout_w_in)(w_in), w_uq=jax.vmap(_layout_w_uq)(w_uq), w_ukv=jax.vmap(_layout_w_ukv)(w_ukv),
             w_o=w_o.astype(BF16), w_router=w_router, b_router=b_router, w_gate=w_gate, w_up=w_up, w_down=w_down,
             ws_gate=ws_gate.astype(BF16), ws_up=ws_up.astype(BF16), ws_down=ws_down.astype(BF16))
    mod_all = _ada(c, w_ada, b_ada).reshape(L, B, 6, D)
    posf = positions.astype(F32).reshape(T, 1)
    tabs = (_rope_tables(posf, 64), _rope_tables(posf, 128))
    xt = x.reshape(T, D)
    for l in range(L):
        mod = [mod_all[l, :, n, :] for n in range(6)]
        xt = _mixers(xt, S, B, tabs, mod, l, w)
        xt = _ffn(xt, S, B, mod, l, w, final_g, l == L - 1)
    return xt.reshape(B, S, D)
```

```python
import functools

import jax
import jax.numpy as jnp
import numpy as np
from jax import lax
from jax.experimental import pallas as pl
from jax.experimental.pallas import tpu as pltpu

F32 = jnp.float32
BF16 = jnp.bfloat16
I32 = jnp.int32

HEAD_DIM = 128
MLA_HEADS = 8
SB_HEADS = 4
DSA_HEADS = 4
N_MIX_HEADS = MLA_HEADS + SB_HEADS + DSA_HEADS
Q_LORA = 512
KV_LORA = 256
MLA_NOPE = 128
MLA_ROPE = 64
IDX_HEADS = 16
IDX_DIM = 64
IDX_SCALE = (IDX_HEADS * IDX_DIM) ** -0.5
TOPK_MAX = 256
ROPE_THETA = 10000.0
N_EXPERTS = 64
TOP_K = 8
N_GROUPS = 8
TOPK_GROUPS = 4
ROUTED_SCALE = 2.5
EPS = 1e-6

LANES = 128
VMEM_LIMIT = 56 * 1024 * 1024
NEG_INF = float("-inf")
INT_MIN = -(2 ** 31)

C_CQ, C_SBQ, C_SBK, C_SBV, C_DSQ = 0, 512, 1024, 1536, 2048
C_CKV, C_DSK, C_DSV, C_IXQ = 2560, 2816, 2944, 3072
C_KR, C_IXK, C_IXW, IN_PAD = 4096, 4224, 4352, 4608

MOE_ROWS = 256


def _cparams(sem):
    return pltpu.CompilerParams(dimension_semantics=sem, vmem_limit_bytes=VMEM_LIMIT)


def _nt_dot(a, b):
    return lax.dot_general(a, b, (((1,), (1,)), ((), ())), preferred_element_type=F32)


def _rms(x, g):
    return x * lax.rsqrt(jnp.mean(x * x, axis=-1, keepdims=True) + EPS) * g


def _ada_kernel(c_ref, w_ref, b_ref, o_ref):
    c = c_ref[...]
    a = (c * jax.nn.sigmoid(c)).astype(BF16)
    o_ref[0] = jnp.dot(a, w_ref[0].astype(BF16), preferred_element_type=F32) + b_ref[0]


def _ada(c, w_ada, b_ada):
    L, D, N = w_ada.shape
    B = c.shape[0]
    tn = 1024
    return pl.pallas_call(
        _ada_kernel,
        grid=(L, N // tn),
        in_specs=[pl.BlockSpec((B, D), lambda l, j: (0, 0)),
                  pl.BlockSpec((1, D, tn), lambda l, j: (l, 0, j)),
                  pl.BlockSpec((1, 1, tn), lambda l, j: (l, 0, j))],
        out_specs=pl.BlockSpec((1, B, tn), lambda l, j: (l, 0, j)),
        out_shape=jax.ShapeDtypeStruct((L, B, N), F32),
        compiler_params=_cparams(("parallel", "parallel")),
        name="ada_mod",
    )(c, w_ada, b_ada.reshape(L, 1, N))


def _rope_tab_kernel(pos_ref, inv_ref, sgn_ref, cos_ref, sin_ref):
    ang = pos_ref[...] * inv_ref[...]
    cos_ref[...] = jnp.cos(ang)
    sin_ref[...] = jnp.sin(ang) * sgn_ref[...]


def _rope_tables(posf, d):
    T = posf.shape[0]
    half = d // 2
    inv = ROPE_THETA ** (-jnp.arange(0, d, 2, dtype=F32) / d)
    reps = LANES // half
    inv_t = jnp.tile(inv, reps).reshape(1, LANES)
    sgn = jnp.tile(jnp.concatenate([-jnp.ones((half,), F32), jnp.ones((half,), F32)]), LANES // d).reshape(1, LANES)
    tm = min(T, 1024)
    return pl.pallas_call(
        _rope_tab_kernel,
        grid=(T // tm,),
        in_specs=[pl.BlockSpec((tm, 1), lambda i: (i, 0)),
                  pl.BlockSpec((1, LANES), lambda i: (0, 0)),
                  pl.BlockSpec((1, LANES), lambda i: (0, 0))],
        out_specs=[pl.BlockSpec((tm, LANES), lambda i: (i, 0))] * 2,
        out_shape=[jax.ShapeDtypeStruct((T, LANES), F32)] * 2,
        compiler_params=_cparams(("parallel",)),
        name=f"rope_tab{d}",
    )(posf, inv_t, sgn)


def _rope64(x, cos, sin_s, first_half):
    rot = jnp.where(first_half, pltpu.roll(x, 96, 1), pltpu.roll(x, 32, 1))
    return x * cos + rot * sin_s


def _rope128(x, cos, sin_s):
    return x * cos + pltpu.roll(x, 64, 1) * sin_s


def _nmm_kernel(*refs, modulate):
    if modulate:
        x_ref, g_ref, sc_ref, sh_ref, w_ref, o_ref, h_s = refs
    else:
        x_ref, g_ref, w_ref, o_ref, h_s = refs

    @pl.when(pl.program_id(1) == 0)
    def _():
        h = _rms(x_ref[...], g_ref[...])
        if modulate:
            h = h * (1.0 + sc_ref[0]) + sh_ref[0]
        h_s[...] = h.astype(BF16)

    o_ref[...] = jnp.dot(h_s[...], w_ref[...], preferred_element_type=F32).astype(o_ref.dtype)


def _nmm(x, col_blk, K, g, w, S, sc=None, sh=None, tm=1024, tn=512, out_dtype=F32):
    T = x.shape[0]
    N = w.shape[1]
    tm = min(tm, S)
    tn = min(tn, N)
    modulate = sc is not None
    in_specs = [pl.BlockSpec((tm, K), lambda i, j: (i, col_blk)),
                pl.BlockSpec((1, K), lambda i, j: (0, 0))]
    args = [x, g.reshape(1, K)]
    if modulate:
        B = sc.shape[0]
        in_specs += [pl.BlockSpec((1, 1, K), lambda i, j: (i * tm // S, 0, 0))] * 2
        args += [sc.reshape(B, 1, K), sh.reshape(B, 1, K)]
    in_specs.append(pl.BlockSpec((K, tn), lambda i, j: (0, j)))
    args.append(w)
    return pl.pallas_call(
        functools.partial(_nmm_kernel, modulate=modulate),
        grid=(T // tm, N // tn),
        in_specs=in_specs,
        out_specs=pl.BlockSpec((tm, tn), lambda i, j: (i, j)),
        out_shape=jax.ShapeDtypeStruct((T, N), out_dtype),
        scratch_shapes=[pltpu.VMEM((tm, K), BF16)],
        compiler_params=_cparams(("parallel", "arbitrary")),
        name="norm_matmul",
    )(*args)


def _mm_res_kernel(a_ref, w_ref, r_ref, gt_ref, o_ref):
    acc = jnp.dot(a_ref[...], w_ref[...], preferred_element_type=F32)
    o_ref[...] = r_ref[...] + gt_ref[0] * acc


def _mm_res(a, w, res, gate, S, tm=1024, tn=512):
    T, K = a.shape
    N = w.shape[1]
    B = gate.shape[0]
    tm = min(tm, S)
    return pl.pallas_call(
        _mm_res_kernel,
        grid=(T // tm, N // tn),
        in_specs=[pl.BlockSpec((tm, K), lambda i, j: (i, 0)),
                  pl.BlockSpec((K, tn), lambda i, j: (0, j)),
                  pl.BlockSpec((tm, tn), lambda i, j: (i, j)),
                  pl.BlockSpec((1, 1, tn), lambda i, j: (i * tm // S, 0, j))],
        out_specs=pl.BlockSpec((tm, tn), lambda i, j: (i, j)),
        out_shape=jax.ShapeDtypeStruct((T, N), F32),
        compiler_params=_cparams(("parallel", "parallel")),
        name="out_proj_residual",
    )(a, w, res, gate.reshape(B, 1, N))


def _prep_kernel(p_ref, q_ref, kv_ref, c64_ref, s64_ref, c128_ref, s128_ref,
                 qm_ref, km_ref, vm_ref, sbq_ref, sbk_ref, sbv_ref,
                 dq_ref, dk_ref, dv_ref, iq_ref, ika_ref, ikb_ref, iw_ref):
    c64, s64 = c64_ref[...], s64_ref[...]
    c128, s128 = c128_ref[...], s128_ref[...]
    lane = lax.broadcasted_iota(I32, c64.shape, 1)
    first = (lane % 64) < 32

    def sl(ref, off, width=LANES):
        return ref[:, off:off + width]

    kr = _rope64(sl(p_ref, C_KR), c64, s64, first).astype(BF16)
    for h in range(MLA_HEADS):
        qm_ref[:, 256 * h:256 * h + 128] = sl(q_ref, 256 * h).astype(BF16)
        qm_ref[:, 256 * h + 128:256 * h + 256] = _rope64(sl(q_ref, 256 * h + 128), c64, s64, first).astype(BF16)
        km_ref[:, 256 * h:256 * h + 128] = sl(kv_ref, 128 * h).astype(BF16)
        km_ref[:, 256 * h + 128:256 * h + 256] = kr
    vm_ref[...] = kv_ref[:, MLA_HEADS * 128:].astype(BF16)
    sbq_ref[...] = sl(p_ref, C_SBQ, 512).astype(BF16)
    sbk_ref[...] = sl(p_ref, C_SBK, 512).astype(BF16)
    sbv_ref[...] = sl(p_ref, C_SBV, 512).astype(BF16)
    for h in range(DSA_HEADS):
        dq_ref[:, 128 * h:128 * h + 128] = _rope128(sl(p_ref, C_DSQ + 128 * h), c128, s128).astype(BF16)
    dk_ref[...] = _rope128(sl(p_ref, C_DSK), c128, s128).astype(BF16)
    dv_ref[...] = sl(p_ref, C_DSV).astype(BF16)
    for j in range(IDX_HEADS // 2):
        iq_ref[:, 128 * j:128 * j + 128] = _rope64(sl(p_ref, C_IXQ + 128 * j), c64, s64, first).astype(BF16)
    ik = _rope64(sl(p_ref, C_IXK), c64, s64, first)
    ika_ref[...] = ik.astype(BF16)
    ikb_ref[...] = pltpu.roll(ik, 64, 1).astype(BF16)
    iw_ref[...] = sl(p_ref, C_IXW)


def _prep(proj, q_raw, kv_raw, tabs64, tabs128, tm=256):
    T = proj.shape[0]
    row = lambda w: pl.BlockSpec((tm, w), lambda i: (i, 0))
    widths = [2048, 2048, 1024, 512, 512, 512, 512, 128, 128, 1024, 128, 128]
    out_shape = [jax.ShapeDtypeStruct((T, w), BF16) for w in widths] + [jax.ShapeDtypeStruct((T, LANES), F32)]
    return pl.pallas_call(
        _prep_kernel,
        grid=(T // tm,),
        in_specs=[row(IN_PAD), row(2048), row(2048), row(LANES), row(LANES), row(LANES), row(LANES)],
        out_specs=[row(w) for w in widths] + [row(LANES)],
        out_shape=out_shape,
        compiler_params=_cparams(("parallel",)),
        name="mixer_prep",
    )(proj, q_raw, kv_raw, *tabs64, *tabs128)


def _head_out(o, g):
    return _rms(o, g).astype(BF16)


def _mla_kernel(q_ref, k_ref, v_ref, g_ref, o_ref, m_s, l_s, acc_s, *, t, scale):
    i = pl.program_id(2)
    q = q_ref[...]
    m_s[...] = jnp.full(m_s.shape, NEG_INF, F32)
    l_s[...] = jnp.zeros(l_s.shape, F32)
    acc_s[...] = jnp.zeros(acc_s.shape, F32)

    def step(c, masked):
        k0 = pl.multiple_of(c * t, t)
        s = _nt_dot(q, k_ref[pl.ds(k0, t), :]) * scale
        if masked:
            row = lax.broadcasted_iota(I32, (t, t), 0)
            col = lax.broadcasted_iota(I32, (t, t), 1)
            s = jnp.where(col <= row, s, NEG_INF)
        m_prev = m_s[...]
        m_new = jnp.maximum(m_prev, jnp.max(s, axis=1, keepdims=True))
        alpha = jnp.exp(m_prev - m_new)
        p = jnp.exp(s - m_new)
        l_s[...] = alpha * l_s[...] + jnp.sum(p, axis=1, keepdims=True)
        acc_s[...] = alpha * acc_s[...] + jnp.dot(p.astype(BF16), v_ref[pl.ds(k0, t), :], preferred_element_type=F32)
        m_s[...] = m_new

    def body(c, carry):
        step(c, False)
        return carry

    lax.fori_loop(0, i, body, 0)
    step(i, True)
    o_ref[...] = _head_out(acc_s[...] / l_s[...], g_ref[...])


def _mla(qm, km, vm, g_out, B, S, t=512):
    T = qm.shape[0]
    t = min(t, S)
    nq = S // t
    H = MLA_HEADS
    return pl.pallas_call(
        functools.partial(_mla_kernel, t=t, scale=(MLA_NOPE + MLA_ROPE) ** -0.5),
        grid=(B, H, nq),
        in_specs=[pl.BlockSpec((t, 256), lambda b, h, i: (b * nq + i, h)),
                  pl.BlockSpec((S, 256), lambda b, h, i: (b, h)),
                  pl.BlockSpec((S, 128), lambda b, h, i: (b, h)),
                  pl.BlockSpec((1, 128), lambda b, h, i: (0, h))],
        out_specs=pl.BlockSpec((t, 128), lambda b, h, i: (b * nq + i, h)),
        out_shape=jax.ShapeDtypeStruct((T, H * HEAD_DIM), BF16),
        scratch_shapes=[pltpu.VMEM((t, 1), F32), pltpu.VMEM((t, 1), F32), pltpu.VMEM((t, 128), F32)],
        compiler_params=_cparams(("parallel", "parallel", "arbitrary")),
        name="mla_attention",
    )(qm, km, vm, g_out)


def _sb_kernel(q_ref, k_ref, v_ref, g_ref, tri_ref, o_ref, carry_s, acc_s, *, t, scale):
    i = pl.program_id(2)
    q = q_ref[...]
    tri = tri_ref[...]
    carry_s[...] = jnp.zeros(carry_s.shape, F32)
    acc_s[...] = jnp.zeros(acc_s.shape, F32)

    def step(c, masked):
        k0 = pl.multiple_of(c * t, t)
        z = _nt_dot(q, k_ref[pl.ds(k0, t), :]) * scale
        sp = jnp.log(1.0 + jnp.exp(-jnp.abs(z)))
        log_beta = jnp.minimum(z, 0.0) - sp
        log_keep = jnp.minimum(-z, 0.0) - sp
        if masked:
            row = lax.broadcasted_iota(I32, (t, t), 0)
            col = lax.broadcasted_iota(I32, (t, t), 1)
            strict = col < row
            log_keep = jnp.where(strict, log_keep, 0.0)
        hi = log_keep.astype(BF16)
        lo = (log_keep - hi.astype(F32)).astype(BF16)
        suffix = jnp.dot(hi, tri, preferred_element_type=F32) + jnp.dot(lo, tri, preferred_element_type=F32)
        a = jnp.exp(log_beta + suffix + carry_s[...])
        if masked:
            a = jnp.where(strict, a, 0.0)
        acc_s[...] += jnp.dot(a.astype(BF16), v_ref[pl.ds(k0, t), :], preferred_element_type=F32)
        carry_s[...] += jnp.sum(log_keep, axis=1, keepdims=True)

    step(i, True)

    def body(j, carry):
        step(i - 1 - j, False)
        return carry

    lax.fori_loop(0, i, body, 0)
    o_ref[...] = _head_out(acc_s[...], g_ref[...])


def _sb(sbq, sbk, sbv, g_out, B, S, t=256):
    T = sbq.shape[0]
    t = min(t, S)
    nq = S // t
    H = SB_HEADS
    r = np.arange(t)
    tri = jnp.asarray((r[:, None] > r[None, :]).astype(np.float32), BF16)
    return pl.pallas_call(
        functools.partial(_sb_kernel, t=t, scale=HEAD_DIM ** -0.5),
        grid=(B, H, nq),
        in_specs=[pl.BlockSpec((t, 128), lambda b, h, i: (b * nq + i, h)),
                  pl.BlockSpec((S, 128), lambda b, h, i: (b, h)),
                  pl.BlockSpec((S, 128), lambda b, h, i: (b, h)),
                  pl.BlockSpec((1, 128), lambda b, h, i: (0, MLA_HEADS + h)),
                  pl.BlockSpec((t, t), lambda b, h, i: (0, 0))],
        out_specs=pl.BlockSpec((t, 128), lambda b, h, i: (b * nq + i, h)),
        out_shape=jax.ShapeDtypeStruct((T, H * HEAD_DIM), BF16),
        scratch_shapes=[pltpu.VMEM((t, 1), F32), pltpu.VMEM((t, 128), F32)],
        compiler_params=_cparams(("parallel", "parallel", "arbitrary")),
        name="stickbreak_attention",
    )(sbq, sbk, sbv, g_out, tri)


def _dsa_kernel(iq_ref, ika_ref, ikb_ref, wt_ref, dq_ref, dk_ref, vt_ref, g_ref, o_ref,
                key_s, m_s, l_s, acc_s, *, tq, tk, n_sel, scale):
    i = pl.program_id(1)
    nch = (i * tq + tq + tk - 1) // tk
    qpos = i * tq + lax.broadcasted_iota(I32, (1, tq), 1)
    kiota = lax.broadcasted_iota(I32, (tk, 1), 0)
    wt = wt_ref[0]
    half = IDX_HEADS // 2

    def score_chunk(c, carry):
        k0 = pl.multiple_of(c * tk, tk)
        kk = jnp.concatenate([ika_ref[pl.ds(k0, tk), :], ikb_ref[pl.ds(k0, tk), :]], axis=0)
        score = jnp.zeros((tk, tq), F32)
        for j in range(half):
            r = jnp.maximum(_nt_dot(kk, iq_ref[:, 128 * j:128 * j + 128]), 0.0)
            score = score + r[:tk] * wt[2 * j:2 * j + 1, :] + r[tk:] * wt[2 * j + 1:2 * j + 2, :]
        score = score * IDX_SCALE
        score = jnp.where(k0 + kiota <= qpos, score, NEG_INF)
        bits = pltpu.bitcast(score, I32)
        key_s[pl.ds(k0, tk), :] = jnp.where(bits < 0, bits ^ jnp.int32(0x7FFFFFFF), bits)
        return carry

    lax.fori_loop(0, nch, score_chunk, 0)

    def count(pred):
        def body(c, acc):
            k0 = pl.multiple_of(c * tk, tk)
            hit = pred(key_s[pl.ds(k0, tk), :], k0 + kiota).astype(I32)
            return acc + jnp.sum(hit.reshape(tk // 8, 8, tq), axis=0)
        part = lax.fori_loop(0, nch, body, jnp.zeros((8, tq), I32))
        return jnp.sum(part, axis=0, keepdims=True)

    c0 = count(lambda k, idx: k >= 0)
    thr = jnp.where(c0 >= n_sel, jnp.int32(0), jnp.int32(INT_MIN))

    def vbit(b, thr):
        cand = thr + jnp.left_shift(jnp.int32(1), 30 - b)
        cnt = count(lambda k, idx: k >= cand)
        return jnp.where(cnt >= n_sel, cand, thr)

    thr = lax.fori_loop(0, 31, vbit, thr)
    need = n_sel - count(lambda k, idx: k > thr)

    idx_bits = max(1, int(np.ceil(np.log2(key_s.shape[0]))))

    def ibit(b, bound):
        cand = bound + jnp.left_shift(jnp.int32(1), idx_bits - 1 - b)
        cnt = count(lambda k, idx: (k == thr) & (idx < cand))
        return jnp.where(cnt < need, cand, bound)

    bound = lax.fori_loop(0, idx_bits, ibit, jnp.zeros((1, tq), I32))

    qc = jnp.concatenate([dq_ref[:, 128 * h:128 * h + 128] for h in range(DSA_HEADS)], axis=0)
    m_s[...] = jnp.full(m_s.shape, NEG_INF, F32)
    l_s[...] = jnp.zeros(l_s.shape, F32)
    acc_s[...] = jnp.zeros(acc_s.shape, F32)

    def attn_chunk(c, carry):
        k0 = pl.multiple_of(c * tk, tk)
        key = key_s[pl.ds(k0, tk), :]
        idx = k0 + kiota
        sel = ((key > thr) | ((key == thr) & (idx <= bound))) & (idx <= qpos)
        bias = jnp.where(sel, 0.0, NEG_INF)
        s = _nt_dot(dk_ref[pl.ds(k0, tk), :], qc) * scale + jnp.concatenate([bias] * DSA_HEADS, axis=1)
        m_prev = m_s[...]
        m_new = jnp.maximum(m_prev, jnp.max(s, axis=0, keepdims=True))
        m_safe = jnp.where(m_new == NEG_INF, 0.0, m_new)
        alpha = jnp.exp(m_prev - m_safe)
        p = jnp.exp(s - m_safe)
        l_s[...] = alpha * l_s[...] + jnp.sum(p, axis=0, keepdims=True)
        acc_s[...] = alpha * acc_s[...] + jnp.dot(vt_ref[:, pl.ds(k0, tk)], p.astype(BF16),
                                                  preferred_element_type=F32)
        m_s[...] = m_new
        return carry

    lax.fori_loop(0, nch, attn_chunk, 0)
    ot = acc_s[...] / l_s[...]
    for h in range(DSA_HEADS):
        o_ref[:, 128 * h:128 * h + 128] = _head_out(ot[:, h * tq:(h + 1) * tq].T, g_ref[:, 128 * h:128 * h + 128])


def _dsa(iq, ika, ikb, wt, dq, dk, vt, g_out, B, S, tq=128, tk=256):
    T = iq.shape[0]
    tk = min(tk, S)
    nq = S // tq
    H = DSA_HEADS
    n_sel = min(TOPK_MAX, S // 4)
    g_c = g_out[:, (MLA_HEADS + SB_HEADS) * HEAD_DIM:]
    return pl.pallas_call(
        functools.partial(_dsa_kernel, tq=tq, tk=tk, n_sel=n_sel, scale=HEAD_DIM ** -0.5),
        grid=(B, nq),
        in_specs=[pl.BlockSpec((tq, IDX_HEADS * IDX_DIM), lambda b, i: (b * nq + i, 0)),
                  pl.BlockSpec((S, 128), lambda b, i: (b, 0)),
                  pl.BlockSpec((S, 128), lambda b, i: (b, 0)),
                  pl.BlockSpec((1, IDX_HEADS, tq), lambda b, i: (b, 0, i)),
                  pl.BlockSpec((tq, H * 128), lambda b, i: (b * nq + i, 0)),
                  pl.BlockSpec((S, 128), lambda b, i: (b, 0)),
                  pl.BlockSpec((128, S), lambda b, i: (0, b)),
                  pl.BlockSpec((1, H * 128), lambda b, i: (0, 0))],
        out_specs=pl.BlockSpec((tq, H * 128), lambda b, i: (b * nq + i, 0)),
        out_shape=jax.ShapeDtypeStruct((T, H * HEAD_DIM), BF16),
        scratch_shapes=[pltpu.VMEM((S, tq), I32), pltpu.VMEM((1, H * tq), F32), pltpu.VMEM((1, H * tq), F32),
                        pltpu.VMEM((128, H * tq), F32)],
        compiler_params=_cparams(("parallel", "arbitrary")),
        name="dsa_attention",
    )(iq, ika, ikb, wt, dq, dk, vt, g_c)


def _route_kernel(x_ref, g_ref, sc_ref, sh_ref, wr_ref, br_ref, tri_ref,
                  h_ref, eidx_ref, gate_ref, pos_ref, cnt_ref, run_s):
    @pl.when(pl.program_id(0) == 0)
    def _():
        run_s[...] = jnp.zeros(run_s.shape, F32)

    h = _rms(x_ref[...], g_ref[...]) * (1.0 + sc_ref[0]) + sh_ref[0]
    hb = h.astype(BF16)
    h_ref[...] = hb
    tm = h.shape[0]
    logits = _nt_dot(wr_ref[...], hb)
    scores = jax.nn.sigmoid(logits)
    biased = scores + br_ref[...]
    gsz = N_EXPERTS // N_GROUPS
    b3 = biased.reshape(N_GROUPS, gsz, tm)
    m1 = jnp.max(b3, axis=1, keepdims=True)
    n1 = jnp.sum((b3 == m1).astype(F32), axis=1, keepdims=True)
    m2 = jnp.max(jnp.where(b3 < m1, b3, NEG_INF), axis=1, keepdims=True)
    grp = (m1 + jnp.where(n1 >= 2.0, m1, m2)).reshape(N_GROUPS, tm)
    gi = lax.broadcasted_iota(I32, (N_GROUPS, 1), 0)
    grank = jnp.zeros((N_GROUPS, tm), F32)
    for g in range(N_GROUPS):
        rowv = grp[g:g + 1, :]
        grank = grank + jnp.where((rowv > grp) | ((rowv == grp) & (g < gi)), 1.0, 0.0)
    gmask = grank.reshape(N_GROUPS, 1, tm) < float(TOPK_GROUPS)
    masked = jnp.where(gmask, b3, NEG_INF).reshape(N_EXPERTS, tm)
    ei = lax.broadcasted_iota(I32, (N_EXPERTS, 1), 0)
    rank = jnp.zeros((N_EXPERTS, tm), F32)
    for e in range(N_EXPERTS):
        rowv = masked[e:e + 1, :]
        rank = rank + jnp.where((rowv > masked) | ((rowv == masked) & (e < ei)), 1.0, 0.0)
    sel = rank < float(TOP_K)
    selm = sel.astype(F32)
    gsum = jnp.sum(scores * selm, axis=0, keepdims=True)
    gate = scores * selm / gsum * ROUTED_SCALE
    within = jnp.dot(selm.astype(BF16), tri_ref[...], preferred_element_type=F32)
    posf = within + run_s[...]
    run_s[...] += jnp.sum(selm, axis=1, keepdims=True)
    cnt_ref[...] = jnp.broadcast_to(run_s[...], cnt_ref.shape)
    eif = ei.astype(F32)
    for k in range(TOP_K):
        onek = rank == float(k)
        eidx_ref[k:k + 1, :] = jnp.sum(jnp.where(onek, eif, 0.0), axis=0, keepdims=True).astype(I32)
        gate_ref[k:k + 1, :] = jnp.sum(jnp.where(onek, gate, 0.0), axis=0, keepdims=True)
        pos_ref[k:k + 1, :] = jnp.sum(jnp.where(onek, posf, 0.0), axis=0, keepdims=True).astype(I32)


def _route(x, g, sc, sh, w_router, b_router, S, tm=512):
    T, D = x.shape
    B = sc.shape[0]
    tm = min(tm, S)
    r = np.arange(tm)
    tri = jnp.asarray((r[:, None] < r[None, :]).astype(np.float32), BF16)
    full = lambda shape: pl.BlockSpec(shape, lambda i: (0,) * len(shape))
    return pl.pallas_call(
        _route_kernel,
        grid=(T // tm,),
        in_specs=[pl.BlockSpec((tm, D), lambda i: (i, 0)),
                  full((1, D)),
                  pl.BlockSpec((1, 1, D), lambda i: (i * tm // S, 0, 0)),
                  pl.BlockSpec((1, 1, D), lambda i: (i * tm // S, 0, 0)),
                  full((N_EXPERTS, D)),
                  full((N_EXPERTS, 1)),
                  full((tm, tm))],
        out_specs=[pl.BlockSpec((tm, D), lambda i: (i, 0)),
                   pl.BlockSpec((TOP_K, tm), lambda i: (0, i)),
                   pl.BlockSpec((TOP_K, tm), lambda i: (0, i)),
                   pl.BlockSpec((TOP_K, tm), lambda i: (0, i)),
                   full((N_EXPERTS, LANES))],
        out_shape=[jax.ShapeDtypeStruct((T, D), BF16),
                   jax.ShapeDtypeStruct((TOP_K, T), I32),
                   jax.ShapeDtypeStruct((TOP_K, T), F32),
                   jax.ShapeDtypeStruct((TOP_K, T), I32),
                   jax.ShapeDtypeStruct((N_EXPERTS, LANES), F32)],
        scratch_shapes=[pltpu.VMEM((N_EXPERTS, 1), F32)],
        compiler_params=_cparams(("arbitrary",)),
        name="ffn_norm_route",
    )(x, g.reshape(1, D), sc.reshape(B, 1, D), sh.reshape(B, 1, D),
      w_router.T.astype(BF16), b_router.reshape(N_EXPERTS, 1), tri)


def _moe_kernel(bexp_ref, bval_ref, slot_hbm, h_hbm, wg_ref, wu_ref, wd_ref, y_hbm,
                xbuf, x_s, ybuf, wg_s, wu_s, wd_s, idx_s, isem, gsem, ssem, *, n_tok):
    b = pl.program_id(0)
    nb = pl.num_programs(0)
    rows = x_s.shape[0]
    half = x_s.shape[1] // 2
    YS = 2 * half // LANES
    cur = b % 2

    def bval(blk):
        return bval_ref[jnp.clip(blk, 0, nb - 1)]

    def idx_copy(blk):
        ring = pl.multiple_of((blk % 3) * rows, rows)
        return pltpu.make_async_copy(slot_hbm.at[blk], idx_s.at[pl.ds(ring, rows)], isem.at[blk % 3])

    def gather(blk, buf):
        ring = (blk % 3) * rows

        def body(r, carry):
            tok = idx_s[ring + r] & (n_tok - 1)
            pltpu.make_async_copy(h_hbm.at[pl.ds(pl.multiple_of(tok * 8, 8), 8), :],
                                  xbuf.at[buf, pl.ds(pl.multiple_of(r * 8, 8), 8), :], gsem.at[buf]).start()
            return carry
        lax.fori_loop(0, rows, body, 0, unroll=8)

    def scatter(blk, buf):
        ring = (blk % 3) * rows

        def one(r):
            dst = idx_s[ring + r]
            pltpu.make_async_copy(ybuf.at[buf, pl.ds(pl.multiple_of(r * YS, YS), YS), :],
                                  y_hbm.at[pl.ds(pl.multiple_of(dst * YS, YS), YS), :], ssem.at[buf]).start()

        def group(g, carry):
            for j in range(8):
                one(g * 8 + j)
            return carry

        def single(r, carry):
            one(r)
            return carry
        n = bval(blk)
        lax.fori_loop(0, n // 8, group, 0)
        lax.fori_loop(n // 8 * 8, n, single, 0)

    def wait_gather(buf):
        pltpu.make_async_copy(xbuf.at[buf], xbuf.at[buf], gsem.at[buf]).wait()

    def unpack(buf):
        for s in range(8):
            w = xbuf[buf, pl.ds(s, rows, stride=8), :]
            x_s[:, 128 * s:128 * s + 128] = pltpu.bitcast(w << 16, F32).astype(BF16)
            x_s[:, half + 128 * s:half + 128 * s + 128] = pltpu.bitcast(w & jnp.uint32(0xFFFF0000), F32).astype(BF16)

    def wait_scatter(blk, buf):
        n = pl.multiple_of(bval(blk) * YS, YS)
        pltpu.make_async_copy(ybuf.at[buf, pl.ds(0, n), :], ybuf.at[buf, pl.ds(0, n), :], ssem.at[buf]).wait()

    @pl.when(b == 0)
    def _():
        idx_copy(0).start()
        idx_copy(0).wait()
        gather(0, 0)

        @pl.when(nb > 1)
        def _():
            idx_copy(1).start()

    @pl.when(b + 2 < nb)
    def _():
        idx_copy(b + 2).start()

    @pl.when(b + 1 < nb)
    def _():
        idx_copy(b + 1).wait()

        @pl.when(bval(b + 1) > 0)
        def _():
            gather(b + 1, 1 - cur)

    @pl.when((b == 0) | (bexp_ref[b] != bexp_ref[jnp.maximum(b - 1, 0)]))
    def _():
        wg_s[...] = wg_ref[0, 0].astype(BF16)
        wu_s[...] = wu_ref[0, 0].astype(BF16)
        wd_s[...] = wd_ref[0, 0].astype(BF16)

    @pl.when((b >= 2) & (bval(b - 2) > 0))
    def _():
        wait_scatter(b - 2, cur)

    @pl.when((b == 0) | (bval(b) > 0))
    def _():
        wait_gather(cur)

    @pl.when(bval(b) > 0)
    def _():
        unpack(cur)
        x = x_s[...]
        gt = jnp.dot(x, wg_s[...], preferred_element_type=F32)
        up = jnp.dot(x, wu_s[...], preferred_element_type=F32)
        mid = (gt * jax.nn.sigmoid(gt) * up).astype(BF16)
        y = jnp.dot(mid, wd_s[...], preferred_element_type=F32)
        for s in range(YS):
            ybuf[cur, pl.ds(s, rows, stride=YS), :] = y[:, LANES * s:LANES * s + LANES]
        scatter(b, cur)

    @pl.when(b == nb - 1)
    def _():
        @pl.when(bval(b) > 0)
        def _():
            wait_scatter(b, cur)

        @pl.when((nb > 1) & (bval(b - 1) > 0))
        def _():
            wait_scatter(b - 1, 1 - cur)


def _pack_rows(hb):
    T, D = hb.shape
    assert D == 2 * 8 * LANES
    bits = lax.bitcast_convert_type(hb, jnp.uint16).astype(jnp.uint32)
    return (bits[:, :D // 2] | (bits[:, D // 2:] << 16)).reshape(T * 8, LANES)


def _moe(hp, slot_ids, blk_exp, blk_val, w_gate, w_up, w_down, l):
    T = hp.shape[0] // 8
    D = w_gate.shape[2]
    assert T & (T - 1) == 0
    DE = w_gate.shape[3]
    nb = blk_exp.shape[0]
    wspec = lambda shape: pl.BlockSpec((1, 1) + shape, lambda b, bexp, bval: (l, bexp[b], 0, 0))
    grid_spec = pltpu.PrefetchScalarGridSpec(
        num_scalar_prefetch=2,
        grid=(nb,),
        in_specs=[pl.BlockSpec(memory_space=pl.ANY), pl.BlockSpec(memory_space=pl.ANY),
                  wspec((D, DE)), wspec((D, DE)), wspec((DE, D))],
        out_specs=pl.BlockSpec(memory_space=pl.ANY),
        scratch_shapes=[pltpu.VMEM((2, MOE_ROWS * 8, LANES), jnp.uint32), pltpu.VMEM((MOE_ROWS, D), BF16),
                        pltpu.VMEM((2, MOE_ROWS * D // LANES, LANES), F32),
                        pltpu.VMEM((D, DE), BF16), pltpu.VMEM((D, DE), BF16), pltpu.VMEM((DE, D), BF16),
                        pltpu.SMEM((3 * MOE_ROWS,), I32),
                        pltpu.SemaphoreType.DMA((3,)), pltpu.SemaphoreType.DMA((2,)), pltpu.SemaphoreType.DMA((2,))],
    )
    return pl.pallas_call(
        functools.partial(_moe_kernel, n_tok=T),
        grid_spec=grid_spec,
        out_shape=jax.ShapeDtypeStruct((TOP_K * T * D // LANES, LANES), F32),
        compiler_params=_cparams(("arbitrary",)),
        name="moe_experts",
    )(blk_exp, blk_val, slot_ids.reshape(nb, MOE_ROWS), hp, w_gate, w_up, w_down)


def _ffn_out_kernel(y_ref, gk_ref, h_ref, x_ref, gf_ref, wsg_ref, wsu_ref, wsd_ref, fg_ref, o_ref, *, final):
    tm, D = x_ref.shape
    ys = D // LANES
    gates = [gk_ref[k] for k in range(TOP_K)]
    chunks = []
    for s in range(ys):
        acc = y_ref[0, pl.ds(s, tm, stride=ys), :] * gates[0]
        for k in range(1, TOP_K):
            acc = acc + y_ref[k, pl.ds(s, tm, stride=ys), :] * gates[k]
        chunks.append(acc)
    routed = jnp.concatenate(chunks, axis=1)
    hb = h_ref[...]
    gt = jnp.dot(hb, wsg_ref[...], preferred_element_type=F32)
    up = jnp.dot(hb, wsu_ref[...], preferred_element_type=F32)
    mid = (gt * jax.nn.sigmoid(gt) * up).astype(BF16)
    shared = jnp.dot(mid, wsd_ref[...], preferred_element_type=F32)
    out = x_ref[...] + gf_ref[0] * (routed + shared)
    if final:
        out = _rms(out, fg_ref[...])
    o_ref[...] = out


def _ffn_out(y3, gate3, h, x, g_f, wsg, wsu, wsd, final_g, final, S, tm=128):
    T, D = x.shape
    B = g_f.shape[0]
    DS = wsg.shape[1]
    tm = min(tm, S)
    full = lambda shape: pl.BlockSpec(shape, lambda i: (0,) * len(shape))
    return pl.pallas_call(
        functools.partial(_ffn_out_kernel, final=final),
        grid=(T // tm,),
        in_specs=[pl.BlockSpec((TOP_K, tm * D // LANES, LANES), lambda i: (0, i, 0)),
                  pl.BlockSpec((TOP_K, tm, 1), lambda i: (0, i, 0)),
                  pl.BlockSpec((tm, D), lambda i: (i, 0)),
                  pl.BlockSpec((tm, D), lambda i: (i, 0)),
                  pl.BlockSpec((1, 1, D), lambda i: (i * tm // S, 0, 0)),
                  full((D, DS)), full((D, DS)), full((DS, D)), full((1, D))],
        out_specs=pl.BlockSpec((tm, D), lambda i: (i, 0)),
        out_shape=jax.ShapeDtypeStruct((T, D), F32),
        compiler_params=_cparams(("parallel",)),
        name="ffn_combine",
    )(y3, gate3, h, x, g_f.reshape(B, 1, D), wsg, wsu, wsd, final_g.reshape(1, D))


def _layout_w_in(w):
    D = w.shape[0]
    sizes = (Q_LORA, KV_LORA, MLA_ROPE, 512, 512, 512, 512, HEAD_DIM, HEAD_DIM, IDX_HEADS * IDX_DIM, IDX_DIM, IDX_HEADS)
    offs = np.concatenate([[0], np.cumsum(sizes)])
    cq, ckv, kr, sbq, sbk, sbv, dsq, dsk, dsv, ixq, ixk, ixw = [w[:, offs[n]:offs[n + 1]] for n in range(12)]
    z = lambda n: jnp.zeros((D, n), w.dtype)
    out = jnp.concatenate([cq, sbq, sbk, sbv, dsq, ckv, dsk, dsv, ixq, kr, z(64), ixk, z(64), ixw, z(112), z(128)], axis=1)
    assert out.shape[1] == IN_PAD
    return out.astype(BF16)


def _layout_w_uq(w):
    K = w.shape[0]
    w3 = w.reshape(K, MLA_HEADS, MLA_NOPE + MLA_ROPE)
    w3 = jnp.concatenate([w3, jnp.zeros((K, MLA_HEADS, 64), w.dtype)], axis=2)
    return w3.reshape(K, MLA_HEADS * 256).astype(BF16)


def _layout_w_ukv(w):
    K = w.shape[0]
    w3 = w.reshape(K, MLA_HEADS, MLA_NOPE + HEAD_DIM)
    return jnp.concatenate([w3[:, :, :MLA_NOPE].reshape(K, -1), w3[:, :, MLA_NOPE:].reshape(K, -1)], axis=1).astype(BF16)


def _expert_blocks(eidx, pos, counts, T):
    M = T * TOP_K
    nb = M // MOE_ROWS + N_EXPERTS
    P = nb * MOE_ROWS
    padded = (counts + MOE_ROWS - 1) // MOE_ROWS * MOE_ROWS
    pend = jnp.cumsum(padded)
    pstart = pend - padded
    experts = jnp.arange(N_EXPERTS, dtype=I32)
    dest = jnp.sum(jnp.where(eidx[:, :, None] == experts, pstart, 0), axis=-1) + pos
    out_row = jnp.arange(TOP_K, dtype=I32)[:, None] * T + jnp.arange(T, dtype=I32)[None, :]
    slot_ids = jnp.zeros((P,), I32).at[dest.reshape(-1)].set(out_row.reshape(-1), unique_indices=True)
    blk_start = jnp.arange(nb, dtype=I32) * MOE_ROWS
    blk_exp = jnp.minimum(jnp.sum((pend[None, :] <= blk_start[:, None]).astype(I32), axis=1), N_EXPERTS - 1)
    real_end = jnp.sum(jnp.where(blk_exp[:, None] == experts, pstart + counts, 0), axis=-1)
    blk_val = jnp.clip(real_end - blk_start, 0, MOE_ROWS).astype(I32)
    return slot_ids, blk_exp, blk_val


def _mixers(x, S, B, positions_tabs, mod, l, w):
    sh_m, sc_m, g_m = mod[0], mod[1], mod[2]
    proj = _nmm(x, 0, x.shape[1], w["norm_mix_g"][l], w["w_in"][l], S, sc=sc_m, sh=sh_m)
    q_raw = _nmm(proj, C_CQ // Q_LORA, Q_LORA, w["g_cq"][l], w["w_uq"][l], S)
    kv_raw = _nmm(proj, C_CKV // KV_LORA, KV_LORA, w["g_ckv"][l], w["w_ukv"][l], S)
    tabs64, tabs128 = positions_tabs
    (qm, km, vm, sbq, sbk, sbv, dq, dk, dv, iq, ika, ikb, iw) = _prep(proj, q_raw, kv_raw, tabs64, tabs128)
    g_out = w["g_out"][l].reshape(1, -1)
    o_a = _mla(qm, km, vm, g_out, B, S)
    o_b = _sb(sbq, sbk, sbv, g_out, B, S)
    wt = iw[:, :IDX_HEADS].reshape(B, S, IDX_HEADS).transpose(0, 2, 1)
    o_c = _dsa(iq, ika, ikb, wt, dq, dk, dv.T, g_out, B, S)
    o = jnp.concatenate([o_a, o_b, o_c], axis=1)
    return _mm_res(o, w["w_o"][l], x, g_m, S)


def _ffn(x, S, B, mod, l, w, final_g, final):
    T, D = x.shape
    sh_f, sc_f, g_f = mod[3], mod[4], mod[5]
    h, eidx, gate, pos, cnt = _route(x, w["norm_ffn_g"][l], sc_f, sh_f, w["w_router"][l], w["b_router"][l], S)
    counts = cnt[:, 0].astype(I32)
    slot_ids, blk_exp, blk_val = _expert_blocks(eidx, pos, counts, T)
    y = _moe(_pack_rows(h), slot_ids, blk_exp, blk_val, w["w_gate"], w["w_up"], w["w_down"], l)
    y3 = y.reshape(TOP_K, T * D // LANES, LANES)
    return _ffn_out(y3, gate.reshape(TOP_K, T, 1), h, x, g_f, w["ws_gate"][l], w["ws_up"][l], w["ws_down"][l],
                    final_g, final, S)


def kernel(x, c, positions, norm_mix_g, norm_ffn_g, w_ada, b_ada, w_in, g_cq, g_ckv, w_uq, w_ukv, g_out, w_o,
           w_router, b_router, w_gate, w_up, w_down, ws_gate, ws_up, ws_down, final_g):
    B, S, D = x.shape
    L = w_in.shape[0]
    T = B * S
    w = dict(norm_mix_g=norm_mix_g, norm_ffn_g=norm_ffn_g, g_cq=g_cq, g_ckv=g_ckv, g_out=g_out,
             w_in=jax.vmap(_layout_w_in)(w_in), w_uq=jax.vmap(_layout_w_uq)(w_uq), w_ukv=jax.vmap(_layout_w_ukv)(w_ukv),
             w_o=w_o.astype(BF16), w_router=w_router, b_router=b_router, w_gate=w_gate, w_up=w_up, w_down=w_down,
             ws_gate=ws_gate.astype(BF16), ws_up=ws_up.astype(BF16), ws_down=ws_down.astype(BF16))
    mod_all = _ada(c, w_ada, b_ada).reshape(L, B, 6, D)
    posf = positions.astype(F32).reshape(T, 1)
    tabs = (_rope_tables(posf, 64), _rope_tables(posf, 128))
    xt = x.reshape(T, D)
    for l in range(L):
        mod = [mod_all[l, :, n, :] for n in range(6)]
        xt = _mixers(xt, S, B, tabs, mod, l, w)
        xt = _ffn(xt, S, B, mod, l, w, final_g, l == L - 1)
    return xt.reshape(B, S, D)
```

```python
import functools

import jax
import jax.numpy as jnp
import numpy as np
from jax import lax
from jax.experimental import pallas as pl
from jax.experimental.pallas import tpu as pltpu

F32 = jnp.float32
BF16 = jnp.bfloat16
I32 = jnp.int32

HEAD_DIM = 128
MLA_HEADS = 8
SB_HEADS = 4
DSA_HEADS = 4
N_MIX_HEADS = MLA_HEADS + SB_HEADS + DSA_HEADS
Q_LORA = 512
KV_LORA = 256
MLA_NOPE = 128
MLA_ROPE = 64
IDX_HEADS = 16
IDX_DIM = 64
IDX_SCALE = (IDX_HEADS * IDX_DIM) ** -0.5
TOPK_MAX = 256
ROPE_THETA = 10000.0
N_EXPERTS = 64
TOP_K = 8
N_GROUPS = 8
TOPK_GROUPS = 4
ROUTED_SCALE = 2.5
EPS = 1e-6

LANES = 128
VMEM_LIMIT = 56 * 1024 * 1024
NEG_INF = float("-inf")
INT_MIN = -(2 ** 31)

C_CQ, C_SBQ, C_SBK, C_SBV, C_DSQ = 0, 512, 1024, 1536, 2048
C_CKV, C_DSK, C_DSV, C_IXQ = 2560, 2816, 2944, 3072
C_KR, C_IXK, C_IXW, IN_PAD = 4096, 4224, 4352, 4608

MOE_ROWS = 256


def _cparams(sem):
    return pltpu.CompilerParams(dimension_semantics=sem, vmem_limit_bytes=VMEM_LIMIT)


def _nt_dot(a, b):
    return lax.dot_general(a, b, (((1,), (1,)), ((), ())), preferred_element_type=F32)


def _rms(x, g):
    return x * lax.rsqrt(jnp.mean(x * x, axis=-1, keepdims=True) + EPS) * g


def _ada_kernel(c_ref, w_ref, b_ref, o_ref):
    c = c_ref[...]
    a = (c * jax.nn.sigmoid(c)).astype(BF16)
    o_ref[0] = jnp.dot(a, w_ref[0].astype(BF16), preferred_element_type=F32) + b_ref[0]


def _ada(c, w_ada, b_ada):
    L, D, N = w_ada.shape
    B = c.shape[0]
    tn = 1024
    return pl.pallas_call(
        _ada_kernel,
        grid=(L, N // tn),
        in_specs=[pl.BlockSpec((B, D), lambda l, j: (0, 0)),
                  pl.BlockSpec((1, D, tn), lambda l, j: (l, 0, j)),
                  pl.BlockSpec((1, 1, tn), lambda l, j: (l, 0, j))],
        out_specs=pl.BlockSpec((1, B, tn), lambda l, j: (l, 0, j)),
        out_shape=jax.ShapeDtypeStruct((L, B, N), F32),
        compiler_params=_cparams(("parallel", "parallel")),
        name="ada_mod",
    )(c, w_ada, b_ada.reshape(L, 1, N))


def _rope_tab_kernel(pos_ref, inv_ref, sgn_ref, cos_ref, sin_ref):
    ang = pos_ref[...] * inv_ref[...]
    cos_ref[...] = jnp.cos(ang)
    sin_ref[...] = jnp.sin(ang) * sgn_ref[...]


def _rope_tables(posf, d):
    T = posf.shape[0]
    half = d // 2
    inv = ROPE_THETA ** (-jnp.arange(0, d, 2, dtype=F32) / d)
    reps = LANES // half
    inv_t = jnp.tile(inv, reps).reshape(1, LANES)
    sgn = jnp.tile(jnp.concatenate([-jnp.ones((half,), F32), jnp.ones((half,), F32)]), LANES // d).reshape(1, LANES)
    tm = min(T, 1024)
    return pl.pallas_call(
        _rope_tab_kernel,
        grid=(T // tm,),
        in_specs=[pl.BlockSpec((tm, 1), lambda i: (i, 0)),
                  pl.BlockSpec((1, LANES), lambda i: (0, 0)),
                  pl.BlockSpec((1, LANES), lambda i: (0, 0))],
        out_specs=[pl.BlockSpec((tm, LANES), lambda i: (i, 0))] * 2,
        out_shape=[jax.ShapeDtypeStruct((T, LANES), F32)] * 2,
        compiler_params=_cparams(("parallel",)),
        name=f"rope_tab{d}",
    )(posf, inv_t, sgn)


def _rope64(x, cos, sin_s, first_half):
    rot = jnp.where(first_half, pltpu.roll(x, 96, 1), pltpu.roll(x, 32, 1))
    return x * cos + rot * sin_s


def _rope128(x, cos, sin_s):
    return x * cos + pltpu.roll(x, 64, 1) * sin_s


def _nmm_kernel(*refs, modulate):
    if modulate:
        x_ref, g_ref, sc_ref, sh_ref, w_ref, o_ref, h_s = refs
    else:
        x_ref, g_ref, w_ref, o_ref, h_s = refs

    @pl.when(pl.program_id(1) == 0)
    def _():
        h = _rms(x_ref[...], g_ref[...])
        if modulate:
            h = h * (1.0 + sc_ref[0]) + sh_ref[0]
        h_s[...] = h.astype(BF16)

    o_ref[...] = jnp.dot(h_s[...], w_ref[...], preferred_element_type=F32).astype(o_ref.dtype)


def _nmm(x, col_blk, K, g, w, S, sc=None, sh=None, tm=1024, tn=512, out_dtype=F32):
    T = x.shape[0]
    N = w.shape[1]
    tm = min(tm, S)
    tn = min(tn, N)
    modulate = sc is not None
    in_specs = [pl.BlockSpec((tm, K), lambda i, j: (i, col_blk)),
                pl.BlockSpec((1, K), lambda i, j: (0, 0))]
    args = [x, g.reshape(1, K)]
    if modulate:
        B = sc.shape[0]
        in_specs += [pl.BlockSpec((1, 1, K), lambda i, j: (i * tm // S, 0, 0))] * 2
        args += [sc.reshape(B, 1, K), sh.reshape(B, 1, K)]
    in_specs.append(pl.BlockSpec((K, tn), lambda i, j: (0, j)))
    args.append(w)
    return pl.pallas_call(
        functools.partial(_nmm_kernel, modulate=modulate),
        grid=(T // tm, N // tn),
        in_specs=in_specs,
        out_specs=pl.BlockSpec((tm, tn), lambda i, j: (i, j)),
        out_shape=jax.ShapeDtypeStruct((T, N), out_dtype),
        scratch_shapes=[pltpu.VMEM((tm, K), BF16)],
        compiler_params=_cparams(("parallel", "arbitrary")),
        name="norm_matmul",
    )(*args)


def _mm_res_kernel(a_ref, w_ref, r_ref, gt_ref, o_ref):
    acc = jnp.dot(a_ref[...], w_ref[...], preferred_element_type=F32)
    o_ref[...] = r_ref[...] + gt_ref[0] * acc


def _mm_res(a, w, res, gate, S, tm=1024, tn=512):
    T, K = a.shape
    N = w.shape[1]
    B = gate.shape[0]
    tm = min(tm, S)
    return pl.pallas_call(
        _mm_res_kernel,
        grid=(T // tm, N // tn),
        in_specs=[pl.BlockSpec((tm, K), lambda i, j: (i, 0)),
                  pl.BlockSpec((K, tn), lambda i, j: (0, j)),
                  pl.BlockSpec((tm, tn), lambda i, j: (i, j)),
                  pl.BlockSpec((1, 1, tn), lambda i, j: (i * tm // S, 0, j))],
        out_specs=pl.BlockSpec((tm, tn), lambda i, j: (i, j)),
        out_shape=jax.ShapeDtypeStruct((T, N), F32),
        compiler_params=_cparams(("parallel", "parallel")),
        name="out_proj_residual",
    )(a, w, res, gate.reshape(B, 1, N))


def _prep_kernel(p_ref, q_ref, kv_ref, c64_ref, s64_ref, c128_ref, s128_ref,
                 qm_ref, km_ref, vm_ref, sbq_ref, sbk_ref, sbv_ref,
                 dq_ref, dk_ref, dv_ref, iq_ref, ika_ref, ikb_ref, iw_ref):
    c64, s64 = c64_ref[...], s64_ref[...]
    c128, s128 = c128_ref[...], s128_ref[...]
    lane = lax.broadcasted_iota(I32, c64.shape, 1)
    first = (lane % 64) < 32

    def sl(ref, off, width=LANES):
        return ref[:, off:off + width]

    kr = _rope64(sl(p_ref, C_KR), c64, s64, first).astype(BF16)
    for h in range(MLA_HEADS):
        qm_ref[:, 256 * h:256 * h + 128] = sl(q_ref, 256 * h).astype(BF16)
        qm_ref[:, 256 * h + 128:256 * h + 256] = _rope64(sl(q_ref, 256 * h + 128), c64, s64, first).astype(BF16)
        km_ref[:, 256 * h:256 * h + 128] = sl(kv_ref, 128 * h).astype(BF16)
        km_ref[:, 256 * h + 128:256 * h + 256] = kr
    vm_ref[...] = kv_ref[:, MLA_HEADS * 128:].astype(BF16)
    sbq_ref[...] = sl(p_ref, C_SBQ, 512).astype(BF16)
    sbk_ref[...] = sl(p_ref, C_SBK, 512).astype(BF16)
    sbv_ref[...] = sl(p_ref, C_SBV, 512).astype(BF16)
    for h in range(DSA_HEADS):
        dq_ref[:, 128 * h:128 * h + 128] = _rope128(sl(p_ref, C_DSQ + 128 * h), c128, s128).astype(BF16)
    dk_ref[...] = _rope128(sl(p_ref, C_DSK), c128, s128).astype(BF16)
    dv_ref[...] = sl(p_ref, C_DSV).astype(BF16)
    for j in range(IDX_HEADS // 2):
        iq_ref[:, 128 * j:128 * j + 128] = _rope64(sl(p_ref, C_IXQ + 128 * j), c64, s64, first).astype(BF16)
    ik = _rope64(sl(p_ref, C_IXK), c64, s64, first)
    ika_ref[...] = ik.astype(BF16)
    ikb_ref[...] = pltpu.roll(ik, 64, 1).astype(BF16)
    iw_ref[...] = sl(p_ref, C_IXW)


def _prep(proj, q_raw, kv_raw, tabs64, tabs128, tm=256):
    T = proj.shape[0]
    row = lambda w: pl.BlockSpec((tm, w), lambda i: (i, 0))
    widths = [2048, 2048, 1024, 512, 512, 512, 512, 128, 128, 1024, 128, 128]
    out_shape = [jax.ShapeDtypeStruct((T, w), BF16) for w in widths] + [jax.ShapeDtypeStruct((T, LANES), F32)]
    return pl.pallas_call(
        _prep_kernel,
        grid=(T // tm,),
        in_specs=[row(IN_PAD), row(2048), row(2048), row(LANES), row(LANES), row(LANES), row(LANES)],
        out_specs=[row(w) for w in widths] + [row(LANES)],
        out_shape=out_shape,
        compiler_params=_cparams(("parallel",)),
        name="mixer_prep",
    )(proj, q_raw, kv_raw, *tabs64, *tabs128)


def _head_out(o, g):
    return _rms(o, g).astype(BF16)


LOG2E = 1.4426950408889634


def _mla_kernel(q_ref, k_ref, vt_ref, g_ref, o_ref, m_s, l_s, acc_s, *, t, scale):
    i = pl.program_id(2)
    hps = m_s.shape[0]
    c = scale * LOG2E
    m_s[...] = jnp.full(m_s.shape, NEG_INF, F32)
    l_s[...] = jnp.zeros(l_s.shape, F32)
    acc_s[...] = jnp.zeros(acc_s.shape, F32)

    def step(cidx, masked):
        k0 = pl.multiple_of(cidx * t, t)
        for hh in range(hps):
            s = _nt_dot(k_ref[pl.ds(k0, t), 256 * hh:256 * hh + 256], q_ref[:, 256 * hh:256 * hh + 256])
            if masked:
                key = lax.broadcasted_iota(I32, (t, t), 0)
                qry = lax.broadcasted_iota(I32, (t, t), 1)
                s = jnp.where(key <= qry, s, NEG_INF)
            m_prev = m_s[hh]
            m_new = jnp.maximum(m_prev, jnp.max(s, axis=0, keepdims=True))
            alpha = jnp.exp2((m_prev - m_new) * c)
            p = jnp.exp2((s - m_new) * c)
            l_s[hh] = alpha * l_s[hh] + jnp.sum(p, axis=0, keepdims=True)
            acc_s[hh] = alpha * acc_s[hh] + jnp.dot(vt_ref[128 * hh:128 * hh + 128, pl.ds(k0, t)], p.astype(BF16),
                                                    preferred_element_type=F32)
            m_s[hh] = m_new

    def body(cidx, carry):
        step(cidx, False)
        return carry

    lax.fori_loop(0, i, body, 0)
    step(i, True)
    for hh in range(hps):
        o_ref[:, 128 * hh:128 * hh + 128] = _head_out((acc_s[hh] / l_s[hh]).T, g_ref[:, 128 * hh:128 * hh + 128])


def _mla(qm, km, vmt, g_out, B, S, t=512, hps=2):
    T = qm.shape[0]
    t = min(t, S)
    nq = S // t
    H = MLA_HEADS
    return pl.pallas_call(
        functools.partial(_mla_kernel, t=t, scale=(MLA_NOPE + MLA_ROPE) ** -0.5),
        grid=(B, H // hps, nq),
        in_specs=[pl.BlockSpec((t, 256 * hps), lambda b, h, i: (b * nq + i, h)),
                  pl.BlockSpec((S, 256 * hps), lambda b, h, i: (b, h)),
                  pl.BlockSpec((128 * hps, S), lambda b, h, i: (h, b)),
                  pl.BlockSpec((1, 128 * hps), lambda b, h, i: (0, h))],
        out_specs=pl.BlockSpec((t, 128 * hps), lambda b, h, i: (b * nq + i, h)),
        out_shape=jax.ShapeDtypeStruct((T, H * HEAD_DIM), BF16),
        scratch_shapes=[pltpu.VMEM((hps, 1, t), F32), pltpu.VMEM((hps, 1, t), F32), pltpu.VMEM((hps, 128, t), F32)],
        compiler_params=_cparams(("parallel", "parallel", "arbitrary")),
        name="mla_attention",
    )(qm, km, vmt, g_out)


def _sb_kernel(q_ref, k_ref, vt_ref, g_ref, tri_ref, o_ref, carry_s, acc_s, *, t, scale):
    i = pl.program_id(2)
    hps = carry_s.shape[0]
    tri = tri_ref[...]
    carry_s[...] = jnp.zeros(carry_s.shape, F32)
    acc_s[...] = jnp.zeros(acc_s.shape, F32)

    def step(cidx, masked):
        k0 = pl.multiple_of(cidx * t, t)
        for hh in range(hps):
            z = _nt_dot(k_ref[pl.ds(k0, t), 128 * hh:128 * hh + 128], q_ref[:, 128 * hh:128 * hh + 128]) * scale
            sp = jnp.log(1.0 + jnp.exp(-jnp.abs(z)))
            log_beta = jnp.minimum(z, 0.0) - sp
            log_keep = jnp.minimum(-z, 0.0) - sp
            if masked:
                key = lax.broadcasted_iota(I32, (t, t), 0)
                qry = lax.broadcasted_iota(I32, (t, t), 1)
                strict = key < qry
                log_keep = jnp.where(strict, log_keep, 0.0)
            hi = log_keep.astype(BF16)
            lo = (log_keep - hi.astype(F32)).astype(BF16)
            suffix = jnp.dot(tri, hi, preferred_element_type=F32) + jnp.dot(tri, lo, preferred_element_type=F32)
            a = jnp.exp(log_beta + suffix + carry_s[hh])
            if masked:
                a = jnp.where(strict, a, 0.0)
            acc_s[hh] += jnp.dot(vt_ref[128 * hh:128 * hh + 128, pl.ds(k0, t)], a.astype(BF16),
                                 preferred_element_type=F32)
            carry_s[hh] += jnp.sum(log_keep, axis=0, keepdims=True)

    step(i, True)

    def body(j, carry):
        step(i - 1 - j, False)
        return carry

    lax.fori_loop(0, i, body, 0)
    for hh in range(hps):
        o_ref[:, 128 * hh:128 * hh + 128] = _head_out(acc_s[hh].T, g_ref[:, 128 * hh:128 * hh + 128])


def _sb(sbq, sbk, sbvt, g_out, B, S, t=512, hps=2):
    T = sbq.shape[0]
    t = min(t, S)
    nq = S // t
    H = SB_HEADS
    r = np.arange(t)
    tri = jnp.asarray((r[None, :] > r[:, None]).astype(np.float32), BF16)
    g0 = MLA_HEADS // hps
    return pl.pallas_call(
        functools.partial(_sb_kernel, t=t, scale=HEAD_DIM ** -0.5),
        grid=(B, H // hps, nq),
        in_specs=[pl.BlockSpec((t, 128 * hps), lambda b, h, i: (b * nq + i, h)),
                  pl.BlockSpec((S, 128 * hps), lambda b, h, i: (b, h)),
                  pl.BlockSpec((128 * hps, S), lambda b, h, i: (h, b)),
                  pl.BlockSpec((1, 128 * hps), lambda b, h, i: (0, g0 + h)),
                  pl.BlockSpec((t, t), lambda b, h, i: (0, 0))],
        out_specs=pl.BlockSpec((t, 128 * hps), lambda b, h, i: (b * nq + i, h)),
        out_shape=jax.ShapeDtypeStruct((T, H * HEAD_DIM), BF16),
        scratch_shapes=[pltpu.VMEM((hps, 1, t), F32), pltpu.VMEM((hps, 128, t), F32)],
        compiler_params=_cparams(("parallel", "parallel", "arbitrary")),
        name="stickbreak_attention",
    )(sbq, sbk, sbvt, g_out, tri)


def _dsa_kernel(iq_ref, ika_ref, ikb_ref, wt_ref, dq_ref, dk_ref, vt_ref, g_ref, o_ref,
                key_s, m_s, l_s, acc_s, *, tq, tk, n_sel, scale):
    i = pl.program_id(1)
    nch = (i * tq + tq + tk - 1) // tk
    qpos = i * tq + lax.broadcasted_iota(I32, (1, tq), 1)
    kiota = lax.broadcasted_iota(I32, (tk, 1), 0)
    wt = wt_ref[0]
    half = IDX_HEADS // 2

    def score_chunk(c, carry):
        k0 = pl.multiple_of(c * tk, tk)
        kk = jnp.concatenate([ika_ref[pl.ds(k0, tk), :], ikb_ref[pl.ds(k0, tk), :]], axis=0)
        score = jnp.zeros((tk, tq), F32)
        for j in range(half):
            r = jnp.maximum(_nt_dot(kk, iq_ref[:, 128 * j:128 * j + 128]), 0.0)
            score = score + r[:tk] * wt[2 * j:2 * j + 1, :] + r[tk:] * wt[2 * j + 1:2 * j + 2, :]
        score = score * IDX_SCALE
        score = jnp.where(k0 + kiota <= qpos, score, NEG_INF)
        bits = pltpu.bitcast(score, I32)
        key_s[pl.ds(k0, tk), :] = jnp.where(bits < 0, bits ^ jnp.int32(0x7FFFFFFF), bits)
        return carry

    lax.fori_loop(0, nch, score_chunk, 0)

    def count(pred):
        def body(c, acc):
            k0 = pl.multiple_of(c * tk, tk)
            hit = pred(key_s[pl.ds(k0, tk), :], k0 + kiota).astype(I32)
            return acc + jnp.sum(hit.reshape(tk // 8, 8, tq), axis=0)
        part = lax.fori_loop(0, nch, body, jnp.zeros((8, tq), I32))
        return jnp.sum(part, axis=0, keepdims=True)

    c0 = count(lambda k, idx: k >= 0)
    thr = jnp.where(c0 >= n_sel, jnp.int32(0), jnp.int32(INT_MIN))

    def vbit(b, thr):
        cand = thr + jnp.left_shift(jnp.int32(1), 30 - b)
        cnt = count(lambda k, idx: k >= cand)
        return jnp.where(cnt >= n_sel, cand, thr)

    thr = lax.fori_loop(0, 31, vbit, thr)
    n_ge = count(lambda k, idx: k >= thr)
    idx_bits = max(1, int(np.ceil(np.log2(key_s.shape[0]))))

    def tie_bound():
        need = n_sel - count(lambda k, idx: k > thr)

        def ibit(b, bound):
            cand = bound + jnp.left_shift(jnp.int32(1), idx_bits - 1 - b)
            cnt = count(lambda k, idx: (k == thr) & (idx < cand))
            return jnp.where(cnt < need, cand, bound)

        return lax.fori_loop(0, idx_bits, ibit, jnp.zeros((1, tq), I32))

    bound = lax.cond(jnp.max(n_ge) > n_sel, tie_bound, lambda: jnp.full((1, tq), 2 ** idx_bits, I32))

    qc = jnp.concatenate([dq_ref[:, 128 * h:128 * h + 128] for h in range(DSA_HEADS)], axis=0)
    sc2 = scale * LOG2E
    m_s[...] = jnp.full(m_s.shape, NEG_INF, F32)
    l_s[...] = jnp.zeros(l_s.shape, F32)
    acc_s[...] = jnp.zeros(acc_s.shape, F32)

    def attn_chunk(c, carry):
        k0 = pl.multiple_of(c * tk, tk)
        key = key_s[pl.ds(k0, tk), :]
        idx = k0 + kiota
        sel = ((key > thr) | ((key == thr) & (idx <= bound))) & (idx <= qpos)
        bias = jnp.where(sel, 0.0, NEG_INF)
        s = _nt_dot(dk_ref[pl.ds(k0, tk), :], qc) + jnp.concatenate([bias] * DSA_HEADS, axis=1)
        m_prev = m_s[...]
        m_new = jnp.maximum(m_prev, jnp.max(s, axis=0, keepdims=True))
        m_safe = jnp.where(m_new == NEG_INF, 0.0, m_new)
        alpha = jnp.exp2((m_prev - m_safe) * sc2)
        p = jnp.exp2((s - m_safe) * sc2)
        l_s[...] = alpha * l_s[...] + jnp.sum(p, axis=0, keepdims=True)
        acc_s[...] = alpha * acc_s[...] + jnp.dot(vt_ref[:, pl.ds(k0, tk)], p.astype(BF16),
                                                  preferred_element_type=F32)
        m_s[...] = m_new
        return carry

    lax.fori_loop(0, nch, attn_chunk, 0)
    ot = acc_s[...] / l_s[...]
    for h in range(DSA_HEADS):
        o_ref[:, 128 * h:128 * h + 128] = _head_out(ot[:, h * tq:(h + 1) * tq].T, g_ref[:, 128 * h:128 * h + 128])


def _dsa(iq, ika, ikb, wt, dq, dk, vt, g_out, B, S, tq=128, tk=256):
    T = iq.shape[0]
    tk = min(tk, S)
    nq = S // tq
    H = DSA_HEADS
    n_sel = min(TOPK_MAX, S // 4)
    g_c = g_out[:, (MLA_HEADS + SB_HEADS) * HEAD_DIM:]
    return pl.pallas_call(
        functools.partial(_dsa_kernel, tq=tq, tk=tk, n_sel=n_sel, scale=HEAD_DIM ** -0.5),
        grid=(B, nq),
        in_specs=[pl.BlockSpec((tq, IDX_HEADS * IDX_DIM), lambda b, i: (b * nq + i, 0)),
                  pl.BlockSpec((S, 128), lambda b, i: (b, 0)),
                  pl.BlockSpec((S, 128), lambda b, i: (b, 0)),
                  pl.BlockSpec((1, IDX_HEADS, tq), lambda b, i: (b, 0, i)),
                  pl.BlockSpec((tq, H * 128), lambda b, i: (b * nq + i, 0)),
                  pl.BlockSpec((S, 128), lambda b, i: (b, 0)),
                  pl.BlockSpec((128, S), lambda b, i: (0, b)),
                  pl.BlockSpec((1, H * 128), lambda b, i: (0, 0))],
        out_specs=pl.BlockSpec((tq, H * 128), lambda b, i: (b * nq + i, 0)),
        out_shape=jax.ShapeDtypeStruct((T, H * HEAD_DIM), BF16),
        scratch_shapes=[pltpu.VMEM((S, tq), I32), pltpu.VMEM((1, H * tq), F32), pltpu.VMEM((1, H * tq), F32),
                        pltpu.VMEM((128, H * tq), F32)],
        compiler_params=_cparams(("parallel", "arbitrary")),
        name="dsa_attention",
    )(iq, ika, ikb, wt, dq, dk, vt, g_c)


def _route_kernel(x_ref, g_ref, sc_ref, sh_ref, wr_ref, br_ref, tri_ref,
                  h_ref, eidx_ref, gate_ref, pos_ref, cnt_ref, run_s):
    @pl.when(pl.program_id(0) == 0)
    def _():
        run_s[...] = jnp.zeros(run_s.shape, F32)

    h = _rms(x_ref[...], g_ref[...]) * (1.0 + sc_ref[0]) + sh_ref[0]
    hb = h.astype(BF16)
    h_ref[...] = hb
    tm = h.shape[0]
    logits = _nt_dot(wr_ref[...], hb)
    scores = jax.nn.sigmoid(logits)
    biased = scores + br_ref[...]
    gsz = N_EXPERTS // N_GROUPS
    b3 = biased.reshape(N_GROUPS, gsz, tm)
    m1 = jnp.max(b3, axis=1, keepdims=True)
    n1 = jnp.sum((b3 == m1).astype(F32), axis=1, keepdims=True)
    m2 = jnp.max(jnp.where(b3 < m1, b3, NEG_INF), axis=1, keepdims=True)
    grp = (m1 + jnp.where(n1 >= 2.0, m1, m2)).reshape(N_GROUPS, tm)
    gi = lax.broadcasted_iota(I32, (N_GROUPS, 1), 0)
    grank = jnp.zeros((N_GROUPS, tm), F32)
    for g in range(N_GROUPS):
        rowv = grp[g:g + 1, :]
        grank = grank + jnp.where((rowv > grp) | ((rowv == grp) & (g < gi)), 1.0, 0.0)
    gmask = grank.reshape(N_GROUPS, 1, tm) < float(TOPK_GROUPS)
    masked = jnp.where(gmask, b3, NEG_INF).reshape(N_EXPERTS, tm)
    ei = lax.broadcasted_iota(I32, (N_EXPERTS, 1), 0)
    rank = jnp.zeros((N_EXPERTS, tm), F32)
    for e in range(N_EXPERTS):
        rowv = masked[e:e + 1, :]
        rank = rank + jnp.where((rowv > masked) | ((rowv == masked) & (e < ei)), 1.0, 0.0)
    sel = rank < float(TOP_K)
    selm = sel.astype(F32)
    gsum = jnp.sum(scores * selm, axis=0, keepdims=True)
    gate = scores * selm / gsum * ROUTED_SCALE
    within = jnp.dot(selm.astype(BF16), tri_ref[...], preferred_element_type=F32)
    posf = within + run_s[...]
    run_s[...] += jnp.sum(selm, axis=1, keepdims=True)
    cnt_ref[...] = jnp.broadcast_to(run_s[...], cnt_ref.shape)
    eif = ei.astype(F32)
    for k in range(TOP_K):
        onek = rank == float(k)
        eidx_ref[k:k + 1, :] = jnp.sum(jnp.where(onek, eif, 0.0), axis=0, keepdims=True).astype(I32)
        gate_ref[k:k + 1, :] = jnp.sum(jnp.where(onek, gate, 0.0), axis=0, keepdims=True)
        pos_ref[k:k + 1, :] = jnp.sum(jnp.where(onek, posf, 0.0), axis=0, keepdims=True).astype(I32)


def _route(x, g, sc, sh, w_router, b_router, S, tm=512):
    T, D = x.shape
    B = sc.shape[0]
    tm = min(tm, S)
    r = np.arange(tm)
    tri = jnp.asarray((r[:, None] < r[None, :]).astype(np.float32), BF16)
    full = lambda shape: pl.BlockSpec(shape, lambda i: (0,) * len(shape))
    return pl.pallas_call(
        _route_kernel,
        grid=(T // tm,),
        in_specs=[pl.BlockSpec((tm, D), lambda i: (i, 0)),
                  full((1, D)),
                  pl.BlockSpec((1, 1, D), lambda i: (i * tm // S, 0, 0)),
                  pl.BlockSpec((1, 1, D), lambda i: (i * tm // S, 0, 0)),
                  full((N_EXPERTS, D)),
                  full((N_EXPERTS, 1)),
                  full((tm, tm))],
        out_specs=[pl.BlockSpec((tm, D), lambda i: (i, 0)),
                   pl.BlockSpec((TOP_K, tm), lambda i: (0, i)),
                   pl.BlockSpec((TOP_K, tm), lambda i: (0, i)),
                   pl.BlockSpec((TOP_K, tm), lambda i: (0, i)),
                   full((N_EXPERTS, LANES))],
        out_shape=[jax.ShapeDtypeStruct((T, D), BF16),
                   jax.ShapeDtypeStruct((TOP_K, T), I32),
                   jax.ShapeDtypeStruct((TOP_K, T), F32),
                   jax.ShapeDtypeStruct((TOP_K, T), I32),
                   jax.ShapeDtypeStruct((N_EXPERTS, LANES), F32)],
        scratch_shapes=[pltpu.VMEM((N_EXPERTS, 1), F32)],
        compiler_params=_cparams(("arbitrary",)),
        name="ffn_norm_route",
    )(x, g.reshape(1, D), sc.reshape(B, 1, D), sh.reshape(B, 1, D),
      w_router.T.astype(BF16), b_router.reshape(N_EXPERTS, 1), tri)


def _moe_kernel(bexp_ref, bval_ref, slot_hbm, h_hbm, wg_ref, wu_ref, wd_ref, y_hbm,
                xbuf0, xbuf1, x_s, mid_s, ybuf0, ybuf1, wg_s, wu_s, wd_s, idx_s, isem, gsem, ssem, *, n_tok):
    b = pl.program_id(0)
    nb = pl.num_programs(0)
    rows, D = x_s.shape
    half = D // 2
    DE = mid_s.shape[1]
    YS = D // LANES
    NW = 256
    xbufs, ybufs = (xbuf0, xbuf1), (ybuf0, ybuf1)
    n_real = TOP_K * n_tok

    def ring(blk):
        return (blk & 3) * rows

    def idx_copy(blk):
        return pltpu.make_async_copy(slot_hbm.at[blk], idx_s.at[pl.ds(pl.multiple_of(ring(blk), rows), rows)],
                                     isem.at[blk & 3])

    def gather_start(base, r, p):
        tok = idx_s[base + r] & (n_tok - 1)
        pltpu.make_async_copy(h_hbm.at[pl.ds(pl.multiple_of(tok * 8, 8), 8), :],
                              xbufs[p].at[pl.ds(r * 8, 8), :], gsem.at[p]).start()

    def scatter_start(base, r, p):
        dst = idx_s[base + r]
        pltpu.make_async_copy(ybufs[p].at[pl.ds(r * YS, YS), :],
                              y_hbm.at[pl.ds(pl.multiple_of(dst * YS, YS), YS), :], ssem.at[p]).start()

    def wait_gather(p):
        pltpu.make_async_copy(xbufs[p], xbufs[p], gsem.at[p]).wait()

    def wait_scatter(p):
        pltpu.make_async_copy(ybufs[p], ybufs[p], ssem.at[p]).wait()

    def compute_pieces(p):
        def unpack():
            for s in range(8):
                w = xbufs[p][pl.ds(s, rows, stride=8), :]
                x_s[:, 128 * s:128 * s + 128] = pltpu.bitcast(w << 16, F32).astype(BF16)
                x_s[:, half + 128 * s:half + 128 * s + 128] = pltpu.bitcast(w & jnp.uint32(0xFFFF0000), F32).astype(BF16)

        def gate_up(j):
            def piece():
                x = x_s[...]
                gt = jnp.dot(x, wg_s[:, NW * j:NW * j + NW], preferred_element_type=F32)
                up = jnp.dot(x, wu_s[:, NW * j:NW * j + NW], preferred_element_type=F32)
                mid_s[:, NW * j:NW * j + NW] = (gt * jax.nn.sigmoid(gt) * up).astype(BF16)
            return piece

        def down(n):
            def piece():
                y = jnp.dot(mid_s[...], wd_s[:, NW * n:NW * n + NW], preferred_element_type=F32)
                for s in range(NW // LANES):
                    ybufs[p][pl.ds(n * (NW // LANES) + s, rows, stride=YS), :] = y[:, LANES * s:LANES * s + LANES]
            return piece
        return [unpack] + [gate_up(j) for j in range(DE // NW)] + [down(n) for n in range(D // NW)]

    def run_block(p, compute):
        gbase, sbase = ring(b + 1), ring(b - 1)
        pieces = compute_pieces(p) if compute else []
        starts = [functools.partial(gather_start, gbase, r, 1 - p) for r in range(rows)]
        starts += [functools.partial(scatter_start, sbase, r, 1 - p) for r in range(rows)]
        n_slots = max(len(pieces), 1)
        per = -(-len(starts) // n_slots)
        for n in range(n_slots):
            if pieces:
                pieces[n]()
            for st in starts[n * per:(n + 1) * per]:
                st()

    @pl.when(b == 0)
    def _():
        for r in range(rows):
            idx_s[3 * rows + r] = n_real + rows + r
        ybuf0[...] = jnp.zeros(ybuf0.shape, F32)
        ybuf1[...] = jnp.zeros(ybuf1.shape, F32)
        init = pltpu.make_async_copy(ybuf0, y_hbm.at[pl.ds(n_real * YS, rows * YS), :], ssem.at[0])
        init.start()
        init.wait()
        idx_copy(0).start()
        idx_copy(0).wait()
        for r in range(rows):
            gather_start(0, r, 0)
        idx_copy(1).start()

    @pl.when(b + 2 < nb)
    def _():
        idx_copy(b + 2).start()

    @pl.when(b + 1 < nb)
    def _():
        idx_copy(b + 1).wait()

    @pl.when((b == 0) | (bexp_ref[b] != bexp_ref[jnp.maximum(b - 1, 0)]))
    def _():
        wg_s[...] = wg_ref[0, 0].astype(BF16)
        wu_s[...] = wu_ref[0, 0].astype(BF16)
        wd_s[...] = wd_ref[0, 0].astype(BF16)

    active = bval_ref[b] > 0
    for p in range(2):
        @pl.when((b & 1) == p)
        def _():
            wait_gather(p)

            @pl.when(b >= 1)
            def _():
                wait_scatter(p)

            @pl.when(active)
            def _():
                run_block(p, True)

            @pl.when(jnp.logical_not(active))
            def _():
                run_block(p, False)

            @pl.when(b == nb - 1)
            def _():
                sbase = ring(b)
                for r in range(rows):
                    scatter_start(sbase, r, p)
                wait_scatter(p)
                wait_scatter(1 - p)
                wait_gather(1 - p)


def _pack_rows(hb):
    T, D = hb.shape
    assert D == 2 * 8 * LANES
    bits = lax.bitcast_convert_type(hb, jnp.uint16).astype(jnp.uint32)
    return (bits[:, :D // 2] | (bits[:, D // 2:] << 16)).reshape(T * 8, LANES)


def _moe(hp, slot_ids, blk_exp, blk_val, w_gate, w_up, w_down, l):
    T = hp.shape[0] // 8
    D = w_gate.shape[2]
    assert T & (T - 1) == 0
    DE = w_gate.shape[3]
    nb = blk_exp.shape[0]
    wspec = lambda shape: pl.BlockSpec((1, 1) + shape, lambda b, bexp, bval: (l, bexp[b], 0, 0))
    grid_spec = pltpu.PrefetchScalarGridSpec(
        num_scalar_prefetch=2,
        grid=(nb,),
        in_specs=[pl.BlockSpec(memory_space=pl.ANY), pl.BlockSpec(memory_space=pl.ANY),
                  wspec((D, DE)), wspec((D, DE)), wspec((DE, D))],
        out_specs=pl.BlockSpec(memory_space=pl.ANY),
        scratch_shapes=[pltpu.VMEM((MOE_ROWS * 8, LANES), jnp.uint32), pltpu.VMEM((MOE_ROWS * 8, LANES), jnp.uint32),
                        pltpu.VMEM((MOE_ROWS, D), BF16), pltpu.VMEM((MOE_ROWS, DE), BF16),
                        pltpu.VMEM((MOE_ROWS * D // LANES, LANES), F32), pltpu.VMEM((MOE_ROWS * D // LANES, LANES), F32),
                        pltpu.VMEM((D, DE), BF16), pltpu.VMEM((D, DE), BF16), pltpu.VMEM((DE, D), BF16),
                        pltpu.SMEM((4 * MOE_ROWS,), I32),
                        pltpu.SemaphoreType.DMA((4,)), pltpu.SemaphoreType.DMA((2,)), pltpu.SemaphoreType.DMA((2,))],
    )
    return pl.pallas_call(
        functools.partial(_moe_kernel, n_tok=T),
        grid_spec=grid_spec,
        out_shape=jax.ShapeDtypeStruct(((TOP_K * T + 2 * MOE_ROWS) * D // LANES, LANES), F32),
        compiler_params=_cparams(("arbitrary",)),
        name="moe_experts",
    )(blk_exp, blk_val, slot_ids.reshape(nb, MOE_ROWS), hp, w_gate, w_up, w_down)


def _ffn_out_kernel(*refs, final):
    y_refs = refs[:TOP_K]
    gk_ref, h_ref, x_ref, gf_ref, wsg_ref, wsu_ref, wsd_ref, fg_ref, o_ref = refs[TOP_K:]
    tm, D = x_ref.shape
    ys = D // LANES
    gates = [gk_ref[k] for k in range(TOP_K)]
    chunks = []
    for s in range(ys):
        acc = y_refs[0][pl.ds(s, tm, stride=ys), :] * gates[0]
        for k in range(1, TOP_K):
            acc = acc + y_refs[k][pl.ds(s, tm, stride=ys), :] * gates[k]
        chunks.append(acc)
    routed = jnp.concatenate(chunks, axis=1)
    hb = h_ref[...]
    gt = jnp.dot(hb, wsg_ref[...], preferred_element_type=F32)
    up = jnp.dot(hb, wsu_ref[...], preferred_element_type=F32)
    mid = (gt * jax.nn.sigmoid(gt) * up).astype(BF16)
    shared = jnp.dot(mid, wsd_ref[...], preferred_element_type=F32)
    out = x_ref[...] + gf_ref[0] * (routed + shared)
    if final:
        out = _rms(out, fg_ref[...])
    o_ref[...] = out


def _ffn_out(y, gate3, h, x, g_f, wsg, wsu, wsd, final_g, final, S, tm=128):
    T, D = x.shape
    B = g_f.shape[0]
    DS = wsg.shape[1]
    tm = min(tm, S)
    nt = T // tm
    full = lambda shape: pl.BlockSpec(shape, lambda i: (0,) * len(shape))
    y_specs = [pl.BlockSpec((tm * D // LANES, LANES), functools.partial(lambda i, k: (k * nt + i, 0), k=k))
               for k in range(TOP_K)]
    return pl.pallas_call(
        functools.partial(_ffn_out_kernel, final=final),
        grid=(nt,),
        in_specs=y_specs + [
                  pl.BlockSpec((TOP_K, tm, 1), lambda i: (0, i, 0)),
                  pl.BlockSpec((tm, D), lambda i: (i, 0)),
                  pl.BlockSpec((tm, D), lambda i: (i, 0)),
                  pl.BlockSpec((1, 1, D), lambda i: (i * tm // S, 0, 0)),
                  full((D, DS)), full((D, DS)), full((DS, D)), full((1, D))],
        out_specs=pl.BlockSpec((tm, D), lambda i: (i, 0)),
        out_shape=jax.ShapeDtypeStruct((T, D), F32),
        compiler_params=_cparams(("parallel",)),
        name="ffn_combine",
    )(*([y] * TOP_K), gate3, h, x, g_f.reshape(B, 1, D), wsg, wsu, wsd, final_g.reshape(1, D))


def _layout_w_in(w):
    D = w.shape[0]
    sizes = (Q_LORA, KV_LORA, MLA_ROPE, 512, 512, 512, 512, HEAD_DIM, HEAD_DIM, IDX_HEADS * IDX_DIM, IDX_DIM, IDX_HEADS)
    offs = np.concatenate([[0], np.cumsum(sizes)])
    cq, ckv, kr, sbq, sbk, sbv, dsq, dsk, dsv, ixq, ixk, ixw = [w[:, offs[n]:offs[n + 1]] for n in range(12)]
    z = lambda n: jnp.zeros((D, n), w.dtype)
    out = jnp.concatenate([cq, sbq, sbk, sbv, dsq, ckv, dsk, dsv, ixq, kr, z(64), ixk, z(64), ixw, z(112), z(128)], axis=1)
    assert out.shape[1] == IN_PAD
    return out.astype(BF16)


def _layout_w_uq(w):
    K = w.shape[0]
    w3 = w.reshape(K, MLA_HEADS, MLA_NOPE + MLA_ROPE)
    w3 = jnp.concatenate([w3, jnp.zeros((K, MLA_HEADS, 64), w.dtype)], axis=2)
    return w3.reshape(K, MLA_HEADS * 256).astype(BF16)


def _layout_w_ukv(w):
    K = w.shape[0]
    w3 = w.reshape(K, MLA_HEADS, MLA_NOPE + HEAD_DIM)
    return jnp.concatenate([w3[:, :, :MLA_NOPE].reshape(K, -1), w3[:, :, MLA_NOPE:].reshape(K, -1)], axis=1).astype(BF16)


def _expert_blocks(eidx, pos, counts, T):
    M = T * TOP_K
    nb = M // MOE_ROWS + N_EXPERTS
    P = nb * MOE_ROWS
    padded = (counts + MOE_ROWS - 1) // MOE_ROWS * MOE_ROWS
    pend = jnp.cumsum(padded)
    pstart = pend - padded
    experts = jnp.arange(N_EXPERTS, dtype=I32)
    dest = jnp.sum(jnp.where(eidx[:, :, None] == experts, pstart, 0), axis=-1) + pos
    out_row = jnp.arange(TOP_K, dtype=I32)[:, None] * T + jnp.arange(T, dtype=I32)[None, :]
    pad_row = M + ((jnp.arange(P, dtype=I32) // MOE_ROWS) % 2) * MOE_ROWS + jnp.arange(P, dtype=I32) % MOE_ROWS
    slot_ids = pad_row.at[dest.reshape(-1)].set(out_row.reshape(-1), unique_indices=True)
    blk_start = jnp.arange(nb, dtype=I32) * MOE_ROWS
    blk_exp = jnp.minimum(jnp.sum((pend[None, :] <= blk_start[:, None]).astype(I32), axis=1), N_EXPERTS - 1)
    real_end = jnp.sum(jnp.where(blk_exp[:, None] == experts, pstart + counts, 0), axis=-1)
    blk_val = jnp.clip(real_end - blk_start, 0, MOE_ROWS).astype(I32)
    return slot_ids, blk_exp, blk_val


def _mixers(x, S, B, positions_tabs, mod, l, w):
    sh_m, sc_m, g_m = mod[0], mod[1], mod[2]
    proj = _nmm(x, 0, x.shape[1], w["norm_mix_g"][l], w["w_in"][l], S, sc=sc_m, sh=sh_m)
    q_raw = _nmm(proj, C_CQ // Q_LORA, Q_LORA, w["g_cq"][l], w["w_uq"][l], S)
    kv_raw = _nmm(proj, C_CKV // KV_LORA, KV_LORA, w["g_ckv"][l], w["w_ukv"][l], S)
    tabs64, tabs128 = positions_tabs
    (qm, km, vm, sbq, sbk, sbv, dq, dk, dv, iq, ika, ikb, iw) = _prep(proj, q_raw, kv_raw, tabs64, tabs128)
    g_out = w["g_out"][l].reshape(1, -1)
    o_a = _mla(qm, km, vm.T, g_out, B, S)
    o_b = _sb(sbq, sbk, sbv.T, g_out, B, S)
    wt = iw[:, :IDX_HEADS].reshape(B, S, IDX_HEADS).transpose(0, 2, 1)
    o_c = _dsa(iq, ika, ikb, wt, dq, dk, dv.T, g_out, B, S)
    o = jnp.concatenate([o_a, o_b, o_c], axis=1)
    return _mm_res(o, w["w_o"][l], x, g_m, S)


def _ffn(x, S, B, mod, l, w, final_g, final):
    T, D = x.shape
    sh_f, sc_f, g_f = mod[3], mod[4], mod[5]
    h, eidx, gate, pos, cnt = _route(x, w["norm_ffn_g"][l], sc_f, sh_f, w["w_router"][l], w["b_router"][l], S)
    counts = cnt[:, 0].astype(I32)
    slot_ids, blk_exp, blk_val = _expert_blocks(eidx, pos, counts, T)
    y = _moe(_pack_rows(h), slot_ids, blk_exp, blk_val, w["w_gate"], w["w_up"], w["w_down"], l)
    return _ffn_out(y, gate.reshape(TOP_K, T, 1), h, x, g_f, w["ws_gate"][l], w["ws_up"][l], w["ws_down"][l],
                    final_g, final, S)


def kernel(x, c, positions, norm_mix_g, norm_ffn_g, w_ada, b_ada, w_in, g_cq, g_ckv, w_uq, w_ukv, g_out, w_o,
           w_router, b_router, w_gate, w_up, w_down, ws_gate, ws_up, ws_down, final_g):
    B, S, D = x.shape
    L = w_in.shape[0]
    T = B * S
    w = dict(norm_mix_g=norm_mix_g, norm_ffn_g=norm_ffn_g, g_cq=g_cq, g_ckv=g_ckv, g_out=g_out,
             w_in=jax.vmap(_layout_w_in)(w_in), w_uq=jax.vmap(_layout_w_uq)(w_uq), w_ukv=jax.vmap(_layout_w_ukv)(w_ukv),
             w_o=w_o.astype(BF16), w_router=w_router, b_router=b_router, w_gate=w_gate, w_up=w_up, w_down=w_down,
             ws_gate=ws_gate.astype(BF16), ws_up=ws_up.astype(BF16), ws_down=ws_down.astype(BF16))
    mod_all = _ada(c, w_ada, b_ada).reshape(L, B, 6, D)
    posf = positions.astype(F32).reshape(T, 1)
    tabs = (_rope_tables(posf, 64), _rope_tables(posf, 128))
    xt = x.reshape(T, D)
    for l in range(L):
        mod = [mod_all[l, :, n, :] for n in range(6)]
        xt = _mixers(xt, S, B, tabs, mod, l, w)
        xt = _ffn(xt, S, B, mod, l, w, final_g, l == L - 1)
    return xt.reshape(B, S, D)
```

```python
import functools

import jax
import jax.numpy as jnp
import numpy as np
from jax import lax
from jax.experimental import pallas as pl
from jax.experimental.pallas import tpu as pltpu

F32 = jnp.float32
BF16 = jnp.bfloat16
I32 = jnp.int32

HEAD_DIM = 128
MLA_HEADS = 8
SB_HEADS = 4
DSA_HEADS = 4
N_MIX_HEADS = MLA_HEADS + SB_HEADS + DSA_HEADS
Q_LORA = 512
KV_LORA = 256
MLA_NOPE = 128
MLA_ROPE = 64
IDX_HEADS = 16
IDX_DIM = 64
IDX_SCALE = (IDX_HEADS * IDX_DIM) ** -0.5
TOPK_MAX = 256
ROPE_THETA = 10000.0
N_EXPERTS = 64
TOP_K = 8
N_GROUPS = 8
TOPK_GROUPS = 4
ROUTED_SCALE = 2.5
EPS = 1e-6

LANES = 128
VMEM_LIMIT = 56 * 1024 * 1024
NEG_INF = float("-inf")
INT_MIN = -(2 ** 31)

C_CQ, C_SBQ, C_SBK, C_SBV, C_DSQ = 0, 512, 1024, 1536, 2048
C_CKV, C_DSK, C_DSV, C_IXQ = 2560, 2816, 2944, 3072
C_KR, C_IXK, C_IXW, IN_PAD = 4096, 4224, 4352, 4608

MOE_ROWS = 256


def _cparams(sem):
    return pltpu.CompilerParams(dimension_semantics=sem, vmem_limit_bytes=VMEM_LIMIT)


def _nt_dot(a, b):
    return lax.dot_general(a, b, (((1,), (1,)), ((), ())), preferred_element_type=F32)


def _rms(x, g):
    return x * lax.rsqrt(jnp.mean(x * x, axis=-1, keepdims=True) + EPS) * g


def _ada_kernel(c_ref, w_ref, b_ref, o_ref):
    c = c_ref[...]
    a = (c * jax.nn.sigmoid(c)).astype(BF16)
    o_ref[0] = jnp.dot(a, w_ref[0].astype(BF16), preferred_element_type=F32) + b_ref[0]


def _ada(c, w_ada, b_ada):
    L, D, N = w_ada.shape
    B = c.shape[0]
    tn = 1024
    return pl.pallas_call(
        _ada_kernel,
        grid=(L, N // tn),
        in_specs=[pl.BlockSpec((B, D), lambda l, j: (0, 0)),
                  pl.BlockSpec((1, D, tn), lambda l, j: (l, 0, j)),
                  pl.BlockSpec((1, 1, tn), lambda l, j: (l, 0, j))],
        out_specs=pl.BlockSpec((1, B, tn), lambda l, j: (l, 0, j)),
        out_shape=jax.ShapeDtypeStruct((L, B, N), F32),
        compiler_params=_cparams(("parallel", "parallel")),
        name="ada_mod",
    )(c, w_ada, b_ada.reshape(L, 1, N))


def _rope_tab_kernel(pos_ref, inv_ref, sgn_ref, cos_ref, sin_ref):
    ang = pos_ref[...] * inv_ref[...]
    cos_ref[...] = jnp.cos(ang)
    sin_ref[...] = jnp.sin(ang) * sgn_ref[...]


def _rope_tables(posf, d):
    T = posf.shape[0]
    half = d // 2
    inv = ROPE_THETA ** (-jnp.arange(0, d, 2, dtype=F32) / d)
    reps = LANES // half
    inv_t = jnp.tile(inv, reps).reshape(1, LANES)
    sgn = jnp.tile(jnp.concatenate([-jnp.ones((half,), F32), jnp.ones((half,), F32)]), LANES // d).reshape(1, LANES)
    tm = min(T, 1024)
    return pl.pallas_call(
        _rope_tab_kernel,
        grid=(T // tm,),
        in_specs=[pl.BlockSpec((tm, 1), lambda i: (i, 0)),
                  pl.BlockSpec((1, LANES), lambda i: (0, 0)),
                  pl.BlockSpec((1, LANES), lambda i: (0, 0))],
        out_specs=[pl.BlockSpec((tm, LANES), lambda i: (i, 0))] * 2,
        out_shape=[jax.ShapeDtypeStruct((T, LANES), F32)] * 2,
        compiler_params=_cparams(("parallel",)),
        name=f"rope_tab{d}",
    )(posf, inv_t, sgn)


def _rope64(x, cos, sin_s, first_half):
    rot = jnp.where(first_half, pltpu.roll(x, 96, 1), pltpu.roll(x, 32, 1))
    return x * cos + rot * sin_s


def _rope128(x, cos, sin_s):
    return x * cos + pltpu.roll(x, 64, 1) * sin_s


def _nmm_kernel(*refs, modulate):
    if modulate:
        x_ref, g_ref, sc_ref, sh_ref, w_ref, o_ref, h_s = refs
    else:
        x_ref, g_ref, w_ref, o_ref, h_s = refs

    @pl.when(pl.program_id(1) == 0)
    def _():
        h = _rms(x_ref[...], g_ref[...])
        if modulate:
            h = h * (1.0 + sc_ref[0]) + sh_ref[0]
        h_s[...] = h.astype(BF16)

    o_ref[...] = jnp.dot(h_s[...], w_ref[...], preferred_element_type=F32).astype(o_ref.dtype)


def _nmm(x, col_blk, K, g, w, S, sc=None, sh=None, tm=1024, tn=512, out_dtype=F32):
    T = x.shape[0]
    N = w.shape[1]
    tm = min(tm, S)
    tn = min(tn, N)
    modulate = sc is not None
    in_specs = [pl.BlockSpec((tm, K), lambda i, j: (i, col_blk)),
                pl.BlockSpec((1, K), lambda i, j: (0, 0))]
    args = [x, g.reshape(1, K)]
    if modulate:
        B = sc.shape[0]
        in_specs += [pl.BlockSpec((1, 1, K), lambda i, j: (i * tm // S, 0, 0))] * 2
        args += [sc.reshape(B, 1, K), sh.reshape(B, 1, K)]
    in_specs.append(pl.BlockSpec((K, tn), lambda i, j: (0, j)))
    args.append(w)
    return pl.pallas_call(
        functools.partial(_nmm_kernel, modulate=modulate),
        grid=(T // tm, N // tn),
        in_specs=in_specs,
        out_specs=pl.BlockSpec((tm, tn), lambda i, j: (i, j)),
        out_shape=jax.ShapeDtypeStruct((T, N), out_dtype),
        scratch_shapes=[pltpu.VMEM((tm, K), BF16)],
        compiler_params=_cparams(("parallel", "arbitrary")),
        name="norm_matmul",
    )(*args)


def _mm_res_kernel(a_ref, w_ref, r_ref, gt_ref, o_ref):
    acc = jnp.dot(a_ref[...], w_ref[...], preferred_element_type=F32)
    o_ref[...] = r_ref[...] + gt_ref[0] * acc


def _mm_res(a, w, res, gate, S, tm=1024, tn=512):
    T, K = a.shape
    N = w.shape[1]
    B = gate.shape[0]
    tm = min(tm, S)
    return pl.pallas_call(
        _mm_res_kernel,
        grid=(T // tm, N // tn),
        in_specs=[pl.BlockSpec((tm, K), lambda i, j: (i, 0)),
                  pl.BlockSpec((K, tn), lambda i, j: (0, j)),
                  pl.BlockSpec((tm, tn), lambda i, j: (i, j)),
                  pl.BlockSpec((1, 1, tn), lambda i, j: (i * tm // S, 0, j))],
        out_specs=pl.BlockSpec((tm, tn), lambda i, j: (i, j)),
        out_shape=jax.ShapeDtypeStruct((T, N), F32),
        compiler_params=_cparams(("parallel", "parallel")),
        name="out_proj_residual",
    )(a, w, res, gate.reshape(B, 1, N))


def _prep_kernel(p_ref, q_ref, kv_ref, c64_ref, s64_ref, c128_ref, s128_ref,
                 qm_ref, km_ref, vm_ref, sbq_ref, sbk_ref, sbv_ref,
                 dq_ref, dk_ref, dv_ref, iq_ref, ika_ref, ikb_ref, iw_ref):
    c64, s64 = c64_ref[...], s64_ref[...]
    c128, s128 = c128_ref[...], s128_ref[...]
    lane = lax.broadcasted_iota(I32, c64.shape, 1)
    first = (lane % 64) < 32

    def sl(ref, off, width=LANES):
        return ref[:, off:off + width]

    kr = _rope64(sl(p_ref, C_KR), c64, s64, first).astype(BF16)
    for h in range(MLA_HEADS):
        qm_ref[:, 256 * h:256 * h + 128] = sl(q_ref, 256 * h).astype(BF16)
        qm_ref[:, 256 * h + 128:256 * h + 256] = _rope64(sl(q_ref, 256 * h + 128), c64, s64, first).astype(BF16)
        km_ref[:, 256 * h:256 * h + 128] = sl(kv_ref, 128 * h).astype(BF16)
        km_ref[:, 256 * h + 128:256 * h + 256] = kr
    vm_ref[...] = kv_ref[:, MLA_HEADS * 128:].astype(BF16)
    sbq_ref[...] = sl(p_ref, C_SBQ, 512).astype(BF16)
    sbk_ref[...] = sl(p_ref, C_SBK, 512).astype(BF16)
    sbv_ref[...] = sl(p_ref, C_SBV, 512).astype(BF16)
    for h in range(DSA_HEADS):
        dq_ref[:, 128 * h:128 * h + 128] = _rope128(sl(p_ref, C_DSQ + 128 * h), c128, s128).astype(BF16)
    dk_ref[...] = _rope128(sl(p_ref, C_DSK), c128, s128).astype(BF16)
    dv_ref[...] = sl(p_ref, C_DSV).astype(BF16)
    for j in range(IDX_HEADS // 2):
        iq_ref[:, 128 * j:128 * j + 128] = _rope64(sl(p_ref, C_IXQ + 128 * j), c64, s64, first).astype(BF16)
    ik = _rope64(sl(p_ref, C_IXK), c64, s64, first)
    ika_ref[...] = ik.astype(BF16)
    ikb_ref[...] = pltpu.roll(ik, 64, 1).astype(BF16)
    iw_ref[...] = sl(p_ref, C_IXW)


def _prep(proj, q_raw, kv_raw, tabs64, tabs128, tm=256):
    T = proj.shape[0]
    row = lambda w: pl.BlockSpec((tm, w), lambda i: (i, 0))
    widths = [2048, 2048, 1024, 512, 512, 512, 512, 128, 128, 1024, 128, 128]
    out_shape = [jax.ShapeDtypeStruct((T, w), BF16) for w in widths] + [jax.ShapeDtypeStruct((T, LANES), F32)]
    return pl.pallas_call(
        _prep_kernel,
        grid=(T // tm,),
        in_specs=[row(IN_PAD), row(2048), row(2048), row(LANES), row(LANES), row(LANES), row(LANES)],
        out_specs=[row(w) for w in widths] + [row(LANES)],
        out_shape=out_shape,
        compiler_params=_cparams(("parallel",)),
        name="mixer_prep",
    )(proj, q_raw, kv_raw, *tabs64, *tabs128)


def _head_out(o, g):
    return _rms(o, g).astype(BF16)


LOG2E = 1.4426950408889634


def _mla_kernel(q_ref, k_ref, vt_ref, g_ref, o_ref, m_s, l_s, acc_s, *, t, scale):
    i = pl.program_id(2)
    hps = m_s.shape[0]
    c = scale * LOG2E
    m_s[...] = jnp.full(m_s.shape, NEG_INF, F32)
    l_s[...] = jnp.zeros(l_s.shape, F32)
    acc_s[...] = jnp.zeros(acc_s.shape, F32)

    def step(cidx, masked):
        k0 = pl.multiple_of(cidx * t, t)
        for hh in range(hps):
            s = _nt_dot(k_ref[pl.ds(k0, t), 256 * hh:256 * hh + 256], q_ref[:, 256 * hh:256 * hh + 256])
            if masked:
                key = lax.broadcasted_iota(I32, (t, t), 0)
                qry = lax.broadcasted_iota(I32, (t, t), 1)
                s = jnp.where(key <= qry, s, NEG_INF)
            m_prev = m_s[hh]
            m_new = jnp.maximum(m_prev, jnp.max(s, axis=0, keepdims=True))
            alpha = jnp.exp2((m_prev - m_new) * c)
            p = jnp.exp2((s - m_new) * c)
            l_s[hh] = alpha * l_s[hh] + jnp.sum(p, axis=0, keepdims=True)
            acc_s[hh] = alpha * acc_s[hh] + jnp.dot(vt_ref[128 * hh:128 * hh + 128, pl.ds(k0, t)], p.astype(BF16),
                                                    preferred_element_type=F32)
            m_s[hh] = m_new

    def body(cidx, carry):
        step(cidx, False)
        return carry

    lax.fori_loop(0, i, body, 0)
    step(i, True)
    for hh in range(hps):
        o_ref[:, 128 * hh:128 * hh + 128] = _head_out((acc_s[hh] / l_s[hh]).T, g_ref[:, 128 * hh:128 * hh + 128])


def _mla(qm, km, vmt, g_out, B, S, t=512, hps=2):
    T = qm.shape[0]
    t = min(t, S)
    nq = S // t
    H = MLA_HEADS
    return pl.pallas_call(
        functools.partial(_mla_kernel, t=t, scale=(MLA_NOPE + MLA_ROPE) ** -0.5),
        grid=(B, H // hps, nq),
        in_specs=[pl.BlockSpec((t, 256 * hps), lambda b, h, i: (b * nq + i, h)),
                  pl.BlockSpec((S, 256 * hps), lambda b, h, i: (b, h)),
                  pl.BlockSpec((128 * hps, S), lambda b, h, i: (h, b)),
                  pl.BlockSpec((1, 128 * hps), lambda b, h, i: (0, h))],
        out_specs=pl.BlockSpec((t, 128 * hps), lambda b, h, i: (b * nq + i, h)),
        out_shape=jax.ShapeDtypeStruct((T, H * HEAD_DIM), BF16),
        scratch_shapes=[pltpu.VMEM((hps, 1, t), F32), pltpu.VMEM((hps, 1, t), F32), pltpu.VMEM((hps, 128, t), F32)],
        compiler_params=_cparams(("parallel", "parallel", "arbitrary")),
        name="mla_attention",
    )(qm, km, vmt, g_out)


def _sb_kernel(q_ref, k_ref, vt_ref, g_ref, tri_ref, o_ref, carry_s, acc_s, *, t, scale):
    i = pl.program_id(2)
    hps = carry_s.shape[0]
    tri = tri_ref[...]
    carry_s[...] = jnp.zeros(carry_s.shape, F32)
    acc_s[...] = jnp.zeros(acc_s.shape, F32)

    def step(cidx, masked):
        k0 = pl.multiple_of(cidx * t, t)
        for hh in range(hps):
            z = _nt_dot(k_ref[pl.ds(k0, t), 128 * hh:128 * hh + 128], q_ref[:, 128 * hh:128 * hh + 128]) * scale
            sp = jnp.log(1.0 + jnp.exp(-jnp.abs(z)))
            log_beta = jnp.minimum(z, 0.0) - sp
            log_keep = jnp.minimum(-z, 0.0) - sp
            if masked:
                key = lax.broadcasted_iota(I32, (t, t), 0)
                qry = lax.broadcasted_iota(I32, (t, t), 1)
                strict = key < qry
                log_keep = jnp.where(strict, log_keep, 0.0)
            hi = log_keep.astype(BF16)
            lo = (log_keep - hi.astype(F32)).astype(BF16)
            suffix = jnp.dot(tri, hi, preferred_element_type=F32) + jnp.dot(tri, lo, preferred_element_type=F32)
            a = jnp.exp(log_beta + suffix + carry_s[hh])
            if masked:
                a = jnp.where(strict, a, 0.0)
            acc_s[hh] += jnp.dot(vt_ref[128 * hh:128 * hh + 128, pl.ds(k0, t)], a.astype(BF16),
                                 preferred_element_type=F32)
            carry_s[hh] += jnp.sum(log_keep, axis=0, keepdims=True)

    step(i, True)

    def body(j, carry):
        step(i - 1 - j, False)
        return carry

    lax.fori_loop(0, i, body, 0)
    for hh in range(hps):
        o_ref[:, 128 * hh:128 * hh + 128] = _head_out(acc_s[hh].T, g_ref[:, 128 * hh:128 * hh + 128])


def _sb(sbq, sbk, sbvt, g_out, B, S, t=512, hps=2):
    T = sbq.shape[0]
    t = min(t, S)
    nq = S // t
    H = SB_HEADS
    r = np.arange(t)
    tri = jnp.asarray((r[None, :] > r[:, None]).astype(np.float32), BF16)
    g0 = MLA_HEADS // hps
    return pl.pallas_call(
        functools.partial(_sb_kernel, t=t, scale=HEAD_DIM ** -0.5),
        grid=(B, H // hps, nq),
        in_specs=[pl.BlockSpec((t, 128 * hps), lambda b, h, i: (b * nq + i, h)),
                  pl.BlockSpec((S, 128 * hps), lambda b, h, i: (b, h)),
                  pl.BlockSpec((128 * hps, S), lambda b, h, i: (h, b)),
                  pl.BlockSpec((1, 128 * hps), lambda b, h, i: (0, g0 + h)),
                  pl.BlockSpec((t, t), lambda b, h, i: (0, 0))],
        out_specs=pl.BlockSpec((t, 128 * hps), lambda b, h, i: (b * nq + i, h)),
        out_shape=jax.ShapeDtypeStruct((T, H * HEAD_DIM), BF16),
        scratch_shapes=[pltpu.VMEM((hps, 1, t), F32), pltpu.VMEM((hps, 128, t), F32)],
        compiler_params=_cparams(("parallel", "parallel", "arbitrary")),
        name="stickbreak_attention",
    )(sbq, sbk, sbvt, g_out, tri)


def _dsa_kernel(iq_ref, ika_ref, ikb_ref, wt_ref, dq_ref, dk_ref, vt_ref, g_ref, o_ref,
                key_s, m_s, l_s, acc_s, *, tq, tk, n_sel, scale):
    i = pl.program_id(1)
    nch = (i * tq + tq + tk - 1) // tk
    qpos = i * tq + lax.broadcasted_iota(I32, (1, tq), 1)
    kiota = lax.broadcasted_iota(I32, (tk, 1), 0)
    wt = wt_ref[0]
    half = IDX_HEADS // 2

    def score_chunk(c, carry):
        k0 = pl.multiple_of(c * tk, tk)
        kk = jnp.concatenate([ika_ref[pl.ds(k0, tk), :], ikb_ref[pl.ds(k0, tk), :]], axis=0)
        score = jnp.zeros((tk, tq), F32)
        for j in range(half):
            r = jnp.maximum(_nt_dot(kk, iq_ref[:, 128 * j:128 * j + 128]), 0.0)
            score = score + r[:tk] * wt[2 * j:2 * j + 1, :] + r[tk:] * wt[2 * j + 1:2 * j + 2, :]
        score = score * IDX_SCALE
        score = jnp.where(k0 + kiota <= qpos, score, NEG_INF)
        bits = pltpu.bitcast(score, I32)
        key_s[pl.ds(k0, tk), :] = jnp.where(bits < 0, bits ^ jnp.int32(0x7FFFFFFF), bits)
        return carry

    lax.fori_loop(0, nch, score_chunk, 0)

    def count(pred):
        def body(c, acc):
            k0 = pl.multiple_of(c * tk, tk)
            hit = pred(key_s[pl.ds(k0, tk), :], k0 + kiota).astype(I32)
            return acc + jnp.sum(hit.reshape(tk // 8, 8, tq), axis=0)
        part = lax.fori_loop(0, nch, body, jnp.zeros((8, tq), I32))
        return jnp.sum(part, axis=0, keepdims=True)

    c0 = count(lambda k, idx: k >= 0)
    thr = jnp.where(c0 >= n_sel, jnp.int32(0), jnp.int32(INT_MIN))

    def vbit(b, thr):
        cand = thr + jnp.left_shift(jnp.int32(1), 30 - b)
        cnt = count(lambda k, idx: k >= cand)
        return jnp.where(cnt >= n_sel, cand, thr)

    thr = lax.fori_loop(0, 31, vbit, thr)
    n_ge = count(lambda k, idx: k >= thr)
    idx_bits = max(1, int(np.ceil(np.log2(key_s.shape[0]))))

    def tie_bound():
        need = n_sel - count(lambda k, idx: k > thr)

        def ibit(b, bound):
            cand = bound + jnp.left_shift(jnp.int32(1), idx_bits - 1 - b)
            cnt = count(lambda k, idx: (k == thr) & (idx < cand))
            return jnp.where(cnt < need, cand, bound)

        return lax.fori_loop(0, idx_bits, ibit, jnp.zeros((1, tq), I32))

    bound = lax.cond(jnp.max(n_ge) > n_sel, tie_bound, lambda: jnp.full((1, tq), 2 ** idx_bits, I32))

    qc = jnp.concatenate([dq_ref[:, 128 * h:128 * h + 128] for h in range(DSA_HEADS)], axis=0)
    sc2 = scale * LOG2E
    m_s[...] = jnp.full(m_s.shape, NEG_INF, F32)
    l_s[...] = jnp.zeros(l_s.shape, F32)
    acc_s[...] = jnp.zeros(acc_s.shape, F32)

    def attn_chunk(c, carry):
        k0 = pl.multiple_of(c * tk, tk)
        key = key_s[pl.ds(k0, tk), :]
        idx = k0 + kiota
        sel = ((key > thr) | ((key == thr) & (idx <= bound))) & (idx <= qpos)
        bias = jnp.where(sel, 0.0, NEG_INF)
        s = _nt_dot(dk_ref[pl.ds(k0, tk), :], qc) + jnp.concatenate([bias] * DSA_HEADS, axis=1)
        m_prev = m_s[...]
        m_new = jnp.maximum(m_prev, jnp.max(s, axis=0, keepdims=True))
        m_safe = jnp.where(m_new == NEG_INF, 0.0, m_new)
        alpha = jnp.exp2((m_prev - m_safe) * sc2)
        p = jnp.exp2((s - m_safe) * sc2)
        l_s[...] = alpha * l_s[...] + jnp.sum(p, axis=0, keepdims=True)
        acc_s[...] = alpha * acc_s[...] + jnp.dot(vt_ref[:, pl.ds(k0, tk)], p.astype(BF16),
                                                  preferred_element_type=F32)
        m_s[...] = m_new
        return carry

    lax.fori_loop(0, nch, attn_chunk, 0)
    ot = acc_s[...] / l_s[...]
    for h in range(DSA_HEADS):
        o_ref[:, 128 * h:128 * h + 128] = _head_out(ot[:, h * tq:(h + 1) * tq].T, g_ref[:, 128 * h:128 * h + 128])


def _dsa(iq, ika, ikb, wt, dq, dk, vt, g_out, B, S, tq=256, tk=256):
    T = iq.shape[0]
    tk = min(tk, S)
    nq = S // tq
    H = DSA_HEADS
    n_sel = min(TOPK_MAX, S // 4)
    g_c = g_out[:, (MLA_HEADS + SB_HEADS) * HEAD_DIM:]
    return pl.pallas_call(
        functools.partial(_dsa_kernel, tq=tq, tk=tk, n_sel=n_sel, scale=HEAD_DIM ** -0.5),
        grid=(B, nq),
        in_specs=[pl.BlockSpec((tq, IDX_HEADS * IDX_DIM), lambda b, i: (b * nq + i, 0)),
                  pl.BlockSpec((S, 128), lambda b, i: (b, 0)),
                  pl.BlockSpec((S, 128), lambda b, i: (b, 0)),
                  pl.BlockSpec((1, IDX_HEADS, tq), lambda b, i: (b, 0, i)),
                  pl.BlockSpec((tq, H * 128), lambda b, i: (b * nq + i, 0)),
                  pl.BlockSpec((S, 128), lambda b, i: (b, 0)),
                  pl.BlockSpec((128, S), lambda b, i: (0, b)),
                  pl.BlockSpec((1, H * 128), lambda b, i: (0, 0))],
        out_specs=pl.BlockSpec((tq, H * 128), lambda b, i: (b * nq + i, 0)),
        out_shape=jax.ShapeDtypeStruct((T, H * HEAD_DIM), BF16),
        scratch_shapes=[pltpu.VMEM((S, tq), I32), pltpu.VMEM((1, H * tq), F32), pltpu.VMEM((1, H * tq), F32),
                        pltpu.VMEM((128, H * tq), F32)],
        compiler_params=_cparams(("parallel", "arbitrary")),
        name="dsa_attention",
    )(iq, ika, ikb, wt, dq, dk, vt, g_c)


def _route_kernel(x_ref, g_ref, sc_ref, sh_ref, wr_ref, br_ref, tri_ref,
                  h_ref, hp_ref, eidx_ref, gate_ref, pos_ref, cnt_ref, run_s):
    @pl.when(pl.program_id(0) == 0)
    def _():
        run_s[...] = jnp.zeros(run_s.shape, F32)

    h = _rms(x_ref[...], g_ref[...]) * (1.0 + sc_ref[0]) + sh_ref[0]
    hb = h.astype(BF16)
    h_ref[...] = hb
    tm, D = h.shape
    bits = pltpu.bitcast(hb.astype(F32), jnp.uint32)
    hp_ref[...] = (bits[:, :D // 2] >> 16) | (bits[:, D // 2:] & jnp.uint32(0xFFFF0000))
    logits = _nt_dot(wr_ref[...], hb)
    scores = jax.nn.sigmoid(logits)
    biased = scores + br_ref[...]
    gsz = N_EXPERTS // N_GROUPS
    b3 = biased.reshape(N_GROUPS, gsz, tm)
    m1 = jnp.max(b3, axis=1, keepdims=True)
    n1 = jnp.sum((b3 == m1).astype(F32), axis=1, keepdims=True)
    m2 = jnp.max(jnp.where(b3 < m1, b3, NEG_INF), axis=1, keepdims=True)
    grp = (m1 + jnp.where(n1 >= 2.0, m1, m2)).reshape(N_GROUPS, tm)
    gi = lax.broadcasted_iota(I32, (N_GROUPS, 1), 0)
    grank = jnp.zeros((N_GROUPS, tm), F32)
    for g in range(N_GROUPS):
        rowv = grp[g:g + 1, :]
        grank = grank + jnp.where((rowv > grp) | ((rowv == grp) & (g < gi)), 1.0, 0.0)
    gmask = grank.reshape(N_GROUPS, 1, tm) < float(TOPK_GROUPS)
    masked = jnp.where(gmask, b3, NEG_INF).reshape(N_EXPERTS, tm)
    ei = lax.broadcasted_iota(I32, (N_EXPERTS, 1), 0)
    rank = jnp.zeros((N_EXPERTS, tm), F32)
    for e in range(N_EXPERTS):
        rowv = masked[e:e + 1, :]
        rank = rank + jnp.where((rowv > masked) | ((rowv == masked) & (e < ei)), 1.0, 0.0)
    sel = rank < float(TOP_K)
    selm = sel.astype(F32)
    gsum = jnp.sum(scores * selm, axis=0, keepdims=True)
    gate = scores * selm / gsum * ROUTED_SCALE
    within = jnp.dot(selm.astype(BF16), tri_ref[...], preferred_element_type=F32)
    posf = within + run_s[...]
    run_s[...] += jnp.sum(selm, axis=1, keepdims=True)
    cnt_ref[...] = jnp.broadcast_to(run_s[...], cnt_ref.shape)
    eif = ei.astype(F32)
    for k in range(TOP_K):
        onek = rank == float(k)
        eidx_ref[k:k + 1, :] = jnp.sum(jnp.where(onek, eif, 0.0), axis=0, keepdims=True).astype(I32)
        gate_ref[k:k + 1, :] = jnp.sum(jnp.where(onek, gate, 0.0), axis=0, keepdims=True)
        pos_ref[k:k + 1, :] = jnp.sum(jnp.where(onek, posf, 0.0), axis=0, keepdims=True).astype(I32)


def _route(x, g, sc, sh, w_router, b_router, S, tm=512):
    T, D = x.shape
    B = sc.shape[0]
    tm = min(tm, S)
    r = np.arange(tm)
    tri = jnp.asarray((r[:, None] < r[None, :]).astype(np.float32), BF16)
    full = lambda shape: pl.BlockSpec(shape, lambda i: (0,) * len(shape))
    return pl.pallas_call(
        _route_kernel,
        grid=(T // tm,),
        in_specs=[pl.BlockSpec((tm, D), lambda i: (i, 0)),
                  full((1, D)),
                  pl.BlockSpec((1, 1, D), lambda i: (i * tm // S, 0, 0)),
                  pl.BlockSpec((1, 1, D), lambda i: (i * tm // S, 0, 0)),
                  full((N_EXPERTS, D)),
                  full((N_EXPERTS, 1)),
                  full((tm, tm))],
        out_specs=[pl.BlockSpec((tm, D), lambda i: (i, 0)),
                   pl.BlockSpec((tm, D // 2), lambda i: (i, 0)),
                   pl.BlockSpec((TOP_K, tm), lambda i: (0, i)),
                   pl.BlockSpec((TOP_K, tm), lambda i: (0, i)),
                   pl.BlockSpec((TOP_K, tm), lambda i: (0, i)),
                   full((N_EXPERTS, LANES))],
        out_shape=[jax.ShapeDtypeStruct((T, D), BF16),
                   jax.ShapeDtypeStruct((T, D // 2), jnp.uint32),
                   jax.ShapeDtypeStruct((TOP_K, T), I32),
                   jax.ShapeDtypeStruct((TOP_K, T), F32),
                   jax.ShapeDtypeStruct((TOP_K, T), I32),
                   jax.ShapeDtypeStruct((N_EXPERTS, LANES), F32)],
        scratch_shapes=[pltpu.VMEM((N_EXPERTS, 1), F32)],
        compiler_params=_cparams(("arbitrary",)),
        name="ffn_norm_route",
    )(x, g.reshape(1, D), sc.reshape(B, 1, D), sh.reshape(B, 1, D),
      w_router.T.astype(BF16), b_router.reshape(N_EXPERTS, 1), tri)


def _moe_kernel(bexp_ref, bval_ref, slot_hbm, h_hbm, wg_ref, wu_ref, wd_ref, y_hbm,
                xbuf0, xbuf1, x_s, mid_s, ybuf0, ybuf1, wg_s, wu_s, wd_s, idx_s, isem, gsem, ssem, *, n_tok):
    b = pl.program_id(0)
    nb = pl.num_programs(0)
    rows, D = x_s.shape
    half = D // 2
    DE = mid_s.shape[1]
    YS = D // LANES
    NW = 256
    xbufs, ybufs = (xbuf0, xbuf1), (ybuf0, ybuf1)
    n_real = TOP_K * n_tok

    def ring(blk):
        return (blk & 3) * rows

    def idx_copy(blk):
        return pltpu.make_async_copy(slot_hbm.at[blk], idx_s.at[pl.ds(pl.multiple_of(ring(blk), rows), rows)],
                                     isem.at[blk & 3])

    def gather_start(base, r, p):
        tok = idx_s[base + r] & (n_tok - 1)
        pltpu.make_async_copy(h_hbm.at[pl.ds(pl.multiple_of(tok * 8, 8), 8), :],
                              xbufs[p].at[pl.ds(r * 8, 8), :], gsem.at[p]).start(priority=r % 2)

    def scatter_start(base, r, p):
        dst = idx_s[base + r]
        pltpu.make_async_copy(ybufs[p].at[pl.ds(r * YS, YS), :],
                              y_hbm.at[pl.ds(pl.multiple_of(dst * YS, YS), YS), :], ssem.at[p]).start(priority=r % 2)

    def wait_gather(p):
        pltpu.make_async_copy(xbufs[p], xbufs[p], gsem.at[p]).wait()

    def wait_scatter(p):
        pltpu.make_async_copy(ybufs[p], ybufs[p], ssem.at[p]).wait()

    def compute_pieces(p):
        def unpack():
            for s in range(8):
                w = xbufs[p][pl.ds(s, rows, stride=8), :]
                x_s[:, 128 * s:128 * s + 128] = pltpu.bitcast(w << 16, F32).astype(BF16)
                x_s[:, half + 128 * s:half + 128 * s + 128] = pltpu.bitcast(w & jnp.uint32(0xFFFF0000), F32).astype(BF16)

        def gate_up(j):
            def piece():
                x = x_s[...]
                gt = jnp.dot(x, wg_s[:, NW * j:NW * j + NW], preferred_element_type=F32)
                up = jnp.dot(x, wu_s[:, NW * j:NW * j + NW], preferred_element_type=F32)
                mid_s[:, NW * j:NW * j + NW] = (gt * jax.nn.sigmoid(gt) * up).astype(BF16)
            return piece

        def down(n):
            def piece():
                y = jnp.dot(mid_s[...], wd_s[:, NW * n:NW * n + NW], preferred_element_type=F32)
                for s in range(NW // LANES):
                    ybufs[p][pl.ds(n * (NW // LANES) + s, rows, stride=YS), :] = y[:, LANES * s:LANES * s + LANES]
            return piece
        return [unpack] + [gate_up(j) for j in range(DE // NW)] + [down(n) for n in range(D // NW)]

    def run_block(p, compute):
        gbase, sbase = ring(b + 1), ring(b - 1)
        pieces = compute_pieces(p) if compute else []
        starts = [functools.partial(gather_start, gbase, r, 1 - p) for r in range(rows)]
        starts += [functools.partial(scatter_start, sbase, r, 1 - p) for r in range(rows)]
        n_slots = max(len(pieces), 1)
        per = -(-len(starts) // n_slots)
        for n in range(n_slots):
            if pieces:
                pieces[n]()
            for st in starts[n * per:(n + 1) * per]:
                st()

    @pl.when(b == 0)
    def _():
        for r in range(rows):
            idx_s[3 * rows + r] = n_real + rows + r
        ybuf0[...] = jnp.zeros(ybuf0.shape, F32)
        ybuf1[...] = jnp.zeros(ybuf1.shape, F32)
        init = pltpu.make_async_copy(ybuf0, y_hbm.at[pl.ds(n_real * YS, rows * YS), :], ssem.at[0])
        init.start()
        init.wait()
        idx_copy(0).start()
        idx_copy(0).wait()
        for r in range(rows):
            gather_start(0, r, 0)
        idx_copy(1).start()

    @pl.when(b + 2 < nb)
    def _():
        idx_copy(b + 2).start()

    @pl.when(b + 1 < nb)
    def _():
        idx_copy(b + 1).wait()

    @pl.when((b == 0) | (bexp_ref[b] != bexp_ref[jnp.maximum(b - 1, 0)]))
    def _():
        wg_s[...] = wg_ref[0, 0].astype(BF16)
        wu_s[...] = wu_ref[0, 0].astype(BF16)
        wd_s[...] = wd_ref[0, 0].astype(BF16)

    active = bval_ref[b] > 0
    for p in range(2):
        @pl.when((b & 1) == p)
        def _():
            wait_gather(p)

            @pl.when(b >= 1)
            def _():
                wait_scatter(p)

            @pl.when(active)
            def _():
                run_block(p, True)

            @pl.when(jnp.logical_not(active))
            def _():
                run_block(p, False)

            @pl.when(b == nb - 1)
            def _():
                sbase = ring(b)
                for r in range(rows):
                    scatter_start(sbase, r, p)
                wait_scatter(p)
                wait_scatter(1 - p)
                wait_gather(1 - p)


def _moe(hp, slot_ids, blk_exp, blk_val, w_gate, w_up, w_down, l):
    T = hp.shape[0] // 8
    D = w_gate.shape[2]
    assert T & (T - 1) == 0 and D == 2 * 8 * LANES
    DE = w_gate.shape[3]
    nb = blk_exp.shape[0]
    wspec = lambda shape: pl.BlockSpec((1, 1) + shape, lambda b, bexp, bval: (l, bexp[b], 0, 0))
    grid_spec = pltpu.PrefetchScalarGridSpec(
        num_scalar_prefetch=2,
        grid=(nb,),
        in_specs=[pl.BlockSpec(memory_space=pl.ANY), pl.BlockSpec(memory_space=pl.ANY),
                  wspec((D, DE)), wspec((D, DE)), wspec((DE, D))],
        out_specs=pl.BlockSpec(memory_space=pl.ANY),
        scratch_shapes=[pltpu.VMEM((MOE_ROWS * 8, LANES), jnp.uint32), pltpu.VMEM((MOE_ROWS * 8, LANES), jnp.uint32),
                        pltpu.VMEM((MOE_ROWS, D), BF16), pltpu.VMEM((MOE_ROWS, DE), BF16),
                        pltpu.VMEM((MOE_ROWS * D // LANES, LANES), F32), pltpu.VMEM((MOE_ROWS * D // LANES, LANES), F32),
                        pltpu.VMEM((D, DE), BF16), pltpu.VMEM((D, DE), BF16), pltpu.VMEM((DE, D), BF16),
                        pltpu.SMEM((4 * MOE_ROWS,), I32),
                        pltpu.SemaphoreType.DMA((4,)), pltpu.SemaphoreType.DMA((2,)), pltpu.SemaphoreType.DMA((2,))],
    )
    return pl.pallas_call(
        functools.partial(_moe_kernel, n_tok=T),
        grid_spec=grid_spec,
        out_shape=jax.ShapeDtypeStruct(((TOP_K * T + 2 * MOE_ROWS) * D // LANES, LANES), F32),
        compiler_params=_cparams(("arbitrary",)),
        name="moe_experts",
    )(blk_exp, blk_val, slot_ids.reshape(nb, MOE_ROWS), hp, w_gate, w_up, w_down)


def _ffn_out_kernel(*refs, final):
    y_refs = refs[:TOP_K]
    gk_ref, h_ref, x_ref, gf_ref, wsg_ref, wsu_ref, wsd_ref, fg_ref, o_ref = refs[TOP_K:]
    tm, D = x_ref.shape
    ys = D // LANES
    gates = [gk_ref[k] for k in range(TOP_K)]
    chunks = []
    for s in range(ys):
        acc = y_refs[0][pl.ds(s, tm, stride=ys), :] * gates[0]
        for k in range(1, TOP_K):
            acc = acc + y_refs[k][pl.ds(s, tm, stride=ys), :] * gates[k]
        chunks.append(acc)
    routed = jnp.concatenate(chunks, axis=1)
    hb = h_ref[...]
    gt = jnp.dot(hb, wsg_ref[...], preferred_element_type=F32)
    up = jnp.dot(hb, wsu_ref[...], preferred_element_type=F32)
    mid = (gt * jax.nn.sigmoid(gt) * up).astype(BF16)
    shared = jnp.dot(mid, wsd_ref[...], preferred_element_type=F32)
    out = x_ref[...] + gf_ref[0] * (routed + shared)
    if final:
        out = _rms(out, fg_ref[...])
    o_ref[...] = out


def _ffn_out(y, gate3, h, x, g_f, wsg, wsu, wsd, final_g, final, S, tm=128):
    T, D = x.shape
    B = g_f.shape[0]
    DS = wsg.shape[1]
    tm = min(tm, S)
    nt = T // tm
    full = lambda shape: pl.BlockSpec(shape, lambda i: (0,) * len(shape))
    y_specs = [pl.BlockSpec((tm * D // LANES, LANES), functools.partial(lambda i, k: (k * nt + i, 0), k=k))
               for k in range(TOP_K)]
    return pl.pallas_call(
        functools.partial(_ffn_out_kernel, final=final),
        grid=(nt,),
        in_specs=y_specs + [
                  pl.BlockSpec((TOP_K, tm, 1), lambda i: (0, i, 0)),
                  pl.BlockSpec((tm, D), lambda i: (i, 0)),
                  pl.BlockSpec((tm, D), lambda i: (i, 0)),
                  pl.BlockSpec((1, 1, D), lambda i: (i * tm // S, 0, 0)),
                  full((D, DS)), full((D, DS)), full((DS, D)), full((1, D))],
        out_specs=pl.BlockSpec((tm, D), lambda i: (i, 0)),
        out_shape=jax.ShapeDtypeStruct((T, D), F32),
        compiler_params=_cparams(("parallel",)),
        name="ffn_combine",
    )(*([y] * TOP_K), gate3, h, x, g_f.reshape(B, 1, D), wsg, wsu, wsd, final_g.reshape(1, D))


def _layout_w_in(w):
    D = w.shape[0]
    sizes = (Q_LORA, KV_LORA, MLA_ROPE, 512, 512, 512, 512, HEAD_DIM, HEAD_DIM, IDX_HEADS * IDX_DIM, IDX_DIM, IDX_HEADS)
    offs = np.concatenate([[0], np.cumsum(sizes)])
    cq, ckv, kr, sbq, sbk, sbv, dsq, dsk, dsv, ixq, ixk, ixw = [w[:, offs[n]:offs[n + 1]] for n in range(12)]
    z = lambda n: jnp.zeros((D, n), w.dtype)
    out = jnp.concatenate([cq, sbq, sbk, sbv, dsq, ckv, dsk, dsv, ixq, kr, z(64), ixk, z(64), ixw, z(112), z(128)], axis=1)
    assert out.shape[1] == IN_PAD
    return out.astype(BF16)


def _layout_w_uq(w):
    K = w.shape[0]
    w3 = w.reshape(K, MLA_HEADS, MLA_NOPE + MLA_ROPE)
    w3 = jnp.concatenate([w3, jnp.zeros((K, MLA_HEADS, 64), w.dtype)], axis=2)
    return w3.reshape(K, MLA_HEADS * 256).astype(BF16)


def _layout_w_ukv(w):
    K = w.shape[0]
    w3 = w.reshape(K, MLA_HEADS, MLA_NOPE + HEAD_DIM)
    return jnp.concatenate([w3[:, :, :MLA_NOPE].reshape(K, -1), w3[:, :, MLA_NOPE:].reshape(K, -1)], axis=1).astype(BF16)


def _expert_blocks(eidx, pos, counts, T):
    M = T * TOP_K
    nb = M // MOE_ROWS + N_EXPERTS
    P = nb * MOE_ROWS
    padded = (counts + MOE_ROWS - 1) // MOE_ROWS * MOE_ROWS
    pend = jnp.cumsum(padded)
    pstart = pend - padded
    experts = jnp.arange(N_EXPERTS, dtype=I32)
    dest = jnp.sum(jnp.where(eidx[:, :, None] == experts, pstart, 0), axis=-1) + pos
    out_row = jnp.arange(TOP_K, dtype=I32)[:, None] * T + jnp.arange(T, dtype=I32)[None, :]
    pad_row = M + ((jnp.arange(P, dtype=I32) // MOE_ROWS) % 2) * MOE_ROWS + jnp.arange(P, dtype=I32) % MOE_ROWS
    slot_ids = pad_row.at[dest.reshape(-1)].set(out_row.reshape(-1), unique_indices=True)
    blk_start = jnp.arange(nb, dtype=I32) * MOE_ROWS
    blk_exp = jnp.minimum(jnp.sum((pend[None, :] <= blk_start[:, None]).astype(I32), axis=1), N_EXPERTS - 1)
    real_end = jnp.sum(jnp.where(blk_exp[:, None] == experts, pstart + counts, 0), axis=-1)
    blk_val = jnp.clip(real_end - blk_start, 0, MOE_ROWS).astype(I32)
    return slot_ids, blk_exp, blk_val


def _mixers(x, S, B, positions_tabs, mod, l, w):
    sh_m, sc_m, g_m = mod[0], mod[1], mod[2]
    proj = _nmm(x, 0, x.shape[1], w["norm_mix_g"][l], w["w_in"][l], S, sc=sc_m, sh=sh_m)
    q_raw = _nmm(proj, C_CQ // Q_LORA, Q_LORA, w["g_cq"][l], w["w_uq"][l], S)
    kv_raw = _nmm(proj, C_CKV // KV_LORA, KV_LORA, w["g_ckv"][l], w["w_ukv"][l], S)
    tabs64, tabs128 = positions_tabs
    (qm, km, vm, sbq, sbk, sbv, dq, dk, dv, iq, ika, ikb, iw) = _prep(proj, q_raw, kv_raw, tabs64, tabs128)
    g_out = w["g_out"][l].reshape(1, -1)
    o_a = _mla(qm, km, vm.T, g_out, B, S)
    o_b = _sb(sbq, sbk, sbv.T, g_out, B, S)
    wt = iw[:, :IDX_HEADS].reshape(B, S, IDX_HEADS).transpose(0, 2, 1)
    o_c = _dsa(iq, ika, ikb, wt, dq, dk, dv.T, g_out, B, S)
    o = jnp.concatenate([o_a, o_b, o_c], axis=1)
    return _mm_res(o, w["w_o"][l], x, g_m, S)


def _ffn(x, S, B, mod, l, w, final_g, final):
    T, D = x.shape
    sh_f, sc_f, g_f = mod[3], mod[4], mod[5]
    h, hp, eidx, gate, pos, cnt = _route(x, w["norm_ffn_g"][l], sc_f, sh_f, w["w_router"][l], w["b_router"][l], S)
    counts = cnt[:, 0].astype(I32)
    slot_ids, blk_exp, blk_val = _expert_blocks(eidx, pos, counts, T)
    y = _moe(hp.reshape(T * 8, LANES), slot_ids, blk_exp, blk_val, w["w_gate"], w["w_up"], w["w_down"], l)
    return _ffn_out(y, gate.reshape(TOP_K, T, 1), h, x, g_f, w["ws_gate"][l], w["ws_up"][l], w["ws_down"][l],
                    final_g, final, S)


def kernel(x, c, positions, norm_mix_g, norm_ffn_g, w_ada, b_ada, w_in, g_cq, g_ckv, w_uq, w_ukv, g_out, w_o,
           w_router, b_router, w_gate, w_up, w_down, ws_gate, ws_up, ws_down, final_g):
    B, S, D = x.shape
    L = w_in.shape[0]
    T = B * S
    w = dict(norm_mix_g=norm_mix_g, norm_ffn_g=norm_ffn_g, g_cq=g_cq, g_ckv=g_ckv, g_out=g_out,
             w_in=jax.vmap(_layout_w_in)(w_in), w_uq=jax.vmap(_layout_w_uq)(w_uq), w_ukv=jax.vmap(_layout_w_ukv)(w_ukv),
             w_o=w_o.astype(BF16), w_router=w_router, b_router=b_router, w_gate=w_gate, w_up=w_up, w_down=w_down,
             ws_gate=ws_gate.astype(BF16), ws_up=ws_up.astype(BF16), ws_down=ws_down.astype(BF16))
    mod_all = _ada(c, w_ada, b_ada).reshape(L, B, 6, D)
    posf = positions.astype(F32).reshape(T, 1)
    tabs = (_rope_tables(posf, 64), _rope_tables(posf, 128))
    xt = x.reshape(T, D)
    for l in range(L):
        mod = [mod_all[l, :, n, :] for n in range(6)]
        xt = _mixers(xt, S, B, tabs, mod, l, w)
        xt = _ffn(xt, S, B, mod, l, w, final_g, l == L - 1)
    return xt.reshape(B, S, D)
```

```python
import functools

import jax
import jax.numpy as jnp
import numpy as np
from jax import lax
from jax.experimental import pallas as pl
from jax.experimental.pallas import tpu as pltpu

F32 = jnp.float32
BF16 = jnp.bfloat16
I32 = jnp.int32

HEAD_DIM = 128
MLA_HEADS = 8
SB_HEADS = 4
DSA_HEADS = 4
N_MIX_HEADS = MLA_HEADS + SB_HEADS + DSA_HEADS
Q_LORA = 512
KV_LORA = 256
MLA_NOPE = 128
MLA_ROPE = 64
IDX_HEADS = 16
IDX_DIM = 64
IDX_SCALE = (IDX_HEADS * IDX_DIM) ** -0.5
TOPK_MAX = 256
ROPE_THETA = 10000.0
N_EXPERTS = 64
TOP_K = 8
N_GROUPS = 8
TOPK_GROUPS = 4
ROUTED_SCALE = 2.5
EPS = 1e-6

LANES = 128
VMEM_LIMIT = 56 * 1024 * 1024
NEG_INF = float("-inf")
INT_MIN = -(2 ** 31)

C_CQ, C_SBQ, C_SBK, C_SBV, C_DSQ = 0, 512, 1024, 1536, 2048
C_CKV, C_DSK, C_DSV, C_IXQ = 2560, 2816, 2944, 3072
C_KR, C_IXK, C_IXW, IN_PAD = 4096, 4224, 4352, 4608

MOE_ROWS = 256


def _cparams(sem):
    return pltpu.CompilerParams(dimension_semantics=sem, vmem_limit_bytes=VMEM_LIMIT)


def _nt_dot(a, b):
    return lax.dot_general(a, b, (((1,), (1,)), ((), ())), preferred_element_type=F32)


def _rms(x, g):
    return x * lax.rsqrt(jnp.mean(x * x, axis=-1, keepdims=True) + EPS) * g


def _ada_kernel(c_ref, w_ref, b_ref, o_ref):
    c = c_ref[...]
    a = (c * jax.nn.sigmoid(c)).astype(BF16)
    o_ref[0] = jnp.dot(a, w_ref[0].astype(BF16), preferred_element_type=F32) + b_ref[0]


def _ada(c, w_ada, b_ada):
    L, D, N = w_ada.shape
    B = c.shape[0]
    tn = 1024
    return pl.pallas_call(
        _ada_kernel,
        grid=(L, N // tn),
        in_specs=[pl.BlockSpec((B, D), lambda l, j: (0, 0)),
                  pl.BlockSpec((1, D, tn), lambda l, j: (l, 0, j)),
                  pl.BlockSpec((1, 1, tn), lambda l, j: (l, 0, j))],
        out_specs=pl.BlockSpec((1, B, tn), lambda l, j: (l, 0, j)),
        out_shape=jax.ShapeDtypeStruct((L, B, N), F32),
        compiler_params=_cparams(("parallel", "parallel")),
        name="ada_mod",
    )(c, w_ada, b_ada.reshape(L, 1, N))


def _rope_tab_kernel(pos_ref, inv_ref, sgn_ref, cos_ref, sin_ref):
    ang = pos_ref[...] * inv_ref[...]
    cos_ref[...] = jnp.cos(ang)
    sin_ref[...] = jnp.sin(ang) * sgn_ref[...]


def _rope_tables(posf, d):
    T = posf.shape[0]
    half = d // 2
    inv = ROPE_THETA ** (-jnp.arange(0, d, 2, dtype=F32) / d)
    reps = LANES // half
    inv_t = jnp.tile(inv, reps).reshape(1, LANES)
    sgn = jnp.tile(jnp.concatenate([-jnp.ones((half,), F32), jnp.ones((half,), F32)]), LANES // d).reshape(1, LANES)
    tm = min(T, 1024)
    return pl.pallas_call(
        _rope_tab_kernel,
        grid=(T // tm,),
        in_specs=[pl.BlockSpec((tm, 1), lambda i: (i, 0)),
                  pl.BlockSpec((1, LANES), lambda i: (0, 0)),
                  pl.BlockSpec((1, LANES), lambda i: (0, 0))],
        out_specs=[pl.BlockSpec((tm, LANES), lambda i: (i, 0))] * 2,
        out_shape=[jax.ShapeDtypeStruct((T, LANES), F32)] * 2,
        compiler_params=_cparams(("parallel",)),
        name=f"rope_tab{d}",
    )(posf, inv_t, sgn)


def _rope64(x, cos, sin_s, first_half):
    rot = jnp.where(first_half, pltpu.roll(x, 96, 1), pltpu.roll(x, 32, 1))
    return x * cos + rot * sin_s


def _rope128(x, cos, sin_s):
    return x * cos + pltpu.roll(x, 64, 1) * sin_s


def _nmm_kernel(*refs, modulate):
    if modulate:
        x_ref, g_ref, sc_ref, sh_ref, w_ref, o_ref, h_s = refs
    else:
        x_ref, g_ref, w_ref, o_ref, h_s = refs

    @pl.when(pl.program_id(1) == 0)
    def _():
        h = _rms(x_ref[...], g_ref[...])
        if modulate:
            h = h * (1.0 + sc_ref[0]) + sh_ref[0]
        h_s[...] = h.astype(BF16)

    o_ref[...] = jnp.dot(h_s[...], w_ref[...], preferred_element_type=F32).astype(o_ref.dtype)


def _nmm(x, col_blk, K, g, w, S, sc=None, sh=None, tm=1024, tn=512, out_dtype=F32):
    T = x.shape[0]
    N = w.shape[1]
    tm = min(tm, S)
    tn = min(tn, N)
    modulate = sc is not None
    in_specs = [pl.BlockSpec((tm, K), lambda i, j: (i, col_blk)),
                pl.BlockSpec((1, K), lambda i, j: (0, 0))]
    args = [x, g.reshape(1, K)]
    if modulate:
        B = sc.shape[0]
        in_specs += [pl.BlockSpec((1, 1, K), lambda i, j: (i * tm // S, 0, 0))] * 2
        args += [sc.reshape(B, 1, K), sh.reshape(B, 1, K)]
    in_specs.append(pl.BlockSpec((K, tn), lambda i, j: (0, j)))
    args.append(w)
    return pl.pallas_call(
        functools.partial(_nmm_kernel, modulate=modulate),
        grid=(T // tm, N // tn),
        in_specs=in_specs,
        out_specs=pl.BlockSpec((tm, tn), lambda i, j: (i, j)),
        out_shape=jax.ShapeDtypeStruct((T, N), out_dtype),
        scratch_shapes=[pltpu.VMEM((tm, K), BF16)],
        compiler_params=_cparams(("parallel", "arbitrary")),
        name="norm_matmul",
    )(*args)


def _mm_res_kernel(a_ref, w_ref, r_ref, gt_ref, o_ref):
    acc = jnp.dot(a_ref[...], w_ref[...], preferred_element_type=F32)
    o_ref[...] = r_ref[...] + gt_ref[0] * acc


def _mm_res(a, w, res, gate, S, tm=1024, tn=512):
    T, K = a.shape
    N = w.shape[1]
    B = gate.shape[0]
    tm = min(tm, S)
    return pl.pallas_call(
        _mm_res_kernel,
        grid=(T // tm, N // tn),
        in_specs=[pl.BlockSpec((tm, K), lambda i, j: (i, 0)),
                  pl.BlockSpec((K, tn), lambda i, j: (0, j)),
                  pl.BlockSpec((tm, tn), lambda i, j: (i, j)),
                  pl.BlockSpec((1, 1, tn), lambda i, j: (i * tm // S, 0, j))],
        out_specs=pl.BlockSpec((tm, tn), lambda i, j: (i, j)),
        out_shape=jax.ShapeDtypeStruct((T, N), F32),
        compiler_params=_cparams(("parallel", "parallel")),
        name="out_proj_residual",
    )(a, w, res, gate.reshape(B, 1, N))


def _prep_kernel(p_ref, q_ref, kv_ref, c64_ref, s64_ref, c128_ref, s128_ref,
                 qm_ref, km_ref, vm_ref, sbq_ref, sbk_ref, sbv_ref,
                 dq_ref, dk_ref, dv_ref, iq_ref, ika_ref, ikb_ref, iw_ref):
    c64, s64 = c64_ref[...], s64_ref[...]
    c128, s128 = c128_ref[...], s128_ref[...]
    lane = lax.broadcasted_iota(I32, c64.shape, 1)
    first = (lane % 64) < 32

    def sl(ref, off, width=LANES):
        return ref[:, off:off + width]

    kr = _rope64(sl(p_ref, C_KR), c64, s64, first).astype(BF16)
    for h in range(MLA_HEADS):
        qm_ref[:, 256 * h:256 * h + 128] = sl(q_ref, 256 * h).astype(BF16)
        qm_ref[:, 256 * h + 128:256 * h + 256] = _rope64(sl(q_ref, 256 * h + 128), c64, s64, first).astype(BF16)
        km_ref[:, 256 * h:256 * h + 128] = sl(kv_ref, 128 * h).astype(BF16)
        km_ref[:, 256 * h + 128:256 * h + 256] = kr
    vm_ref[...] = kv_ref[:, MLA_HEADS * 128:].astype(BF16)
    sbq_ref[...] = sl(p_ref, C_SBQ, 512).astype(BF16)
    sbk_ref[...] = sl(p_ref, C_SBK, 512).astype(BF16)
    sbv_ref[...] = sl(p_ref, C_SBV, 512).astype(BF16)
    for h in range(DSA_HEADS):
        dq_ref[:, 128 * h:128 * h + 128] = _rope128(sl(p_ref, C_DSQ + 128 * h), c128, s128).astype(BF16)
    dk_ref[...] = _rope128(sl(p_ref, C_DSK), c128, s128).astype(BF16)
    dv_ref[...] = sl(p_ref, C_DSV).astype(BF16)
    for j in range(IDX_HEADS // 2):
        iq_ref[:, 128 * j:128 * j + 128] = _rope64(sl(p_ref, C_IXQ + 128 * j), c64, s64, first).astype(BF16)
    ik = _rope64(sl(p_ref, C_IXK), c64, s64, first)
    ika_ref[...] = ik.astype(BF16)
    ikb_ref[...] = pltpu.roll(ik, 64, 1).astype(BF16)
    iw_ref[...] = sl(p_ref, C_IXW)


def _prep(proj, q_raw, kv_raw, tabs64, tabs128, tm=256):
    T = proj.shape[0]
    row = lambda w: pl.BlockSpec((tm, w), lambda i: (i, 0))
    widths = [2048, 2048, 1024, 512, 512, 512, 512, 128, 128, 1024, 128, 128]
    out_shape = [jax.ShapeDtypeStruct((T, w), BF16) for w in widths] + [jax.ShapeDtypeStruct((T, LANES), F32)]
    return pl.pallas_call(
        _prep_kernel,
        grid=(T // tm,),
        in_specs=[row(IN_PAD), row(2048), row(2048), row(LANES), row(LANES), row(LANES), row(LANES)],
        out_specs=[row(w) for w in widths] + [row(LANES)],
        out_shape=out_shape,
        compiler_params=_cparams(("parallel",)),
        name="mixer_prep",
    )(proj, q_raw, kv_raw, *tabs64, *tabs128)


def _head_out(o, g):
    return _rms(o, g).astype(BF16)


LOG2E = 1.4426950408889634


def _mla_kernel(q_ref, k_ref, vt_ref, g_ref, o_ref, m_s, l_s, acc_s, *, t, nq, scale):
    i = pl.program_id(2)
    hps = m_s.shape[0]
    c = scale * LOG2E
    m_s[...] = jnp.full(m_s.shape, NEG_INF, F32)
    l_s[...] = jnp.zeros(l_s.shape, F32)
    acc_s[...] = jnp.zeros(acc_s.shape, F32)

    def step(cidx, masked):
        k0 = pl.multiple_of(cidx * t, t)
        for hh in range(hps):
            s = _nt_dot(k_ref[pl.ds(k0, t), 256 * hh:256 * hh + 256], q_ref[:, 256 * hh:256 * hh + 256])
            if masked:
                key = lax.broadcasted_iota(I32, (t, t), 0)
                qry = lax.broadcasted_iota(I32, (t, t), 1)
                s = jnp.where(key <= qry, s, NEG_INF)
            m_prev = m_s[hh]
            m_new = jnp.maximum(m_prev, jnp.max(s, axis=0, keepdims=True))
            alpha = jnp.exp2((m_prev - m_new) * c)
            p = jnp.exp2((s - m_new) * c)
            l_s[hh] = alpha * l_s[hh] + jnp.sum(p, axis=0, keepdims=True)
            acc_s[hh] = alpha * acc_s[hh] + jnp.dot(vt_ref[128 * hh:128 * hh + 128, pl.ds(k0, t)], p.astype(BF16),
                                                    preferred_element_type=F32)
            m_s[hh] = m_new

    for n in range(nq):
        @pl.when(i == n)
        def _():
            for cidx in range(n):
                step(cidx, False)
            step(n, True)

    for hh in range(hps):
        o_ref[:, 128 * hh:128 * hh + 128] = _head_out((acc_s[hh] / l_s[hh]).T, g_ref[:, 128 * hh:128 * hh + 128])


def _mla(qm, km, vmt, g_out, B, S, t=512, hps=2):
    T = qm.shape[0]
    t = min(t, S)
    nq = S // t
    H = MLA_HEADS
    return pl.pallas_call(
        functools.partial(_mla_kernel, t=t, nq=nq, scale=(MLA_NOPE + MLA_ROPE) ** -0.5),
        grid=(B, H // hps, nq),
        in_specs=[pl.BlockSpec((t, 256 * hps), lambda b, h, i: (b * nq + i, h)),
                  pl.BlockSpec((S, 256 * hps), lambda b, h, i: (b, h)),
                  pl.BlockSpec((128 * hps, S), lambda b, h, i: (h, b)),
                  pl.BlockSpec((1, 128 * hps), lambda b, h, i: (0, h))],
        out_specs=pl.BlockSpec((t, 128 * hps), lambda b, h, i: (b * nq + i, h)),
        out_shape=jax.ShapeDtypeStruct((T, H * HEAD_DIM), BF16),
        scratch_shapes=[pltpu.VMEM((hps, 1, t), F32), pltpu.VMEM((hps, 1, t), F32), pltpu.VMEM((hps, 128, t), F32)],
        compiler_params=_cparams(("parallel", "parallel", "arbitrary")),
        name="mla_attention",
    )(qm, km, vmt, g_out)


def _sb_kernel(q_ref, k_ref, vt_ref, g_ref, tri_ref, o_ref, carry_s, acc_s, *, t, nq, scale):
    i = pl.program_id(2)
    hps = carry_s.shape[0]
    tri = tri_ref[...]
    carry_s[...] = jnp.zeros(carry_s.shape, F32)
    acc_s[...] = jnp.zeros(acc_s.shape, F32)

    def step(cidx, masked):
        k0 = pl.multiple_of(cidx * t, t)
        for hh in range(hps):
            z = _nt_dot(k_ref[pl.ds(k0, t), 128 * hh:128 * hh + 128], q_ref[:, 128 * hh:128 * hh + 128]) * scale
            sp = jnp.log(1.0 + jnp.exp(-jnp.abs(z)))
            log_beta = jnp.minimum(z, 0.0) - sp
            log_keep = jnp.minimum(-z, 0.0) - sp
            if masked:
                key = lax.broadcasted_iota(I32, (t, t), 0)
                qry = lax.broadcasted_iota(I32, (t, t), 1)
                strict = key < qry
                log_keep = jnp.where(strict, log_keep, 0.0)
            hi = log_keep.astype(BF16)
            lo = (log_keep - hi.astype(F32)).astype(BF16)
            suffix = jnp.dot(tri, hi, preferred_element_type=F32) + jnp.dot(tri, lo, preferred_element_type=F32)
            a = jnp.exp(log_beta + suffix + carry_s[hh])
            if masked:
                a = jnp.where(strict, a, 0.0)
            acc_s[hh] += jnp.dot(vt_ref[128 * hh:128 * hh + 128, pl.ds(k0, t)], a.astype(BF16),
                                 preferred_element_type=F32)
            carry_s[hh] += jnp.sum(log_keep, axis=0, keepdims=True)

    for n in range(nq):
        @pl.when(i == n)
        def _():
            step(n, True)
            for cidx in range(n - 1, -1, -1):
                step(cidx, False)

    for hh in range(hps):
        o_ref[:, 128 * hh:128 * hh + 128] = _head_out(acc_s[hh].T, g_ref[:, 128 * hh:128 * hh + 128])


def _sb(sbq, sbk, sbvt, g_out, B, S, t=512, hps=2):
    T = sbq.shape[0]
    t = min(t, S)
    nq = S // t
    H = SB_HEADS
    r = np.arange(t)
    tri = jnp.asarray((r[None, :] > r[:, None]).astype(np.float32), BF16)
    g0 = MLA_HEADS // hps
    return pl.pallas_call(
        functools.partial(_sb_kernel, t=t, nq=nq, scale=HEAD_DIM ** -0.5),
        grid=(B, H // hps, nq),
        in_specs=[pl.BlockSpec((t, 128 * hps), lambda b, h, i: (b * nq + i, h)),
                  pl.BlockSpec((S, 128 * hps), lambda b, h, i: (b, h)),
                  pl.BlockSpec((128 * hps, S), lambda b, h, i: (h, b)),
                  pl.BlockSpec((1, 128 * hps), lambda b, h, i: (0, g0 + h)),
                  pl.BlockSpec((t, t), lambda b, h, i: (0, 0))],
        out_specs=pl.BlockSpec((t, 128 * hps), lambda b, h, i: (b * nq + i, h)),
        out_shape=jax.ShapeDtypeStruct((T, H * HEAD_DIM), BF16),
        scratch_shapes=[pltpu.VMEM((hps, 1, t), F32), pltpu.VMEM((hps, 128, t), F32)],
        compiler_params=_cparams(("parallel", "parallel", "arbitrary")),
        name="stickbreak_attention",
    )(sbq, sbk, sbvt, g_out, tri)


def _dsa_kernel(iq_ref, ika_ref, ikb_ref, wt_ref, dq_ref, dk_ref, vt_ref, g_ref, o_ref,
                key_s, m_s, l_s, acc_s, *, tq, tk, n_sel, scale):
    i = pl.program_id(1)
    nch = (i * tq + tq + tk - 1) // tk
    qpos = i * tq + lax.broadcasted_iota(I32, (1, tq), 1)
    kiota = lax.broadcasted_iota(I32, (tk, 1), 0)
    wt = wt_ref[0]
    half = IDX_HEADS // 2

    def score_chunk(c, carry):
        k0 = pl.multiple_of(c * tk, tk)
        kk = jnp.concatenate([ika_ref[pl.ds(k0, tk), :], ikb_ref[pl.ds(k0, tk), :]], axis=0)
        score = jnp.zeros((tk, tq), F32)
        for j in range(half):
            r = jnp.maximum(_nt_dot(kk, iq_ref[:, 128 * j:128 * j + 128]), 0.0)
            score = score + r[:tk] * wt[2 * j:2 * j + 1, :] + r[tk:] * wt[2 * j + 1:2 * j + 2, :]
        score = score * IDX_SCALE
        score = jnp.where(k0 + kiota <= qpos, score, NEG_INF)
        bits = pltpu.bitcast(score, I32)
        key_s[pl.ds(k0, tk), :] = jnp.where(bits < 0, bits ^ jnp.int32(0x7FFFFFFF), bits)
        return carry

    lax.fori_loop(0, nch, score_chunk, 0)

    def count(pred):
        def body(c, acc):
            k0 = pl.multiple_of(c * tk, tk)
            hit = pred(key_s[pl.ds(k0, tk), :], k0 + kiota).astype(I32)
            return acc + jnp.sum(hit.reshape(tk // 8, 8, tq), axis=0)
        part = lax.fori_loop(0, nch, body, jnp.zeros((8, tq), I32))
        return jnp.sum(part, axis=0, keepdims=True)

    c0 = count(lambda k, idx: k >= 0)
    thr = jnp.where(c0 >= n_sel, jnp.int32(0), jnp.int32(INT_MIN))

    def vbit(b, thr):
        cand = thr + jnp.left_shift(jnp.int32(1), 30 - b)
        cnt = count(lambda k, idx: k >= cand)
        return jnp.where(cnt >= n_sel, cand, thr)

    thr = lax.fori_loop(0, 31, vbit, thr)
    n_ge = count(lambda k, idx: k >= thr)
    idx_bits = max(1, int(np.ceil(np.log2(key_s.shape[0]))))

    def tie_bound():
        need = n_sel - count(lambda k, idx: k > thr)

        def ibit(b, bound):
            cand = bound + jnp.left_shift(jnp.int32(1), idx_bits - 1 - b)
            cnt = count(lambda k, idx: (k == thr) & (idx < cand))
            return jnp.where(cnt < need, cand, bound)

        return lax.fori_loop(0, idx_bits, ibit, jnp.zeros((1, tq), I32))

    bound = lax.cond(jnp.max(n_ge) > n_sel, tie_bound, lambda: jnp.full((1, tq), 2 ** idx_bits, I32))

    qc = jnp.concatenate([dq_ref[:, 128 * h:128 * h + 128] for h in range(DSA_HEADS)], axis=0)
    sc2 = scale * LOG2E
    m_s[...] = jnp.full(m_s.shape, NEG_INF, F32)
    l_s[...] = jnp.zeros(l_s.shape, F32)
    acc_s[...] = jnp.zeros(acc_s.shape, F32)

    def attn_chunk(c, carry):
        k0 = pl.multiple_of(c * tk, tk)
        key = key_s[pl.ds(k0, tk), :]
        idx = k0 + kiota
        sel = ((key > thr) | ((key == thr) & (idx <= bound))) & (idx <= qpos)
        bias = jnp.where(sel, 0.0, NEG_INF)
        s = _nt_dot(dk_ref[pl.ds(k0, tk), :], qc) + jnp.concatenate([bias] * DSA_HEADS, axis=1)
        m_prev = m_s[...]
        m_new = jnp.maximum(m_prev, jnp.max(s, axis=0, keepdims=True))
        m_safe = jnp.where(m_new == NEG_INF, 0.0, m_new)
        alpha = jnp.exp2((m_prev - m_safe) * sc2)
        p = jnp.exp2((s - m_safe) * sc2)
        l_s[...] = alpha * l_s[...] + jnp.sum(p, axis=0, keepdims=True)
        acc_s[...] = alpha * acc_s[...] + jnp.dot(vt_ref[:, pl.ds(k0, tk)], p.astype(BF16),
                                                  preferred_element_type=F32)
        m_s[...] = m_new
        return carry

    lax.fori_loop(0, nch, attn_chunk, 0)
    ot = acc_s[...] / l_s[...]
    for h in range(DSA_HEADS):
        o_ref[:, 128 * h:128 * h + 128] = _head_out(ot[:, h * tq:(h + 1) * tq].T, g_ref[:, 128 * h:128 * h + 128])


def _dsa(iq, ika, ikb, wt, dq, dk, vt, g_out, B, S, tq=256, tk=256):
    T = iq.shape[0]
    tk = min(tk, S)
    nq = S // tq
    H = DSA_HEADS
    n_sel = min(TOPK_MAX, S // 4)
    g_c = g_out[:, (MLA_HEADS + SB_HEADS) * HEAD_DIM:]
    return pl.pallas_call(
        functools.partial(_dsa_kernel, tq=tq, tk=tk, n_sel=n_sel, scale=HEAD_DIM ** -0.5),
        grid=(B, nq),
        in_specs=[pl.BlockSpec((tq, IDX_HEADS * IDX_DIM), lambda b, i: (b * nq + i, 0)),
                  pl.BlockSpec((S, 128), lambda b, i: (b, 0)),
                  pl.BlockSpec((S, 128), lambda b, i: (b, 0)),
                  pl.BlockSpec((1, IDX_HEADS, tq), lambda b, i: (b, 0, i)),
                  pl.BlockSpec((tq, H * 128), lambda b, i: (b * nq + i, 0)),
                  pl.BlockSpec((S, 128), lambda b, i: (b, 0)),
                  pl.BlockSpec((128, S), lambda b, i: (0, b)),
                  pl.BlockSpec((1, H * 128), lambda b, i: (0, 0))],
        out_specs=pl.BlockSpec((tq, H * 128), lambda b, i: (b * nq + i, 0)),
        out_shape=jax.ShapeDtypeStruct((T, H * HEAD_DIM), BF16),
        scratch_shapes=[pltpu.VMEM((S, tq), I32), pltpu.VMEM((1, H * tq), F32), pltpu.VMEM((1, H * tq), F32),
                        pltpu.VMEM((128, H * tq), F32)],
        compiler_params=_cparams(("parallel", "arbitrary")),
        name="dsa_attention",
    )(iq, ika, ikb, wt, dq, dk, vt, g_c)


def _route_kernel(x_ref, g_ref, sc_ref, sh_ref, wr_ref, br_ref, tri_ref,
                  h_ref, hp_ref, eidx_ref, gate_ref, pos_ref, cnt_ref, run_s):
    @pl.when(pl.program_id(0) == 0)
    def _():
        run_s[...] = jnp.zeros(run_s.shape, F32)

    h = _rms(x_ref[...], g_ref[...]) * (1.0 + sc_ref[0]) + sh_ref[0]
    hb = h.astype(BF16)
    h_ref[...] = hb
    tm, D = h.shape
    bits = pltpu.bitcast(hb.astype(F32), jnp.uint32)
    hp_ref[...] = (bits[:, :D // 2] >> 16) | (bits[:, D // 2:] & jnp.uint32(0xFFFF0000))
    logits = _nt_dot(wr_ref[...], hb)
    scores = jax.nn.sigmoid(logits)
    biased = scores + br_ref[...]
    gsz = N_EXPERTS // N_GROUPS
    b3 = biased.reshape(N_GROUPS, gsz, tm)
    m1 = jnp.max(b3, axis=1, keepdims=True)
    n1 = jnp.sum((b3 == m1).astype(F32), axis=1, keepdims=True)
    m2 = jnp.max(jnp.where(b3 < m1, b3, NEG_INF), axis=1, keepdims=True)
    grp = (m1 + jnp.where(n1 >= 2.0, m1, m2)).reshape(N_GROUPS, tm)
    gi = lax.broadcasted_iota(I32, (N_GROUPS, 1), 0)
    grank = jnp.zeros((N_GROUPS, tm), F32)
    for g in range(N_GROUPS):
        rowv = grp[g:g + 1, :]
        grank = grank + jnp.where((rowv > grp) | ((rowv == grp) & (g < gi)), 1.0, 0.0)
    gmask = grank.reshape(N_GROUPS, 1, tm) < float(TOPK_GROUPS)
    masked = jnp.where(gmask, b3, NEG_INF).reshape(N_EXPERTS, tm)
    ei = lax.broadcasted_iota(I32, (N_EXPERTS, 1), 0)
    rank = jnp.zeros((N_EXPERTS, tm), F32)
    for e in range(N_EXPERTS):
        rowv = masked[e:e + 1, :]
        rank = rank + jnp.where((rowv > masked) | ((rowv == masked) & (e < ei)), 1.0, 0.0)
    sel = rank < float(TOP_K)
    selm = sel.astype(F32)
    gsum = jnp.sum(scores * selm, axis=0, keepdims=True)
    gate = scores * selm / gsum * ROUTED_SCALE
    within = jnp.dot(selm.astype(BF16), tri_ref[...], preferred_element_type=F32)
    posf = within + run_s[...]
    run_s[...] += jnp.sum(selm, axis=1, keepdims=True)
    cnt_ref[...] = jnp.broadcast_to(run_s[...], cnt_ref.shape)
    eif = ei.astype(F32)
    for k in range(TOP_K):
        onek = rank == float(k)
        eidx_ref[k:k + 1, :] = jnp.sum(jnp.where(onek, eif, 0.0), axis=0, keepdims=True).astype(I32)
        gate_ref[k:k + 1, :] = jnp.sum(jnp.where(onek, gate, 0.0), axis=0, keepdims=True)
        pos_ref[k:k + 1, :] = jnp.sum(jnp.where(onek, posf, 0.0), axis=0, keepdims=True).astype(I32)


def _route(x, g, sc, sh, w_router, b_router, S, tm=512):
    T, D = x.shape
    B = sc.shape[0]
    tm = min(tm, S)
    r = np.arange(tm)
    tri = jnp.asarray((r[:, None] < r[None, :]).astype(np.float32), BF16)
    full = lambda shape: pl.BlockSpec(shape, lambda i: (0,) * len(shape))
    return pl.pallas_call(
        _route_kernel,
        grid=(T // tm,),
        in_specs=[pl.BlockSpec((tm, D), lambda i: (i, 0)),
                  full((1, D)),
                  pl.BlockSpec((1, 1, D), lambda i: (i * tm // S, 0, 0)),
                  pl.BlockSpec((1, 1, D), lambda i: (i * tm // S, 0, 0)),
                  full((N_EXPERTS, D)),
                  full((N_EXPERTS, 1)),
                  full((tm, tm))],
        out_specs=[pl.BlockSpec((tm, D), lambda i: (i, 0)),
                   pl.BlockSpec((tm, D // 2), lambda i: (i, 0)),
                   pl.BlockSpec((TOP_K, tm), lambda i: (0, i)),
                   pl.BlockSpec((TOP_K, tm), lambda i: (0, i)),
                   pl.BlockSpec((TOP_K, tm), lambda i: (0, i)),
                   full((N_EXPERTS, LANES))],
        out_shape=[jax.ShapeDtypeStruct((T, D), BF16),
                   jax.ShapeDtypeStruct((T, D // 2), jnp.uint32),
                   jax.ShapeDtypeStruct((TOP_K, T), I32),
                   jax.ShapeDtypeStruct((TOP_K, T), F32),
                   jax.ShapeDtypeStruct((TOP_K, T), I32),
                   jax.ShapeDtypeStruct((N_EXPERTS, LANES), F32)],
        scratch_shapes=[pltpu.VMEM((N_EXPERTS, 1), F32)],
        compiler_params=_cparams(("arbitrary",)),
        name="ffn_norm_route",
    )(x, g.reshape(1, D), sc.reshape(B, 1, D), sh.reshape(B, 1, D),
      w_router.T.astype(BF16), b_router.reshape(N_EXPERTS, 1), tri)


def _moe_kernel(bexp_ref, bval_ref, slot_hbm, h_hbm, wg_ref, wu_ref, wd_ref, y_hbm,
                xbuf0, xbuf1, x_s, mid_s, ybuf0, ybuf1, wg_s, wu_s, wd_s, idx_s, isem, gsem, ssem, *, n_tok):
    b = pl.program_id(0)
    nb = pl.num_programs(0)
    rows, D = x_s.shape
    half = D // 2
    DE = mid_s.shape[1]
    YS = D // LANES
    NW = 256
    xbufs, ybufs = (xbuf0, xbuf1), (ybuf0, ybuf1)
    n_real = TOP_K * n_tok

    def ring(blk):
        return (blk & 3) * rows

    def idx_copy(blk):
        return pltpu.make_async_copy(slot_hbm.at[blk], idx_s.at[pl.ds(pl.multiple_of(ring(blk), rows), rows)],
                                     isem.at[blk & 3])

    def gather_start(base, r, p):
        tok = idx_s[base + r] & (n_tok - 1)
        pltpu.make_async_copy(h_hbm.at[pl.ds(pl.multiple_of(tok * 8, 8), 8), :],
                              xbufs[p].at[pl.ds(r * 8, 8), :], gsem.at[p]).start(priority=r % 2)

    def scatter_start(base, r, p):
        dst = idx_s[base + r]
        pltpu.make_async_copy(ybufs[p].at[pl.ds(r * YS, YS), :],
                              y_hbm.at[pl.ds(pl.multiple_of(dst * YS, YS), YS), :], ssem.at[p]).start(priority=r % 2)

    def wait_gather(p):
        pltpu.make_async_copy(xbufs[p], xbufs[p], gsem.at[p]).wait()

    def wait_scatter(p):
        pltpu.make_async_copy(ybufs[p], ybufs[p], ssem.at[p]).wait()

    def compute_pieces(p):
        def unpack():
            for s in range(8):
                w = xbufs[p][pl.ds(s, rows, stride=8), :]
                x_s[:, 128 * s:128 * s + 128] = pltpu.bitcast(w << 16, F32).astype(BF16)
                x_s[:, half + 128 * s:half + 128 * s + 128] = pltpu.bitcast(w & jnp.uint32(0xFFFF0000), F32).astype(BF16)

        def gate_up(j):
            def piece():
                x = x_s[...]
                gt = jnp.dot(x, wg_s[:, NW * j:NW * j + NW], preferred_element_type=F32)
                up = jnp.dot(x, wu_s[:, NW * j:NW * j + NW], preferred_element_type=F32)
                mid_s[:, NW * j:NW * j + NW] = (gt * jax.nn.sigmoid(gt) * up).astype(BF16)
            return piece

        def down(n):
            def piece():
                y = jnp.dot(mid_s[...], wd_s[:, NW * n:NW * n + NW], preferred_element_type=F32)
                for s in range(NW // LANES):
                    ybufs[p][pl.ds(n * (NW // LANES) + s, rows, stride=YS), :] = y[:, LANES * s:LANES * s + LANES]
            return piece
        return [unpack] + [gate_up(j) for j in range(DE // NW)], [down(n) for n in range(D // NW)]

    def interleave(pieces, starts):
        n_slots = max(len(pieces), 1)
        per = -(-len(starts) // n_slots)
        for n in range(n_slots):
            if pieces:
                pieces[n]()
            for st in starts[n * per:(n + 1) * per]:
                st()

    def run_block(p, compute):
        first, second = compute_pieces(p) if compute else ([], [])
        interleave(first, [functools.partial(gather_start, ring(b + 1), r, 1 - p) for r in range(rows)])
        wait_scatter(p)
        interleave(second, [functools.partial(scatter_start, ring(b - 1), r, 1 - p) for r in range(rows)])

    @pl.when(b == 0)
    def _():
        for r in range(rows):
            idx_s[3 * rows + r] = n_real + rows + r
        ybuf0[...] = jnp.zeros(ybuf0.shape, F32)
        ybuf1[...] = jnp.zeros(ybuf1.shape, F32)
        pltpu.make_async_copy(ybuf0, y_hbm.at[pl.ds(n_real * YS, rows * YS), :], ssem.at[0]).start()
        idx_copy(0).start()
        idx_copy(0).wait()
        for r in range(rows):
            gather_start(0, r, 0)
        idx_copy(1).start()

    @pl.when(b + 2 < nb)
    def _():
        idx_copy(b + 2).start()

    @pl.when(b + 1 < nb)
    def _():
        idx_copy(b + 1).wait()

    @pl.when((b == 0) | (bexp_ref[b] != bexp_ref[jnp.maximum(b - 1, 0)]))
    def _():
        wg_s[...] = wg_ref[0, 0].astype(BF16)
        wu_s[...] = wu_ref[0, 0].astype(BF16)
        wd_s[...] = wd_ref[0, 0].astype(BF16)

    active = bval_ref[b] > 0
    for p in range(2):
        @pl.when((b & 1) == p)
        def _():
            wait_gather(p)

            @pl.when(active)
            def _():
                run_block(p, True)

            @pl.when(jnp.logical_not(active))
            def _():
                run_block(p, False)

            @pl.when(b == nb - 1)
            def _():
                sbase = ring(b)
                for r in range(rows):
                    scatter_start(sbase, r, p)
                wait_scatter(p)
                wait_scatter(1 - p)
                wait_gather(1 - p)


def _moe(hp, slot_ids, blk_exp, blk_val, w_gate, w_up, w_down, l):
    T = hp.shape[0] // 8
    D = w_gate.shape[2]
    assert T & (T - 1) == 0 and D == 2 * 8 * LANES
    DE = w_gate.shape[3]
    nb = blk_exp.shape[0]
    wspec = lambda shape: pl.BlockSpec((1, 1) + shape, lambda b, bexp, bval: (l, bexp[b], 0, 0))
    grid_spec = pltpu.PrefetchScalarGridSpec(
        num_scalar_prefetch=2,
        grid=(nb,),
        in_specs=[pl.BlockSpec(memory_space=pl.ANY), pl.BlockSpec(memory_space=pl.ANY),
                  wspec((D, DE)), wspec((D, DE)), wspec((DE, D))],
        out_specs=pl.BlockSpec(memory_space=pl.ANY),
        scratch_shapes=[pltpu.VMEM((MOE_ROWS * 8, LANES), jnp.uint32), pltpu.VMEM((MOE_ROWS * 8, LANES), jnp.uint32),
                        pltpu.VMEM((MOE_ROWS, D), BF16), pltpu.VMEM((MOE_ROWS, DE), BF16),
                        pltpu.VMEM((MOE_ROWS * D // LANES, LANES), F32), pltpu.VMEM((MOE_ROWS * D // LANES, LANES), F32),
                        pltpu.VMEM((D, DE), BF16), pltpu.VMEM((D, DE), BF16), pltpu.VMEM((DE, D), BF16),
                        pltpu.SMEM((4 * MOE_ROWS,), I32),
                        pltpu.SemaphoreType.DMA((4,)), pltpu.SemaphoreType.DMA((2,)), pltpu.SemaphoreType.DMA((2,))],
    )
    return pl.pallas_call(
        functools.partial(_moe_kernel, n_tok=T),
        grid_spec=grid_spec,
        out_shape=jax.ShapeDtypeStruct(((TOP_K * T + 2 * MOE_ROWS) * D // LANES, LANES), F32),
        compiler_params=_cparams(("arbitrary",)),
        name="moe_experts",
    )(blk_exp, blk_val, slot_ids.reshape(nb, MOE_ROWS), hp, w_gate, w_up, w_down)


def _ffn_out_kernel(*refs, final):
    y_refs = refs[:TOP_K]
    gk_ref, h_ref, x_ref, gf_ref, wsg_ref, wsu_ref, wsd_ref, fg_ref, o_ref = refs[TOP_K:]
    tm, D = x_ref.shape
    ys = D // LANES
    gates = [gk_ref[k] for k in range(TOP_K)]
    chunks = []
    for s in range(ys):
        acc = y_refs[0][pl.ds(s, tm, stride=ys), :] * gates[0]
        for k in range(1, TOP_K):
            acc = acc + y_refs[k][pl.ds(s, tm, stride=ys), :] * gates[k]
        chunks.append(acc)
    routed = jnp.concatenate(chunks, axis=1)
    hb = h_ref[...]
    gt = jnp.dot(hb, wsg_ref[...], preferred_element_type=F32)
    up = jnp.dot(hb, wsu_ref[...], preferred_element_type=F32)
    mid = (gt * jax.nn.sigmoid(gt) * up).astype(BF16)
    shared = jnp.dot(mid, wsd_ref[...], preferred_element_type=F32)
    out = x_ref[...] + gf_ref[0] * (routed + shared)
    if final:
        out = _rms(out, fg_ref[...])
    o_ref[...] = out


def _ffn_out(y, gate3, h, x, g_f, wsg, wsu, wsd, final_g, final, S, tm=128):
    T, D = x.shape
    B = g_f.shape[0]
    DS = wsg.shape[1]
    tm = min(tm, S)
    nt = T // tm
    full = lambda shape: pl.BlockSpec(shape, lambda i: (0,) * len(shape))
    y_specs = [pl.BlockSpec((tm * D // LANES, LANES), functools.partial(lambda i, k: (k * nt + i, 0), k=k))
               for k in range(TOP_K)]
    return pl.pallas_call(
        functools.partial(_ffn_out_kernel, final=final),
        grid=(nt,),
        in_specs=y_specs + [
                  pl.BlockSpec((TOP_K, tm, 1), lambda i: (0, i, 0)),
                  pl.BlockSpec((tm, D), lambda i: (i, 0)),
                  pl.BlockSpec((tm, D), lambda i: (i, 0)),
                  pl.BlockSpec((1, 1, D), lambda i: (i * tm // S, 0, 0)),
                  full((D, DS)), full((D, DS)), full((DS, D)), full((1, D))],
        out_specs=pl.BlockSpec((tm, D), lambda i: (i, 0)),
        out_shape=jax.ShapeDtypeStruct((T, D), F32),
        compiler_params=_cparams(("parallel",)),
        name="ffn_combine",
    )(*([y] * TOP_K), gate3, h, x, g_f.reshape(B, 1, D), wsg, wsu, wsd, final_g.reshape(1, D))


def _layout_w_in(w):
    D = w.shape[0]
    sizes = (Q_LORA, KV_LORA, MLA_ROPE, 512, 512, 512, 512, HEAD_DIM, HEAD_DIM, IDX_HEADS * IDX_DIM, IDX_DIM, IDX_HEADS)
    offs = np.concatenate([[0], np.cumsum(sizes)])
    cq, ckv, kr, sbq, sbk, sbv, dsq, dsk, dsv, ixq, ixk, ixw = [w[:, offs[n]:offs[n + 1]] for n in range(12)]
    z = lambda n: jnp.zeros((D, n), w.dtype)
    out = jnp.concatenate([cq, sbq, sbk, sbv, dsq, ckv, dsk, dsv, ixq, kr, z(64), ixk, z(64), ixw, z(112), z(128)], axis=1)
    assert out.shape[1] == IN_PAD
    return out.astype(BF16)


def _layout_w_uq(w):
    K = w.shape[0]
    w3 = w.reshape(K, MLA_HEADS, MLA_NOPE + MLA_ROPE)
    w3 = jnp.concatenate([w3, jnp.zeros((K, MLA_HEADS, 64), w.dtype)], axis=2)
    return w3.reshape(K, MLA_HEADS * 256).astype(BF16)


def _layout_w_ukv(w):
    K = w.shape[0]
    w3 = w.reshape(K, MLA_HEADS, MLA_NOPE + HEAD_DIM)
    return jnp.concatenate([w3[:, :, :MLA_NOPE].reshape(K, -1), w3[:, :, MLA_NOPE:].reshape(K, -1)], axis=1).astype(BF16)


def _expert_blocks(eidx, pos, counts, T):
    M = T * TOP_K
    nb = M // MOE_ROWS + N_EXPERTS
    P = nb * MOE_ROWS
    padded = (counts + MOE_ROWS - 1) // MOE_ROWS * MOE_ROWS
    pend = jnp.cumsum(padded)
    pstart = pend - padded
    experts = jnp.arange(N_EXPERTS, dtype=I32)
    dest = jnp.sum(jnp.where(eidx[:, :, None] == experts, pstart, 0), axis=-1) + pos
    out_row = jnp.arange(TOP_K, dtype=I32)[:, None] * T + jnp.arange(T, dtype=I32)[None, :]
    pad_row = M + ((jnp.arange(P, dtype=I32) // MOE_ROWS) % 2) * MOE_ROWS + jnp.arange(P, dtype=I32) % MOE_ROWS
    slot_ids = pad_row.at[dest.reshape(-1)].set(out_row.reshape(-1), unique_indices=True)
    blk_start = jnp.arange(nb, dtype=I32) * MOE_ROWS
    blk_exp = jnp.minimum(jnp.sum((pend[None, :] <= blk_start[:, None]).astype(I32), axis=1), N_EXPERTS - 1)
    real_end = jnp.sum(jnp.where(blk_exp[:, None] == experts, pstart + counts, 0), axis=-1)
    blk_val = jnp.clip(real_end - blk_start, 0, MOE_ROWS).astype(I32)
    return slot_ids, blk_exp, blk_val


def _mixers(x, S, B, positions_tabs, mod, l, w):
    sh_m, sc_m, g_m = mod[0], mod[1], mod[2]
    proj = _nmm(x, 0, x.shape[1], w["norm_mix_g"][l], w["w_in"][l], S, sc=sc_m, sh=sh_m)
    q_raw = _nmm(proj, C_CQ // Q_LORA, Q_LORA, w["g_cq"][l], w["w_uq"][l], S)
    kv_raw = _nmm(proj, C_CKV // KV_LORA, KV_LORA, w["g_ckv"][l], w["w_ukv"][l], S)
    tabs64, tabs128 = positions_tabs
    (qm, km, vm, sbq, sbk, sbv, dq, dk, dv, iq, ika, ikb, iw) = _prep(proj, q_raw, kv_raw, tabs64, tabs128)
    g_out = w["g_out"][l].reshape(1, -1)
    o_a = _mla(qm, km, vm.T, g_out, B, S)
    o_b = _sb(sbq, sbk, sbv.T, g_out, B, S)
    wt = iw[:, :IDX_HEADS].reshape(B, S, IDX_HEADS).transpose(0, 2, 1)
    o_c = _dsa(iq, ika, ikb, wt, dq, dk, dv.T, g_out, B, S)
    o = jnp.concatenate([o_a, o_b, o_c], axis=1)
    return _mm_res(o, w["w_o"][l], x, g_m, S)


def _ffn(x, S, B, mod, l, w, final_g, final):
    T, D = x.shape
    sh_f, sc_f, g_f = mod[3], mod[4], mod[5]
    h, hp, eidx, gate, pos, cnt = _route(x, w["norm_ffn_g"][l], sc_f, sh_f, w["w_router"][l], w["b_router"][l], S)
    counts = cnt[:, 0].astype(I32)
    slot_ids, blk_exp, blk_val = _expert_blocks(eidx, pos, counts, T)
    y = _moe(hp.reshape(T * 8, LANES), slot_ids, blk_exp, blk_val, w["w_gate"], w["w_up"], w["w_down"], l)
    return _ffn_out(y, gate.reshape(TOP_K, T, 1), h, x, g_f, w["ws_gate"][l], w["ws_up"][l], w["ws_down"][l],
                    final_g, final, S)


def kernel(x, c, positions, norm_mix_g, norm_ffn_g, w_ada, b_ada, w_in, g_cq, g_ckv, w_uq, w_ukv, g_out, w_o,
           w_router, b_router, w_gate, w_up, w_down, ws_gate, ws_up, ws_down, final_g):
    B, S, D = x.shape
    L = w_in.shape[0]
    T = B * S
    w = dict(norm_mix_g=norm_mix_g, norm_ffn_g=norm_ffn_g, g_cq=g_cq, g_ckv=g_ckv, g_out=g_out,
             w_in=jax.vmap(_layout_w_in)(w_in), w_uq=jax.vmap(_layout_w_uq)(w_uq), w_ukv=jax.vmap(_layout_w_ukv)(w_ukv),
             w_o=w_o.astype(BF16), w_router=w_router, b_router=b_router, w_gate=w_gate, w_up=w_up, w_down=w_down,
             ws_gate=ws_gate.astype(BF16), ws_up=ws_up.astype(BF16), ws_down=ws_down.astype(BF16))
    mod_all = _ada(c, w_ada, b_ada).reshape(L, B, 6, D)
    posf = positions.astype(F32).reshape(T, 1)
    tabs = (_rope_tables(posf, 64), _rope_tables(posf, 128))
    xt = x.reshape(T, D)
    for l in range(L):
        mod = [mod_all[l, :, n, :] for n in range(6)]
        xt = _mixers(xt, S, B, tabs, mod, l, w)
        xt = _ffn(xt, S, B, mod, l, w, final_g, l == L - 1)
    return xt.reshape(B, S, D)
```

```python
import functools

import jax
import jax.numpy as jnp
import numpy as np
from jax import lax
from jax.experimental import pallas as pl
from jax.experimental.pallas import tpu as pltpu

F32 = jnp.float32
BF16 = jnp.bfloat16
I32 = jnp.int32

HEAD_DIM = 128
MLA_HEADS = 8
SB_HEADS = 4
DSA_HEADS = 4
N_MIX_HEADS = MLA_HEADS + SB_HEADS + DSA_HEADS
Q_LORA = 512
KV_LORA = 256
MLA_NOPE = 128
MLA_ROPE = 64
IDX_HEADS = 16
IDX_DIM = 64
IDX_SCALE = (IDX_HEADS * IDX_DIM) ** -0.5
TOPK_MAX = 256
ROPE_THETA = 10000.0
N_EXPERTS = 64
TOP_K = 8
N_GROUPS = 8
TOPK_GROUPS = 4
ROUTED_SCALE = 2.5
EPS = 1e-6

LANES = 128
VMEM_LIMIT = 56 * 1024 * 1024
NEG_INF = float("-inf")
INT_MIN = -(2 ** 31)

C_CQ, C_SBQ, C_SBK, C_SBV, C_DSQ = 0, 512, 1024, 1536, 2048
C_CKV, C_DSK, C_DSV, C_IXQ = 2560, 2816, 2944, 3072
C_KR, C_IXK, C_IXW, IN_PAD = 4096, 4224, 4352, 4608

MOE_ROWS = 256


def _cparams(sem):
    return pltpu.CompilerParams(dimension_semantics=sem, vmem_limit_bytes=VMEM_LIMIT)


def _nt_dot(a, b):
    return lax.dot_general(a, b, (((1,), (1,)), ((), ())), preferred_element_type=F32)


def _rms(x, g):
    return x * lax.rsqrt(jnp.mean(x * x, axis=-1, keepdims=True) + EPS) * g


def _ada_kernel(c_ref, w_ref, b_ref, o_ref):
    c = c_ref[...]
    a = (c * jax.nn.sigmoid(c)).astype(BF16)
    o_ref[0] = jnp.dot(a, w_ref[0].astype(BF16), preferred_element_type=F32) + b_ref[0]


def _ada(c, w_ada, b_ada):
    L, D, N = w_ada.shape
    B = c.shape[0]
    tn = 1024
    return pl.pallas_call(
        _ada_kernel,
        grid=(L, N // tn),
        in_specs=[pl.BlockSpec((B, D), lambda l, j: (0, 0)),
                  pl.BlockSpec((1, D, tn), lambda l, j: (l, 0, j)),
                  pl.BlockSpec((1, 1, tn), lambda l, j: (l, 0, j))],
        out_specs=pl.BlockSpec((1, B, tn), lambda l, j: (l, 0, j)),
        out_shape=jax.ShapeDtypeStruct((L, B, N), F32),
        compiler_params=_cparams(("parallel", "parallel")),
        name="ada_mod",
    )(c, w_ada, b_ada.reshape(L, 1, N))


def _rope_tab_kernel(pos_ref, inv_ref, sgn_ref, cos_ref, sin_ref):
    ang = pos_ref[...] * inv_ref[...]
    cos_ref[...] = jnp.cos(ang)
    sin_ref[...] = jnp.sin(ang) * sgn_ref[...]


def _rope_tables(posf, d):
    T = posf.shape[0]
    half = d // 2
    inv = ROPE_THETA ** (-jnp.arange(0, d, 2, dtype=F32) / d)
    reps = LANES // half
    inv_t = jnp.tile(inv, reps).reshape(1, LANES)
    sgn = jnp.tile(jnp.concatenate([-jnp.ones((half,), F32), jnp.ones((half,), F32)]), LANES // d).reshape(1, LANES)
    tm = min(T, 1024)
    return pl.pallas_call(
        _rope_tab_kernel,
        grid=(T // tm,),
        in_specs=[pl.BlockSpec((tm, 1), lambda i: (i, 0)),
                  pl.BlockSpec((1, LANES), lambda i: (0, 0)),
                  pl.BlockSpec((1, LANES), lambda i: (0, 0))],
        out_specs=[pl.BlockSpec((tm, LANES), lambda i: (i, 0))] * 2,
        out_shape=[jax.ShapeDtypeStruct((T, LANES), F32)] * 2,
        compiler_params=_cparams(("parallel",)),
        name=f"rope_tab{d}",
    )(posf, inv_t, sgn)


def _rope64(x, cos, sin_s, first_half):
    rot = jnp.where(first_half, pltpu.roll(x, 96, 1), pltpu.roll(x, 32, 1))
    return x * cos + rot * sin_s


def _rope128(x, cos, sin_s):
    return x * cos + pltpu.roll(x, 64, 1) * sin_s


def _nmm_kernel(*refs, modulate):
    if modulate:
        x_ref, g_ref, sc_ref, sh_ref, w_ref, o_ref, h_s = refs
    else:
        x_ref, g_ref, w_ref, o_ref, h_s = refs

    @pl.when(pl.program_id(1) == 0)
    def _():
        h = _rms(x_ref[...], g_ref[...])
        if modulate:
            h = h * (1.0 + sc_ref[0]) + sh_ref[0]
        h_s[...] = h.astype(BF16)

    o_ref[...] = jnp.dot(h_s[...], w_ref[...], preferred_element_type=F32).astype(o_ref.dtype)


def _nmm(x, col_blk, K, g, w, S, sc=None, sh=None, tm=1024, tn=512, out_dtype=F32):
    T = x.shape[0]
    N = w.shape[1]
    tm = min(tm, S)
    tn = min(tn, N)
    modulate = sc is not None
    in_specs = [pl.BlockSpec((tm, K), lambda i, j: (i, col_blk)),
                pl.BlockSpec((1, K), lambda i, j: (0, 0))]
    args = [x, g.reshape(1, K)]
    if modulate:
        B = sc.shape[0]
        in_specs += [pl.BlockSpec((1, 1, K), lambda i, j: (i * tm // S, 0, 0))] * 2
        args += [sc.reshape(B, 1, K), sh.reshape(B, 1, K)]
    in_specs.append(pl.BlockSpec((K, tn), lambda i, j: (0, j)))
    args.append(w)
    return pl.pallas_call(
        functools.partial(_nmm_kernel, modulate=modulate),
        grid=(T // tm, N // tn),
        in_specs=in_specs,
        out_specs=pl.BlockSpec((tm, tn), lambda i, j: (i, j)),
        out_shape=jax.ShapeDtypeStruct((T, N), out_dtype),
        scratch_shapes=[pltpu.VMEM((tm, K), BF16)],
        compiler_params=_cparams(("parallel", "arbitrary")),
        name="norm_matmul",
    )(*args)


def _mm_res_kernel(a_ref, w_ref, r_ref, gt_ref, o_ref):
    acc = jnp.dot(a_ref[...], w_ref[...], preferred_element_type=F32)
    o_ref[...] = r_ref[...] + gt_ref[0] * acc


def _mm_res(a, w, res, gate, S, tm=1024, tn=512):
    T, K = a.shape
    N = w.shape[1]
    B = gate.shape[0]
    tm = min(tm, S)
    return pl.pallas_call(
        _mm_res_kernel,
        grid=(T // tm, N // tn),
        in_specs=[pl.BlockSpec((tm, K), lambda i, j: (i, 0)),
                  pl.BlockSpec((K, tn), lambda i, j: (0, j)),
                  pl.BlockSpec((tm, tn), lambda i, j: (i, j)),
                  pl.BlockSpec((1, 1, tn), lambda i, j: (i * tm // S, 0, j))],
        out_specs=pl.BlockSpec((tm, tn), lambda i, j: (i, j)),
        out_shape=jax.ShapeDtypeStruct((T, N), F32),
        compiler_params=_cparams(("parallel", "parallel")),
        name="out_proj_residual",
    )(a, w, res, gate.reshape(B, 1, N))


def _prep_kernel(p_ref, q_ref, kv_ref, c64_ref, s64_ref, c128_ref, s128_ref,
                 qm_ref, km_ref, vm_ref, sbq_ref, sbk_ref, sbv_ref,
                 dq_ref, dk_ref, dv_ref, iq_ref, ika_ref, ikb_ref, iw_ref):
    c64, s64 = c64_ref[...], s64_ref[...]
    c128, s128 = c128_ref[...], s128_ref[...]
    lane = lax.broadcasted_iota(I32, c64.shape, 1)
    first = (lane % 64) < 32

    def sl(ref, off, width=LANES):
        return ref[:, off:off + width]

    kr = _rope64(sl(p_ref, C_KR), c64, s64, first).astype(BF16)
    for h in range(MLA_HEADS):
        qm_ref[:, 256 * h:256 * h + 128] = sl(q_ref, 256 * h).astype(BF16)
        qm_ref[:, 256 * h + 128:256 * h + 256] = _rope64(sl(q_ref, 256 * h + 128), c64, s64, first).astype(BF16)
        km_ref[:, 256 * h:256 * h + 128] = sl(kv_ref, 128 * h).astype(BF16)
        km_ref[:, 256 * h + 128:256 * h + 256] = kr
    vm_ref[...] = kv_ref[:, MLA_HEADS * 128:].astype(BF16)
    sbq_ref[...] = sl(p_ref, C_SBQ, 512).astype(BF16)
    sbk_ref[...] = sl(p_ref, C_SBK, 512).astype(BF16)
    sbv_ref[...] = sl(p_ref, C_SBV, 512).astype(BF16)
    for h in range(DSA_HEADS):
        dq_ref[:, 128 * h:128 * h + 128] = _rope128(sl(p_ref, C_DSQ + 128 * h), c128, s128).astype(BF16)
    dk_ref[...] = _rope128(sl(p_ref, C_DSK), c128, s128).astype(BF16)
    dv_ref[...] = sl(p_ref, C_DSV).astype(BF16)
    for j in range(IDX_HEADS // 2):
        iq_ref[:, 128 * j:128 * j + 128] = _rope64(sl(p_ref, C_IXQ + 128 * j), c64, s64, first).astype(BF16)
    ik = _rope64(sl(p_ref, C_IXK), c64, s64, first)
    ika_ref[...] = ik.astype(BF16)
    ikb_ref[...] = pltpu.roll(ik, 64, 1).astype(BF16)
    iw_ref[...] = sl(p_ref, C_IXW)


def _prep(proj, q_raw, kv_raw, tabs64, tabs128, tm=256):
    T = proj.shape[0]
    row = lambda w: pl.BlockSpec((tm, w), lambda i: (i, 0))
    widths = [2048, 2048, 1024, 512, 512, 512, 512, 128, 128, 1024, 128, 128]
    out_shape = [jax.ShapeDtypeStruct((T, w), BF16) for w in widths] + [jax.ShapeDtypeStruct((T, LANES), F32)]
    return pl.pallas_call(
        _prep_kernel,
        grid=(T // tm,),
        in_specs=[row(IN_PAD), row(2048), row(2048), row(LANES), row(LANES), row(LANES), row(LANES)],
        out_specs=[row(w) for w in widths] + [row(LANES)],
        out_shape=out_shape,
        compiler_params=_cparams(("parallel",)),
        name="mixer_prep",
    )(proj, q_raw, kv_raw, *tabs64, *tabs128)


def _head_out(o, g):
    return _rms(o, g).astype(BF16)


LOG2E = 1.4426950408889634


def _mla_kernel(q_ref, k_ref, vt_ref, g_ref, o_ref, m_s, l_s, acc_s, *, t, nq, scale):
    i = pl.program_id(2)
    hps = m_s.shape[0]
    c = scale * LOG2E
    m_s[...] = jnp.full(m_s.shape, NEG_INF, F32)
    l_s[...] = jnp.zeros(l_s.shape, F32)
    acc_s[...] = jnp.zeros(acc_s.shape, F32)

    def step(cidx, masked):
        k0 = pl.multiple_of(cidx * t, t)
        for hh in range(hps):
            s = _nt_dot(k_ref[pl.ds(k0, t), 256 * hh:256 * hh + 256], q_ref[:, 256 * hh:256 * hh + 256])
            if masked:
                key = lax.broadcasted_iota(I32, (t, t), 0)
                qry = lax.broadcasted_iota(I32, (t, t), 1)
                s = jnp.where(key <= qry, s, NEG_INF)
            m_prev = m_s[hh]
            m_new = jnp.maximum(m_prev, jnp.max(s, axis=0, keepdims=True))
            alpha = jnp.exp2((m_prev - m_new) * c)
            p = jnp.exp2((s - m_new) * c)
            l_s[hh] = alpha * l_s[hh] + jnp.sum(p, axis=0, keepdims=True)
            acc_s[hh] = alpha * acc_s[hh] + jnp.dot(vt_ref[128 * hh:128 * hh + 128, pl.ds(k0, t)], p.astype(BF16),
                                                    preferred_element_type=F32)
            m_s[hh] = m_new

    for n in range(nq):
        @pl.when(i == n)
        def _():
            for cidx in range(n):
                step(cidx, False)
            step(n, True)

    for hh in range(hps):
        o_ref[:, 128 * hh:128 * hh + 128] = _head_out((acc_s[hh] / l_s[hh]).T, g_ref[:, 128 * hh:128 * hh + 128])


def _mla(qm, km, vmt, g_out, B, S, t=1024, hps=2):
    T = qm.shape[0]
    t = min(t, S)
    nq = S // t
    H = MLA_HEADS
    return pl.pallas_call(
        functools.partial(_mla_kernel, t=t, nq=nq, scale=(MLA_NOPE + MLA_ROPE) ** -0.5),
        grid=(B, H // hps, nq),
        in_specs=[pl.BlockSpec((t, 256 * hps), lambda b, h, i: (b * nq + i, h)),
                  pl.BlockSpec((S, 256 * hps), lambda b, h, i: (b, h)),
                  pl.BlockSpec((128 * hps, S), lambda b, h, i: (h, b)),
                  pl.BlockSpec((1, 128 * hps), lambda b, h, i: (0, h))],
        out_specs=pl.BlockSpec((t, 128 * hps), lambda b, h, i: (b * nq + i, h)),
        out_shape=jax.ShapeDtypeStruct((T, H * HEAD_DIM), BF16),
        scratch_shapes=[pltpu.VMEM((hps, 1, t), F32), pltpu.VMEM((hps, 1, t), F32), pltpu.VMEM((hps, 128, t), F32)],
        compiler_params=_cparams(("parallel", "parallel", "arbitrary")),
        name="mla_attention",
    )(qm, km, vmt, g_out)


def _sb_kernel(q_ref, k_ref, vt_ref, g_ref, tri_ref, o_ref, carry_s, acc_s, *, t, nq, scale):
    i = pl.program_id(2)
    hps = carry_s.shape[0]
    tri = tri_ref[...]
    carry_s[...] = jnp.zeros(carry_s.shape, F32)
    acc_s[...] = jnp.zeros(acc_s.shape, F32)

    def step(cidx, masked):
        k0 = pl.multiple_of(cidx * t, t)
        for hh in range(hps):
            z = _nt_dot(k_ref[pl.ds(k0, t), 128 * hh:128 * hh + 128], q_ref[:, 128 * hh:128 * hh + 128]) * scale
            sp = jnp.log(1.0 + jnp.exp(-jnp.abs(z)))
            log_beta = jnp.minimum(z, 0.0) - sp
            log_keep = jnp.minimum(-z, 0.0) - sp
            if masked:
                key = lax.broadcasted_iota(I32, (t, t), 0)
                qry = lax.broadcasted_iota(I32, (t, t), 1)
                strict = key < qry
                log_keep = jnp.where(strict, log_keep, 0.0)
            hi = log_keep.astype(BF16)
            lo = (log_keep - hi.astype(F32)).astype(BF16)
            suffix = jnp.dot(tri, hi, preferred_element_type=F32) + jnp.dot(tri, lo, preferred_element_type=F32)
            a = jnp.exp(log_beta + suffix + carry_s[hh])
            if masked:
                a = jnp.where(strict, a, 0.0)
            acc_s[hh] += jnp.dot(vt_ref[128 * hh:128 * hh + 128, pl.ds(k0, t)], a.astype(BF16),
                                 preferred_element_type=F32)
            carry_s[hh] += jnp.sum(log_keep, axis=0, keepdims=True)

    for n in range(nq):
        @pl.when(i == n)
        def _():
            step(n, True)
            for cidx in range(n - 1, -1, -1):
                step(cidx, False)

    for hh in range(hps):
        o_ref[:, 128 * hh:128 * hh + 128] = _head_out(acc_s[hh].T, g_ref[:, 128 * hh:128 * hh + 128])


def _sb(sbq, sbk, sbvt, g_out, B, S, t=512, hps=2):
    T = sbq.shape[0]
    t = min(t, S)
    nq = S // t
    H = SB_HEADS
    r = np.arange(t)
    tri = jnp.asarray((r[None, :] > r[:, None]).astype(np.float32), BF16)
    g0 = MLA_HEADS // hps
    return pl.pallas_call(
        functools.partial(_sb_kernel, t=t, nq=nq, scale=HEAD_DIM ** -0.5),
        grid=(B, H // hps, nq),
        in_specs=[pl.BlockSpec((t, 128 * hps), lambda b, h, i: (b * nq + i, h)),
                  pl.BlockSpec((S, 128 * hps), lambda b, h, i: (b, h)),
                  pl.BlockSpec((128 * hps, S), lambda b, h, i: (h, b)),
                  pl.BlockSpec((1, 128 * hps), lambda b, h, i: (0, g0 + h)),
                  pl.BlockSpec((t, t), lambda b, h, i: (0, 0))],
        out_specs=pl.BlockSpec((t, 128 * hps), lambda b, h, i: (b * nq + i, h)),
        out_shape=jax.ShapeDtypeStruct((T, H * HEAD_DIM), BF16),
        scratch_shapes=[pltpu.VMEM((hps, 1, t), F32), pltpu.VMEM((hps, 128, t), F32)],
        compiler_params=_cparams(("parallel", "parallel", "arbitrary")),
        name="stickbreak_attention",
    )(sbq, sbk, sbvt, g_out, tri)


def _dsa_kernel(iq_ref, ika_ref, ikb_ref, wt_ref, dq_ref, dk_ref, vt_ref, g_ref, o_ref,
                key_s, m_s, l_s, acc_s, *, tq, tk, n_sel, scale):
    i = pl.program_id(1)
    nch = (i * tq + tq + tk - 1) // tk
    qpos = i * tq + lax.broadcasted_iota(I32, (1, tq), 1)
    kiota = lax.broadcasted_iota(I32, (tk, 1), 0)
    wt = wt_ref[0]
    half = IDX_HEADS // 2

    def score_chunk(c, carry):
        k0 = pl.multiple_of(c * tk, tk)
        kk = jnp.concatenate([ika_ref[pl.ds(k0, tk), :], ikb_ref[pl.ds(k0, tk), :]], axis=0)
        score = jnp.zeros((tk, tq), F32)
        for j in range(half):
            r = jnp.maximum(_nt_dot(kk, iq_ref[:, 128 * j:128 * j + 128]), 0.0)
            score = score + r[:tk] * wt[2 * j:2 * j + 1, :] + r[tk:] * wt[2 * j + 1:2 * j + 2, :]
        score = score * IDX_SCALE
        score = jnp.where(k0 + kiota <= qpos, score, NEG_INF)
        bits = pltpu.bitcast(score, I32)
        key_s[pl.ds(k0, tk), :] = jnp.where(bits < 0, bits ^ jnp.int32(0x7FFFFFFF), bits)
        return carry

    lax.fori_loop(0, nch, score_chunk, 0)

    def count(pred):
        def body(c, acc):
            k0 = pl.multiple_of(c * tk, tk)
            hit = pred(key_s[pl.ds(k0, tk), :], k0 + kiota).astype(I32)
            return acc + jnp.sum(hit.reshape(tk // 8, 8, tq), axis=0)
        part = lax.fori_loop(0, nch, body, jnp.zeros((8, tq), I32))
        return jnp.sum(part, axis=0, keepdims=True)

    c0 = count(lambda k, idx: k >= 0)
    thr = jnp.where(c0 >= n_sel, jnp.int32(0), jnp.int32(INT_MIN))

    def vbit(b, thr):
        cand = thr + jnp.left_shift(jnp.int32(1), 30 - b)
        cnt = count(lambda k, idx: k >= cand)
        return jnp.where(cnt >= n_sel, cand, thr)

    thr = lax.fori_loop(0, 31, vbit, thr)
    n_ge = count(lambda k, idx: k >= thr)
    idx_bits = max(1, int(np.ceil(np.log2(key_s.shape[0]))))

    def tie_bound():
        need = n_sel - count(lambda k, idx: k > thr)

        def ibit(b, bound):
            cand = bound + jnp.left_shift(jnp.int32(1), idx_bits - 1 - b)
            cnt = count(lambda k, idx: (k == thr) & (idx < cand))
            return jnp.where(cnt < need, cand, bound)

        return lax.fori_loop(0, idx_bits, ibit, jnp.zeros((1, tq), I32))

    bound = lax.cond(jnp.max(n_ge) > n_sel, tie_bound, lambda: jnp.full((1, tq), 2 ** idx_bits, I32))

    qc = jnp.concatenate([dq_ref[:, 128 * h:128 * h + 128] for h in range(DSA_HEADS)], axis=0)
    sc2 = scale * LOG2E
    m_s[...] = jnp.full(m_s.shape, NEG_INF, F32)
    l_s[...] = jnp.zeros(l_s.shape, F32)
    acc_s[...] = jnp.zeros(acc_s.shape, F32)

    def attn_chunk(c, carry):
        k0 = pl.multiple_of(c * tk, tk)
        key = key_s[pl.ds(k0, tk), :]
        idx = k0 + kiota
        sel = ((key > thr) | ((key == thr) & (idx <= bound))) & (idx <= qpos)
        bias = jnp.where(sel, 0.0, NEG_INF)
        s = _nt_dot(dk_ref[pl.ds(k0, tk), :], qc) + jnp.concatenate([bias] * DSA_HEADS, axis=1)
        m_prev = m_s[...]
        m_new = jnp.maximum(m_prev, jnp.max(s, axis=0, keepdims=True))
        m_safe = jnp.where(m_new == NEG_INF, 0.0, m_new)
        alpha = jnp.exp2((m_prev - m_safe) * sc2)
        p = jnp.exp2((s - m_safe) * sc2)
        l_s[...] = alpha * l_s[...] + jnp.sum(p, axis=0, keepdims=True)
        acc_s[...] = alpha * acc_s[...] + jnp.dot(vt_ref[:, pl.ds(k0, tk)], p.astype(BF16),
                                                  preferred_element_type=F32)
        m_s[...] = m_new
        return carry

    lax.fori_loop(0, nch, attn_chunk, 0)
    ot = acc_s[...] / l_s[...]
    for h in range(DSA_HEADS):
        o_ref[:, 128 * h:128 * h + 128] = _head_out(ot[:, h * tq:(h + 1) * tq].T, g_ref[:, 128 * h:128 * h + 128])


def _dsa(iq, ika, ikb, wt, dq, dk, vt, g_out, B, S, tq=256, tk=256):
    T = iq.shape[0]
    tk = min(tk, S)
    nq = S // tq
    H = DSA_HEADS
    n_sel = min(TOPK_MAX, S // 4)
    g_c = g_out[:, (MLA_HEADS + SB_HEADS) * HEAD_DIM:]
    return pl.pallas_call(
        functools.partial(_dsa_kernel, tq=tq, tk=tk, n_sel=n_sel, scale=HEAD_DIM ** -0.5),
        grid=(B, nq),
        in_specs=[pl.BlockSpec((tq, IDX_HEADS * IDX_DIM), lambda b, i: (b * nq + i, 0)),
                  pl.BlockSpec((S, 128), lambda b, i: (b, 0)),
                  pl.BlockSpec((S, 128), lambda b, i: (b, 0)),
                  pl.BlockSpec((1, IDX_HEADS, tq), lambda b, i: (b, 0, i)),
                  pl.BlockSpec((tq, H * 128), lambda b, i: (b * nq + i, 0)),
                  pl.BlockSpec((S, 128), lambda b, i: (b, 0)),
                  pl.BlockSpec((128, S), lambda b, i: (0, b)),
                  pl.BlockSpec((1, H * 128), lambda b, i: (0, 0))],
        out_specs=pl.BlockSpec((tq, H * 128), lambda b, i: (b * nq + i, 0)),
        out_shape=jax.ShapeDtypeStruct((T, H * HEAD_DIM), BF16),
        scratch_shapes=[pltpu.VMEM((S, tq), I32), pltpu.VMEM((1, H * tq), F32), pltpu.VMEM((1, H * tq), F32),
                        pltpu.VMEM((128, H * tq), F32)],
        compiler_params=_cparams(("parallel", "arbitrary")),
        name="dsa_attention",
    )(iq, ika, ikb, wt, dq, dk, vt, g_c)


def _route_kernel(x_ref, g_ref, sc_ref, sh_ref, wr_ref, br_ref, tri_ref,
                  h_ref, hp_ref, eidx_ref, gate_ref, pos_ref, cnt_ref, run_s):
    @pl.when(pl.program_id(0) == 0)
    def _():
        run_s[...] = jnp.zeros(run_s.shape, F32)

    h = _rms(x_ref[...], g_ref[...]) * (1.0 + sc_ref[0]) + sh_ref[0]
    hb = h.astype(BF16)
    h_ref[...] = hb
    tm, D = h.shape
    bits = pltpu.bitcast(hb.astype(F32), jnp.uint32)
    hp_ref[...] = (bits[:, :D // 2] >> 16) | (bits[:, D // 2:] & jnp.uint32(0xFFFF0000))
    logits = _nt_dot(wr_ref[...], hb)
    scores = jax.nn.sigmoid(logits)
    biased = scores + br_ref[...]
    gsz = N_EXPERTS // N_GROUPS
    b3 = biased.reshape(N_GROUPS, gsz, tm)
    m1 = jnp.max(b3, axis=1, keepdims=True)
    n1 = jnp.sum((b3 == m1).astype(F32), axis=1, keepdims=True)
    m2 = jnp.max(jnp.where(b3 < m1, b3, NEG_INF), axis=1, keepdims=True)
    grp = (m1 + jnp.where(n1 >= 2.0, m1, m2)).reshape(N_GROUPS, tm)
    gi = lax.broadcasted_iota(I32, (N_GROUPS, 1), 0)
    grank = jnp.zeros((N_GROUPS, tm), F32)
    for g in range(N_GROUPS):
        rowv = grp[g:g + 1, :]
        grank = grank + jnp.where((rowv > grp) | ((rowv == grp) & (g < gi)), 1.0, 0.0)
    gmask = grank.reshape(N_GROUPS, 1, tm) < float(TOPK_GROUPS)
    masked = jnp.where(gmask, b3, NEG_INF).reshape(N_EXPERTS, tm)
    ei = lax.broadcasted_iota(I32, (N_EXPERTS, 1), 0)
    rank = jnp.zeros((N_EXPERTS, tm), F32)
    for e in range(N_EXPERTS):
        rowv = masked[e:e + 1, :]
        rank = rank + jnp.where((rowv > masked) | ((rowv == masked) & (e < ei)), 1.0, 0.0)
    sel = rank < float(TOP_K)
    selm = sel.astype(F32)
    gsum = jnp.sum(scores * selm, axis=0, keepdims=True)
    gate = scores * selm / gsum * ROUTED_SCALE
    within = jnp.dot(selm.astype(BF16), tri_ref[...], preferred_element_type=F32)
    posf = within + run_s[...]
    run_s[...] += jnp.sum(selm, axis=1, keepdims=True)
    cnt_ref[...] = jnp.broadcast_to(run_s[...], cnt_ref.shape)
    eif = ei.astype(F32)
    for k in range(TOP_K):
        onek = rank == float(k)
        eidx_ref[k:k + 1, :] = jnp.sum(jnp.where(onek, eif, 0.0), axis=0, keepdims=True).astype(I32)
        gate_ref[k:k + 1, :] = jnp.sum(jnp.where(onek, gate, 0.0), axis=0, keepdims=True)
        pos_ref[k:k + 1, :] = jnp.sum(jnp.where(onek, posf, 0.0), axis=0, keepdims=True).astype(I32)


def _route(x, g, sc, sh, w_router, b_router, S, tm=512):
    T, D = x.shape
    B = sc.shape[0]
    tm = min(tm, S)
    r = np.arange(tm)
    tri = jnp.asarray((r[:, None] < r[None, :]).astype(np.float32), BF16)
    full = lambda shape: pl.BlockSpec(shape, lambda i: (0,) * len(shape))
    return pl.pallas_call(
        _route_kernel,
        grid=(T // tm,),
        in_specs=[pl.BlockSpec((tm, D), lambda i: (i, 0)),
                  full((1, D)),
                  pl.BlockSpec((1, 1, D), lambda i: (i * tm // S, 0, 0)),
                  pl.BlockSpec((1, 1, D), lambda i: (i * tm // S, 0, 0)),
                  full((N_EXPERTS, D)),
                  full((N_EXPERTS, 1)),
                  full((tm, tm))],
        out_specs=[pl.BlockSpec((tm, D), lambda i: (i, 0)),
                   pl.BlockSpec((tm, D // 2), lambda i: (i, 0)),
                   pl.BlockSpec((TOP_K, tm), lambda i: (0, i)),
                   pl.BlockSpec((TOP_K, tm), lambda i: (0, i)),
                   pl.BlockSpec((TOP_K, tm), lambda i: (0, i)),
                   full((N_EXPERTS, LANES))],
        out_shape=[jax.ShapeDtypeStruct((T, D), BF16),
                   jax.ShapeDtypeStruct((T, D // 2), jnp.uint32),
                   jax.ShapeDtypeStruct((TOP_K, T), I32),
                   jax.ShapeDtypeStruct((TOP_K, T), F32),
                   jax.ShapeDtypeStruct((TOP_K, T), I32),
                   jax.ShapeDtypeStruct((N_EXPERTS, LANES), F32)],
        scratch_shapes=[pltpu.VMEM((N_EXPERTS, 1), F32)],
        compiler_params=_cparams(("arbitrary",)),
        name="ffn_norm_route",
    )(x, g.reshape(1, D), sc.reshape(B, 1, D), sh.reshape(B, 1, D),
      w_router.T.astype(BF16), b_router.reshape(N_EXPERTS, 1), tri)


def _moe_kernel(bexp_ref, bval_ref, slot_hbm, h_hbm, wg_ref, wu_ref, wd_ref, y_hbm,
                xbuf0, xbuf1, x_s, mid_s, ybuf0, ybuf1, wg_s, wu_s, wd_s, idx_s, isem, gsem, ssem, *, n_tok):
    b = pl.program_id(0)
    nb = pl.num_programs(0)
    rows, D = x_s.shape
    half = D // 2
    DE = mid_s.shape[1]
    YS = D // LANES
    NW = 256
    xbufs, ybufs = (xbuf0, xbuf1), (ybuf0, ybuf1)
    n_real = TOP_K * n_tok

    def ring(blk):
        return (blk & 3) * rows

    def idx_copy(blk):
        return pltpu.make_async_copy(slot_hbm.at[blk], idx_s.at[pl.ds(pl.multiple_of(ring(blk), rows), rows)],
                                     isem.at[blk & 3])

    def gather_start(base, r, p):
        tok = idx_s[base + r] & (n_tok - 1)
        pltpu.make_async_copy(h_hbm.at[pl.ds(pl.multiple_of(tok * 8, 8), 8), :],
                              xbufs[p].at[pl.ds(r * 8, 8), :], gsem.at[p]).start(priority=r % 2)

    def scatter_start(base, r, p):
        dst = idx_s[base + r]
        pltpu.make_async_copy(ybufs[p].at[pl.ds(r * YS, YS), :],
                              y_hbm.at[pl.ds(pl.multiple_of(dst * YS, YS), YS), :], ssem.at[p]).start(priority=r % 2)

    def wait_gather(p):
        pltpu.make_async_copy(xbufs[p], xbufs[p], gsem.at[p]).wait()

    def wait_scatter(p):
        pltpu.make_async_copy(ybufs[p], ybufs[p], ssem.at[p]).wait()

    def compute_pieces(p):
        def unpack():
            for s in range(8):
                w = xbufs[p][pl.ds(s, rows, stride=8), :]
                x_s[:, 128 * s:128 * s + 128] = pltpu.bitcast(w << 16, F32).astype(BF16)
                x_s[:, half + 128 * s:half + 128 * s + 128] = pltpu.bitcast(w & jnp.uint32(0xFFFF0000), F32).astype(BF16)

        def gate_up(j):
            def piece():
                x = x_s[...]
                gt = jnp.dot(x, wg_s[:, NW * j:NW * j + NW], preferred_element_type=F32)
                up = jnp.dot(x, wu_s[:, NW * j:NW * j + NW], preferred_element_type=F32)
                mid_s[:, NW * j:NW * j + NW] = (gt * jax.nn.sigmoid(gt) * up).astype(BF16)
            return piece

        def down(n):
            def piece():
                y = jnp.dot(mid_s[...], wd_s[:, NW * n:NW * n + NW], preferred_element_type=F32)
                for s in range(NW // LANES):
                    ybufs[p][pl.ds(n * (NW // LANES) + s, rows, stride=YS), :] = y[:, LANES * s:LANES * s + LANES]
            return piece
        return [unpack] + [gate_up(j) for j in range(DE // NW)], [down(n) for n in range(D // NW)]

    def interleave(pieces, starts):
        n_slots = max(len(pieces), 1)
        per = -(-len(starts) // n_slots)
        for n in range(n_slots):
            if pieces:
                pieces[n]()
            for st in starts[n * per:(n + 1) * per]:
                st()

    def run_block(p, compute):
        first, second = compute_pieces(p) if compute else ([], [])
        interleave(first, [functools.partial(gather_start, ring(b + 1), r, 1 - p) for r in range(rows)])
        wait_scatter(p)
        interleave(second, [functools.partial(scatter_start, ring(b - 1), r, 1 - p) for r in range(rows)])

    @pl.when(b == 0)
    def _():
        for r in range(rows):
            idx_s[3 * rows + r] = n_real + rows + r
        ybuf0[...] = jnp.zeros(ybuf0.shape, F32)
        ybuf1[...] = jnp.zeros(ybuf1.shape, F32)
        pltpu.make_async_copy(ybuf0, y_hbm.at[pl.ds(n_real * YS, rows * YS), :], ssem.at[0]).start()
        idx_copy(0).start()
        idx_copy(0).wait()
        for r in range(rows):
            gather_start(0, r, 0)
        idx_copy(1).start()

    @pl.when(b + 2 < nb)
    def _():
        idx_copy(b + 2).start()

    @pl.when(b + 1 < nb)
    def _():
        idx_copy(b + 1).wait()

    @pl.when((b == 0) | (bexp_ref[b] != bexp_ref[jnp.maximum(b - 1, 0)]))
    def _():
        wg_s[...] = wg_ref[0, 0].astype(BF16)
        wu_s[...] = wu_ref[0, 0].astype(BF16)
        wd_s[...] = wd_ref[0, 0].astype(BF16)

    active = bval_ref[b] > 0
    for p in range(2):
        @pl.when((b & 1) == p)
        def _():
            wait_gather(p)

            @pl.when(active)
            def _():
                run_block(p, True)

            @pl.when(jnp.logical_not(active))
            def _():
                run_block(p, False)

            @pl.when(b == nb - 1)
            def _():
                sbase = ring(b)
                for r in range(rows):
                    scatter_start(sbase, r, p)
                wait_scatter(p)
                wait_scatter(1 - p)
                wait_gather(1 - p)


def _moe(hp, slot_ids, blk_exp, blk_val, w_gate, w_up, w_down, l):
    T = hp.shape[0] // 8
    D = w_gate.shape[2]
    assert T & (T - 1) == 0 and D == 2 * 8 * LANES
    DE = w_gate.shape[3]
    nb = blk_exp.shape[0]
    wspec = lambda shape: pl.BlockSpec((1, 1) + shape, lambda b, bexp, bval: (l, bexp[b], 0, 0))
    grid_spec = pltpu.PrefetchScalarGridSpec(
        num_scalar_prefetch=2,
        grid=(nb,),
        in_specs=[pl.BlockSpec(memory_space=pl.ANY), pl.BlockSpec(memory_space=pl.ANY),
                  wspec((D, DE)), wspec((D, DE)), wspec((DE, D))],
        out_specs=pl.BlockSpec(memory_space=pl.ANY),
        scratch_shapes=[pltpu.VMEM((MOE_ROWS * 8, LANES), jnp.uint32), pltpu.VMEM((MOE_ROWS * 8, LANES), jnp.uint32),
                        pltpu.VMEM((MOE_ROWS, D), BF16), pltpu.VMEM((MOE_ROWS, DE), BF16),
                        pltpu.VMEM((MOE_ROWS * D // LANES, LANES), F32), pltpu.VMEM((MOE_ROWS * D // LANES, LANES), F32),
                        pltpu.VMEM((D, DE), BF16), pltpu.VMEM((D, DE), BF16), pltpu.VMEM((DE, D), BF16),
                        pltpu.SMEM((4 * MOE_ROWS,), I32),
                        pltpu.SemaphoreType.DMA((4,)), pltpu.SemaphoreType.DMA((2,)), pltpu.SemaphoreType.DMA((2,))],
    )
    return pl.pallas_call(
        functools.partial(_moe_kernel, n_tok=T),
        grid_spec=grid_spec,
        out_shape=jax.ShapeDtypeStruct(((TOP_K * T + 2 * MOE_ROWS) * D // LANES, LANES), F32),
        compiler_params=_cparams(("arbitrary",)),
        name="moe_experts",
    )(blk_exp, blk_val, slot_ids.reshape(nb, MOE_ROWS), hp, w_gate, w_up, w_down)


def _ffn_out_kernel(*refs, final):
    y_refs = refs[:TOP_K]
    gk_ref, h_ref, x_ref, gf_ref, wsg_ref, wsu_ref, wsd_ref, fg_ref, o_ref, r_s = refs[TOP_K:]
    tm, D = x_ref.shape
    ys = D // LANES
    acc = None
    for k in range(TOP_K):
        gk = jnp.broadcast_to(gk_ref[k][:, None, :], (tm, ys, LANES)).reshape(tm * ys, LANES)
        term = y_refs[k][...] * gk
        acc = term if acc is None else acc + term
    r_s[...] = acc
    routed = jnp.concatenate([r_s[pl.ds(s, tm, stride=ys), :] for s in range(ys)], axis=1)
    hb = h_ref[...]
    gt = jnp.dot(hb, wsg_ref[...], preferred_element_type=F32)
    up = jnp.dot(hb, wsu_ref[...], preferred_element_type=F32)
    mid = (gt * jax.nn.sigmoid(gt) * up).astype(BF16)
    shared = jnp.dot(mid, wsd_ref[...], preferred_element_type=F32)
    out = x_ref[...] + gf_ref[0] * (routed + shared)
    if final:
        out = _rms(out, fg_ref[...])
    o_ref[...] = out


def _ffn_out(y, gate3, h, x, g_f, wsg, wsu, wsd, final_g, final, S, tm=128):
    T, D = x.shape
    B = g_f.shape[0]
    DS = wsg.shape[1]
    tm = min(tm, S)
    nt = T // tm
    full = lambda shape: pl.BlockSpec(shape, lambda i: (0,) * len(shape))
    y_specs = [pl.BlockSpec((tm * D // LANES, LANES), functools.partial(lambda i, k: (k * nt + i, 0), k=k))
               for k in range(TOP_K)]
    return pl.pallas_call(
        functools.partial(_ffn_out_kernel, final=final),
        grid=(nt,),
        in_specs=y_specs + [
                  pl.BlockSpec((TOP_K, tm, 1), lambda i: (0, i, 0)),
                  pl.BlockSpec((tm, D), lambda i: (i, 0)),
                  pl.BlockSpec((tm, D), lambda i: (i, 0)),
                  pl.BlockSpec((1, 1, D), lambda i: (i * tm // S, 0, 0)),
                  full((D, DS)), full((D, DS)), full((DS, D)), full((1, D))],
        out_specs=pl.BlockSpec((tm, D), lambda i: (i, 0)),
        out_shape=jax.ShapeDtypeStruct((T, D), F32),
        scratch_shapes=[pltpu.VMEM((tm * D // LANES, LANES), F32)],
        compiler_params=_cparams(("parallel",)),
        name="ffn_combine",
    )(*([y] * TOP_K), gate3, h, x, g_f.reshape(B, 1, D), wsg, wsu, wsd, final_g.reshape(1, D))


def _layout_w_in(w):
    D = w.shape[0]
    sizes = (Q_LORA, KV_LORA, MLA_ROPE, 512, 512, 512, 512, HEAD_DIM, HEAD_DIM, IDX_HEADS * IDX_DIM, IDX_DIM, IDX_HEADS)
    offs = np.concatenate([[0], np.cumsum(sizes)])
    cq, ckv, kr, sbq, sbk, sbv, dsq, dsk, dsv, ixq, ixk, ixw = [w[:, offs[n]:offs[n + 1]] for n in range(12)]
    z = lambda n: jnp.zeros((D, n), w.dtype)
    out = jnp.concatenate([cq, sbq, sbk, sbv, dsq, ckv, dsk, dsv, ixq, kr, z(64), ixk, z(64), ixw, z(112), z(128)], axis=1)
    assert out.shape[1] == IN_PAD
    return out.astype(BF16)


def _layout_w_uq(w):
    K = w.shape[0]
    w3 = w.reshape(K, MLA_HEADS, MLA_NOPE + MLA_ROPE)
    w3 = jnp.concatenate([w3, jnp.zeros((K, MLA_HEADS, 64), w.dtype)], axis=2)
    return w3.reshape(K, MLA_HEADS * 256).astype(BF16)


def _layout_w_ukv(w):
    K = w.shape[0]
    w3 = w.reshape(K, MLA_HEADS, MLA_NOPE + HEAD_DIM)
    return jnp.concatenate([w3[:, :, :MLA_NOPE].reshape(K, -1), w3[:, :, MLA_NOPE:].reshape(K, -1)], axis=1).astype(BF16)


def _expert_blocks(eidx, pos, counts, T):
    M = T * TOP_K
    nb = M // MOE_ROWS + N_EXPERTS
    P = nb * MOE_ROWS
    padded = (counts + MOE_ROWS - 1) // MOE_ROWS * MOE_ROWS
    pend = jnp.cumsum(padded)
    pstart = pend - padded
    experts = jnp.arange(N_EXPERTS, dtype=I32)
    dest = jnp.sum(jnp.where(eidx[:, :, None] == experts, pstart, 0), axis=-1) + pos
    out_row = jnp.arange(TOP_K, dtype=I32)[:, None] * T + jnp.arange(T, dtype=I32)[None, :]
    pad_row = M + ((jnp.arange(P, dtype=I32) // MOE_ROWS) % 2) * MOE_ROWS + jnp.arange(P, dtype=I32) % MOE_ROWS
    slot_ids = pad_row.at[dest.reshape(-1)].set(out_row.reshape(-1), unique_indices=True)
    blk_start = jnp.arange(nb, dtype=I32) * MOE_ROWS
    blk_exp = jnp.minimum(jnp.sum((pend[None, :] <= blk_start[:, None]).astype(I32), axis=1), N_EXPERTS - 1)
    real_end = jnp.sum(jnp.where(blk_exp[:, None] == experts, pstart + counts, 0), axis=-1)
    blk_val = jnp.clip(real_end - blk_start, 0, MOE_ROWS).astype(I32)
    return slot_ids, blk_exp, blk_val


def _mixers(x, S, B, positions_tabs, mod, l, w):
    sh_m, sc_m, g_m = mod[0], mod[1], mod[2]
    proj = _nmm(x, 0, x.shape[1], w["norm_mix_g"][l], w["w_in"][l], S, sc=sc_m, sh=sh_m)
    q_raw = _nmm(proj, C_CQ // Q_LORA, Q_LORA, w["g_cq"][l], w["w_uq"][l], S)
    kv_raw = _nmm(proj, C_CKV // KV_LORA, KV_LORA, w["g_ckv"][l], w["w_ukv"][l], S)
    tabs64, tabs128 = positions_tabs
    (qm, km, vm, sbq, sbk, sbv, dq, dk, dv, iq, ika, ikb, iw) = _prep(proj, q_raw, kv_raw, tabs64, tabs128)
    g_out = w["g_out"][l].reshape(1, -1)
    o_a = _mla(qm, km, vm.T, g_out, B, S)
    o_b = _sb(sbq, sbk, sbv.T, g_out, B, S)
    wt = iw[:, :IDX_HEADS].reshape(B, S, IDX_HEADS).transpose(0, 2, 1)
    o_c = _dsa(iq, ika, ikb, wt, dq, dk, dv.T, g_out, B, S)
    o = jnp.concatenate([o_a, o_b, o_c], axis=1)
    return _mm_res(o, w["w_o"][l], x, g_m, S)


def _ffn(x, S, B, mod, l, w, final_g, final):
    T, D = x.shape
    sh_f, sc_f, g_f = mod[3], mod[4], mod[5]
    h, hp, eidx, gate, pos, cnt = _route(x, w["norm_ffn_g"][l], sc_f, sh_f, w["w_router"][l], w["b_router"][l], S)
    counts = cnt[:, 0].astype(I32)
    slot_ids, blk_exp, blk_val = _expert_blocks(eidx, pos, counts, T)
    y = _moe(hp.reshape(T * 8, LANES), slot_ids, blk_exp, blk_val, w["w_gate"], w["w_up"], w["w_down"], l)
    return _ffn_out(y, gate.reshape(TOP_K, T, 1), h, x, g_f, w["ws_gate"][l], w["ws_up"][l], w["ws_down"][l],
                    final_g, final, S)


def kernel(x, c, positions, norm_mix_g, norm_ffn_g, w_ada, b_ada, w_in, g_cq, g_ckv, w_uq, w_ukv, g_out, w_o,
           w_router, b_router, w_gate, w_up, w_down, ws_gate, ws_up, ws_down, final_g):
    B, S, D = x.shape
    L = w_in.shape[0]
    T = B * S
    w = dict(norm_mix_g=norm_mix_g, norm_ffn_g=norm_ffn_g, g_cq=g_cq, g_ckv=g_ckv, g_out=g_out,
             w_in=jax.vmap(_layout_w_in)(w_in), w_uq=jax.vmap(_layout_w_uq)(w_uq), w_ukv=jax.vmap(_layout_w_ukv)(w_ukv),
             w_o=w_o.astype(BF16), w_router=w_router, b_router=b_router, w_gate=w_gate, w_up=w_up, w_down=w_down,
             ws_gate=ws_gate.astype(BF16), ws_up=ws_up.astype(BF16), ws_down=ws_down.astype(BF16))
    mod_all = _ada(c, w_ada, b_ada).reshape(L, B, 6, D)
    posf = positions.astype(F32).reshape(T, 1)
    tabs = (_rope_tables(posf, 64), _rope_tables(posf, 128))
    xt = x.reshape(T, D)
    for l in range(L):
        mod = [mod_all[l, :, n, :] for n in range(6)]
        xt = _mixers(xt, S, B, tabs, mod, l, w)
        xt = _ffn(xt, S, B, mod, l, w, final_g, l == L - 1)
    return xt.reshape(B, S, D)
```

```python
import functools

import jax
import jax.numpy as jnp
import numpy as np
from jax import lax
from jax.experimental import pallas as pl
from jax.experimental.pallas import tpu as pltpu

F32 = jnp.float32
BF16 = jnp.bfloat16
I32 = jnp.int32

HEAD_DIM = 128
MLA_HEADS = 8
SB_HEADS = 4
DSA_HEADS = 4
N_MIX_HEADS = MLA_HEADS + SB_HEADS + DSA_HEADS
Q_LORA = 512
KV_LORA = 256
MLA_NOPE = 128
MLA_ROPE = 64
IDX_HEADS = 16
IDX_DIM = 64
IDX_SCALE = (IDX_HEADS * IDX_DIM) ** -0.5
TOPK_MAX = 256
ROPE_THETA = 10000.0
N_EXPERTS = 64
TOP_K = 8
N_GROUPS = 8
TOPK_GROUPS = 4
ROUTED_SCALE = 2.5
EPS = 1e-6

LANES = 128
VMEM_LIMIT = 56 * 1024 * 1024
NEG_INF = float("-inf")
INT_MIN = -(2 ** 31)

C_CQ, C_SBQ, C_SBK, C_SBV, C_DSQ = 0, 512, 1024, 1536, 2048
C_CKV, C_DSK, C_DSV, C_IXQ = 2560, 2816, 2944, 3072
C_KR, C_IXK, C_IXW, IN_PAD = 4096, 4224, 4352, 4608

MOE_ROWS = 256


def _cparams(sem):
    return pltpu.CompilerParams(dimension_semantics=sem, vmem_limit_bytes=VMEM_LIMIT)


def _nt_dot(a, b):
    return lax.dot_general(a, b, (((1,), (1,)), ((), ())), preferred_element_type=F32)


def _rms(x, g):
    return x * lax.rsqrt(jnp.mean(x * x, axis=-1, keepdims=True) + EPS) * g


def _ada_kernel(c_ref, w_ref, b_ref, o_ref):
    c = c_ref[...]
    a = (c * jax.nn.sigmoid(c)).astype(BF16)
    o_ref[0] = jnp.dot(a, w_ref[0].astype(BF16), preferred_element_type=F32) + b_ref[0]


def _ada(c, w_ada, b_ada):
    L, D, N = w_ada.shape
    B = c.shape[0]
    tn = 1024
    return pl.pallas_call(
        _ada_kernel,
        grid=(L, N // tn),
        in_specs=[pl.BlockSpec((B, D), lambda l, j: (0, 0)),
                  pl.BlockSpec((1, D, tn), lambda l, j: (l, 0, j)),
                  pl.BlockSpec((1, 1, tn), lambda l, j: (l, 0, j))],
        out_specs=pl.BlockSpec((1, B, tn), lambda l, j: (l, 0, j)),
        out_shape=jax.ShapeDtypeStruct((L, B, N), F32),
        compiler_params=_cparams(("parallel", "parallel")),
        name="ada_mod",
    )(c, w_ada, b_ada.reshape(L, 1, N))


def _rope_tab_kernel(pos_ref, inv_ref, sgn_ref, cos_ref, sin_ref):
    ang = pos_ref[...] * inv_ref[...]
    cos_ref[...] = jnp.cos(ang)
    sin_ref[...] = jnp.sin(ang) * sgn_ref[...]


def _rope_tables(posf, d):
    T = posf.shape[0]
    half = d // 2
    inv = ROPE_THETA ** (-jnp.arange(0, d, 2, dtype=F32) / d)
    reps = LANES // half
    inv_t = jnp.tile(inv, reps).reshape(1, LANES)
    sgn = jnp.tile(jnp.concatenate([-jnp.ones((half,), F32), jnp.ones((half,), F32)]), LANES // d).reshape(1, LANES)
    tm = min(T, 1024)
    return pl.pallas_call(
        _rope_tab_kernel,
        grid=(T // tm,),
        in_specs=[pl.BlockSpec((tm, 1), lambda i: (i, 0)),
                  pl.BlockSpec((1, LANES), lambda i: (0, 0)),
                  pl.BlockSpec((1, LANES), lambda i: (0, 0))],
        out_specs=[pl.BlockSpec((tm, LANES), lambda i: (i, 0))] * 2,
        out_shape=[jax.ShapeDtypeStruct((T, LANES), F32)] * 2,
        compiler_params=_cparams(("parallel",)),
        name=f"rope_tab{d}",
    )(posf, inv_t, sgn)


def _rope64(x, cos, sin_s, first_half):
    rot = jnp.where(first_half, pltpu.roll(x, 96, 1), pltpu.roll(x, 32, 1))
    return x * cos + rot * sin_s


def _rope128(x, cos, sin_s):
    return x * cos + pltpu.roll(x, 64, 1) * sin_s


def _nmm_kernel(*refs, modulate):
    if modulate:
        x_ref, g_ref, sc_ref, sh_ref, w_ref, o_ref, h_s = refs
    else:
        x_ref, g_ref, w_ref, o_ref, h_s = refs

    @pl.when(pl.program_id(1) == 0)
    def _():
        h = _rms(x_ref[...], g_ref[...])
        if modulate:
            h = h * (1.0 + sc_ref[0]) + sh_ref[0]
        h_s[...] = h.astype(BF16)

    o_ref[...] = jnp.dot(h_s[...], w_ref[...], preferred_element_type=F32).astype(o_ref.dtype)


def _nmm(x, col_blk, K, g, w, S, sc=None, sh=None, tm=1024, tn=512, out_dtype=F32):
    T = x.shape[0]
    N = w.shape[1]
    tm = min(tm, S)
    tn = min(tn, N)
    modulate = sc is not None
    in_specs = [pl.BlockSpec((tm, K), lambda i, j: (i, col_blk)),
                pl.BlockSpec((1, K), lambda i, j: (0, 0))]
    args = [x, g.reshape(1, K)]
    if modulate:
        B = sc.shape[0]
        in_specs += [pl.BlockSpec((1, 1, K), lambda i, j: (i * tm // S, 0, 0))] * 2
        args += [sc.reshape(B, 1, K), sh.reshape(B, 1, K)]
    in_specs.append(pl.BlockSpec((K, tn), lambda i, j: (0, j)))
    args.append(w)
    return pl.pallas_call(
        functools.partial(_nmm_kernel, modulate=modulate),
        grid=(T // tm, N // tn),
        in_specs=in_specs,
        out_specs=pl.BlockSpec((tm, tn), lambda i, j: (i, j)),
        out_shape=jax.ShapeDtypeStruct((T, N), out_dtype),
        scratch_shapes=[pltpu.VMEM((tm, K), BF16)],
        compiler_params=_cparams(("parallel", "arbitrary")),
        name="norm_matmul",
    )(*args)


def _mm_res_kernel(a_ref, w_ref, r_ref, gt_ref, o_ref):
    acc = jnp.dot(a_ref[...], w_ref[...], preferred_element_type=F32)
    o_ref[...] = r_ref[...] + gt_ref[0] * acc


def _mm_res(a, w, res, gate, S, tm=1024, tn=512):
    T, K = a.shape
    N = w.shape[1]
    B = gate.shape[0]
    tm = min(tm, S)
    return pl.pallas_call(
        _mm_res_kernel,
        grid=(T // tm, N // tn),
        in_specs=[pl.BlockSpec((tm, K), lambda i, j: (i, 0)),
                  pl.BlockSpec((K, tn), lambda i, j: (0, j)),
                  pl.BlockSpec((tm, tn), lambda i, j: (i, j)),
                  pl.BlockSpec((1, 1, tn), lambda i, j: (i * tm // S, 0, j))],
        out_specs=pl.BlockSpec((tm, tn), lambda i, j: (i, j)),
        out_shape=jax.ShapeDtypeStruct((T, N), F32),
        compiler_params=_cparams(("parallel", "parallel")),
        name="out_proj_residual",
    )(a, w, res, gate.reshape(B, 1, N))


def _prep_kernel(p_ref, q_ref, kv_ref, c64_ref, s64_ref, c128_ref, s128_ref,
                 qm_ref, km_ref, vm_ref, sbq_ref, sbk_ref, sbv_ref,
                 dq_ref, dk_ref, dv_ref, iq_ref, ika_ref, ikb_ref, iw_ref):
    c64, s64 = c64_ref[...], s64_ref[...]
    c128, s128 = c128_ref[...], s128_ref[...]
    lane = lax.broadcasted_iota(I32, c64.shape, 1)
    first = (lane % 64) < 32

    def sl(ref, off, width=LANES):
        return ref[:, off:off + width]

    kr = _rope64(sl(p_ref, C_KR), c64, s64, first).astype(BF16)
    for h in range(MLA_HEADS):
        qm_ref[:, 256 * h:256 * h + 128] = sl(q_ref, 256 * h).astype(BF16)
        qm_ref[:, 256 * h + 128:256 * h + 256] = _rope64(sl(q_ref, 256 * h + 128), c64, s64, first).astype(BF16)
        km_ref[:, 256 * h:256 * h + 128] = sl(kv_ref, 128 * h).astype(BF16)
        km_ref[:, 256 * h + 128:256 * h + 256] = kr
    vm_ref[...] = kv_ref[:, MLA_HEADS * 128:].astype(BF16)
    sbq_ref[...] = sl(p_ref, C_SBQ, 512).astype(BF16)
    sbk_ref[...] = sl(p_ref, C_SBK, 512).astype(BF16)
    sbv_ref[...] = sl(p_ref, C_SBV, 512).astype(BF16)
    for h in range(DSA_HEADS):
        dq_ref[:, 128 * h:128 * h + 128] = _rope128(sl(p_ref, C_DSQ + 128 * h), c128, s128).astype(BF16)
    dk_ref[...] = _rope128(sl(p_ref, C_DSK), c128, s128).astype(BF16)
    dv_ref[...] = sl(p_ref, C_DSV).astype(BF16)
    for j in range(IDX_HEADS // 2):
        iq_ref[:, 128 * j:128 * j + 128] = _rope64(sl(p_ref, C_IXQ + 128 * j), c64, s64, first).astype(BF16)
    ik = _rope64(sl(p_ref, C_IXK), c64, s64, first)
    ika_ref[...] = ik.astype(BF16)
    ikb_ref[...] = pltpu.roll(ik, 64, 1).astype(BF16)
    iw_ref[...] = sl(p_ref, C_IXW)


def _prep(proj, q_raw, kv_raw, tabs64, tabs128, tm=256):
    T = proj.shape[0]
    row = lambda w: pl.BlockSpec((tm, w), lambda i: (i, 0))
    widths = [2048, 2048, 1024, 512, 512, 512, 512, 128, 128, 1024, 128, 128]
    out_shape = [jax.ShapeDtypeStruct((T, w), BF16) for w in widths] + [jax.ShapeDtypeStruct((T, LANES), F32)]
    return pl.pallas_call(
        _prep_kernel,
        grid=(T // tm,),
        in_specs=[row(IN_PAD), row(2048), row(2048), row(LANES), row(LANES), row(LANES), row(LANES)],
        out_specs=[row(w) for w in widths] + [row(LANES)],
        out_shape=out_shape,
        compiler_params=_cparams(("parallel",)),
        name="mixer_prep",
    )(proj, q_raw, kv_raw, *tabs64, *tabs128)


def _head_out(o, g):
    return _rms(o, g).astype(BF16)


LOG2E = 1.4426950408889634


def _mla_kernel(q_ref, k_ref, vt_ref, g_ref, o_ref, m_s, l_s, acc_s, *, t, nq, scale):
    i = pl.program_id(2)
    hps = m_s.shape[0]
    c = scale * LOG2E
    m_s[...] = jnp.full(m_s.shape, NEG_INF, F32)
    l_s[...] = jnp.zeros(l_s.shape, F32)
    acc_s[...] = jnp.zeros(acc_s.shape, F32)

    def step(cidx, masked):
        k0 = pl.multiple_of(cidx * t, t)
        for hh in range(hps):
            s = _nt_dot(k_ref[pl.ds(k0, t), 256 * hh:256 * hh + 256], q_ref[:, 256 * hh:256 * hh + 256])
            if masked:
                key = lax.broadcasted_iota(I32, (t, t), 0)
                qry = lax.broadcasted_iota(I32, (t, t), 1)
                s = jnp.where(key <= qry, s, NEG_INF)
            m_prev = m_s[hh]
            m_new = jnp.maximum(m_prev, jnp.max(s, axis=0, keepdims=True))
            alpha = jnp.exp2((m_prev - m_new) * c)
            p = jnp.exp2((s - m_new) * c)
            l_s[hh] = alpha * l_s[hh] + jnp.sum(p, axis=0, keepdims=True)
            acc_s[hh] = alpha * acc_s[hh] + jnp.dot(vt_ref[128 * hh:128 * hh + 128, pl.ds(k0, t)], p.astype(BF16),
                                                    preferred_element_type=F32)
            m_s[hh] = m_new

    for n in range(nq):
        @pl.when(i == n)
        def _():
            for cidx in range(n):
                step(cidx, False)
            step(n, True)

    for hh in range(hps):
        o_ref[:, 128 * hh:128 * hh + 128] = _head_out((acc_s[hh] / l_s[hh]).T, g_ref[:, 128 * hh:128 * hh + 128])


def _mla(qm, km, vmt, g_out, B, S, t=1024, hps=2):
    T = qm.shape[0]
    t = min(t, S)
    nq = S // t
    H = MLA_HEADS
    return pl.pallas_call(
        functools.partial(_mla_kernel, t=t, nq=nq, scale=(MLA_NOPE + MLA_ROPE) ** -0.5),
        grid=(B, H // hps, nq),
        in_specs=[pl.BlockSpec((t, 256 * hps), lambda b, h, i: (b * nq + i, h)),
                  pl.BlockSpec((S, 256 * hps), lambda b, h, i: (b, h)),
                  pl.BlockSpec((128 * hps, S), lambda b, h, i: (h, b)),
                  pl.BlockSpec((1, 128 * hps), lambda b, h, i: (0, h))],
        out_specs=pl.BlockSpec((t, 128 * hps), lambda b, h, i: (b * nq + i, h)),
        out_shape=jax.ShapeDtypeStruct((T, H * HEAD_DIM), BF16),
        scratch_shapes=[pltpu.VMEM((hps, 1, t), F32), pltpu.VMEM((hps, 1, t), F32), pltpu.VMEM((hps, 128, t), F32)],
        compiler_params=_cparams(("parallel", "parallel", "arbitrary")),
        name="mla_attention",
    )(qm, km, vmt, g_out)


def _sb_kernel(q_ref, k_ref, vt_ref, g_ref, tri_ref, o_ref, carry_s, acc_s, *, t, nq, scale):
    i = pl.program_id(2)
    hps = carry_s.shape[0]
    tri = tri_ref[...]
    carry_s[...] = jnp.zeros(carry_s.shape, F32)
    acc_s[...] = jnp.zeros(acc_s.shape, F32)

    def step(cidx, masked):
        k0 = pl.multiple_of(cidx * t, t)
        for hh in range(hps):
            z = _nt_dot(k_ref[pl.ds(k0, t), 128 * hh:128 * hh + 128], q_ref[:, 128 * hh:128 * hh + 128]) * scale
            sp = jnp.log(1.0 + jnp.exp(-jnp.abs(z)))
            log_beta = jnp.minimum(z, 0.0) - sp
            log_keep = jnp.minimum(-z, 0.0) - sp
            if masked:
                key = lax.broadcasted_iota(I32, (t, t), 0)
                qry = lax.broadcasted_iota(I32, (t, t), 1)
                strict = key < qry
                log_keep = jnp.where(strict, log_keep, 0.0)
            hi = log_keep.astype(BF16)
            lo = (log_keep - hi.astype(F32)).astype(BF16)
            suffix = jnp.dot(tri, hi, preferred_element_type=F32) + jnp.dot(tri, lo, preferred_element_type=F32)
            a = jnp.exp(log_beta + suffix + carry_s[hh])
            if masked:
                a = jnp.where(strict, a, 0.0)
            acc_s[hh] += jnp.dot(vt_ref[128 * hh:128 * hh + 128, pl.ds(k0, t)], a.astype(BF16),
                                 preferred_element_type=F32)
            carry_s[hh] += jnp.sum(log_keep, axis=0, keepdims=True)

    for n in range(nq):
        @pl.when(i == n)
        def _():
            step(n, True)
            for cidx in range(n - 1, -1, -1):
                step(cidx, False)

    for hh in range(hps):
        o_ref[:, 128 * hh:128 * hh + 128] = _head_out(acc_s[hh].T, g_ref[:, 128 * hh:128 * hh + 128])


def _sb(sbq, sbk, sbvt, g_out, B, S, t=512, hps=2):
    T = sbq.shape[0]
    t = min(t, S)
    nq = S // t
    H = SB_HEADS
    r = np.arange(t)
    tri = jnp.asarray((r[None, :] > r[:, None]).astype(np.float32), BF16)
    g0 = MLA_HEADS // hps
    return pl.pallas_call(
        functools.partial(_sb_kernel, t=t, nq=nq, scale=HEAD_DIM ** -0.5),
        grid=(B, H // hps, nq),
        in_specs=[pl.BlockSpec((t, 128 * hps), lambda b, h, i: (b * nq + i, h)),
                  pl.BlockSpec((S, 128 * hps), lambda b, h, i: (b, h)),
                  pl.BlockSpec((128 * hps, S), lambda b, h, i: (h, b)),
                  pl.BlockSpec((1, 128 * hps), lambda b, h, i: (0, g0 + h)),
                  pl.BlockSpec((t, t), lambda b, h, i: (0, 0))],
        out_specs=pl.BlockSpec((t, 128 * hps), lambda b, h, i: (b * nq + i, h)),
        out_shape=jax.ShapeDtypeStruct((T, H * HEAD_DIM), BF16),
        scratch_shapes=[pltpu.VMEM((hps, 1, t), F32), pltpu.VMEM((hps, 128, t), F32)],
        compiler_params=_cparams(("parallel", "parallel", "arbitrary")),
        name="stickbreak_attention",
    )(sbq, sbk, sbvt, g_out, tri)


def _dsa_kernel(iq_ref, ika_ref, ikb_ref, wt_ref, dq_ref, dk_ref, vt_ref, g_ref, o_ref,
                key_s, m_s, l_s, acc_s, *, tq, tk, n_sel, scale):
    i = pl.program_id(1)
    nch = (i * tq + tq + tk - 1) // tk
    qpos = i * tq + lax.broadcasted_iota(I32, (1, tq), 1)
    kiota = lax.broadcasted_iota(I32, (tk, 1), 0)
    wt = wt_ref[0]
    half = IDX_HEADS // 2

    def score_chunk(c, carry):
        k0 = pl.multiple_of(c * tk, tk)
        kk = jnp.concatenate([ika_ref[pl.ds(k0, tk), :], ikb_ref[pl.ds(k0, tk), :]], axis=0)
        score = jnp.zeros((tk, tq), F32)
        for j in range(half):
            r = jnp.maximum(_nt_dot(kk, iq_ref[:, 128 * j:128 * j + 128]), 0.0)
            score = score + r[:tk] * wt[2 * j:2 * j + 1, :] + r[tk:] * wt[2 * j + 1:2 * j + 2, :]
        score = score * IDX_SCALE
        score = jnp.where(k0 + kiota <= qpos, score, NEG_INF)
        bits = pltpu.bitcast(score, I32)
        key_s[pl.ds(k0, tk), :] = jnp.where(bits < 0, bits ^ jnp.int32(0x7FFFFFFF), bits)
        return carry

    lax.fori_loop(0, nch, score_chunk, 0)

    def count(pred):
        def body(c, acc):
            k0 = pl.multiple_of(c * tk, tk)
            hit = pred(key_s[pl.ds(k0, tk), :], k0 + kiota).astype(I32)
            return acc + jnp.sum(hit.reshape(tk // 8, 8, tq), axis=0)
        part = lax.fori_loop(0, nch, body, jnp.zeros((8, tq), I32))
        return jnp.sum(part, axis=0, keepdims=True)

    c0 = count(lambda k, idx: k >= 0)
    thr = jnp.where(c0 >= n_sel, jnp.int32(0), jnp.int32(INT_MIN))

    def vbit(b, thr):
        cand = thr + jnp.left_shift(jnp.int32(1), 30 - b)
        cnt = count(lambda k, idx: k >= cand)
        return jnp.where(cnt >= n_sel, cand, thr)

    thr = lax.fori_loop(0, 31, vbit, thr)
    n_ge = count(lambda k, idx: k >= thr)
    idx_bits = max(1, int(np.ceil(np.log2(key_s.shape[0]))))

    def tie_bound():
        need = n_sel - count(lambda k, idx: k > thr)

        def ibit(b, bound):
            cand = bound + jnp.left_shift(jnp.int32(1), idx_bits - 1 - b)
            cnt = count(lambda k, idx: (k == thr) & (idx < cand))
            return jnp.where(cnt < need, cand, bound)

        return lax.fori_loop(0, idx_bits, ibit, jnp.zeros((1, tq), I32))

    bound = lax.cond(jnp.max(n_ge) > n_sel, tie_bound, lambda: jnp.full((1, tq), 2 ** idx_bits, I32))

    qc = jnp.concatenate([dq_ref[:, 128 * h:128 * h + 128] for h in range(DSA_HEADS)], axis=0)
    sc2 = scale * LOG2E
    m_s[...] = jnp.full(m_s.shape, NEG_INF, F32)
    l_s[...] = jnp.zeros(l_s.shape, F32)
    acc_s[...] = jnp.zeros(acc_s.shape, F32)

    def attn_chunk(c, carry):
        k0 = pl.multiple_of(c * tk, tk)
        key = key_s[pl.ds(k0, tk), :]
        idx = k0 + kiota
        sel = ((key > thr) | ((key == thr) & (idx <= bound))) & (idx <= qpos)
        bias = jnp.where(sel, 0.0, NEG_INF)
        s = _nt_dot(dk_ref[pl.ds(k0, tk), :], qc) + jnp.concatenate([bias] * DSA_HEADS, axis=1)
        m_prev = m_s[...]
        m_new = jnp.maximum(m_prev, jnp.max(s, axis=0, keepdims=True))
        m_safe = jnp.where(m_new == NEG_INF, 0.0, m_new)
        alpha = jnp.exp2((m_prev - m_safe) * sc2)
        p = jnp.exp2((s - m_safe) * sc2)
        l_s[...] = alpha * l_s[...] + jnp.sum(p, axis=0, keepdims=True)
        acc_s[...] = alpha * acc_s[...] + jnp.dot(vt_ref[:, pl.ds(k0, tk)], p.astype(BF16),
                                                  preferred_element_type=F32)
        m_s[...] = m_new
        return carry

    lax.fori_loop(0, nch, attn_chunk, 0)
    ot = acc_s[...] / l_s[...]
    for h in range(DSA_HEADS):
        o_ref[:, 128 * h:128 * h + 128] = _head_out(ot[:, h * tq:(h + 1) * tq].T, g_ref[:, 128 * h:128 * h + 128])


def _dsa(iq, ika, ikb, wt, dq, dk, vt, g_out, B, S, tq=256, tk=256):
    T = iq.shape[0]
    tk = min(tk, S)
    nq = S // tq
    H = DSA_HEADS
    n_sel = min(TOPK_MAX, S // 4)
    g_c = g_out[:, (MLA_HEADS + SB_HEADS) * HEAD_DIM:]
    return pl.pallas_call(
        functools.partial(_dsa_kernel, tq=tq, tk=tk, n_sel=n_sel, scale=HEAD_DIM ** -0.5),
        grid=(B, nq),
        in_specs=[pl.BlockSpec((tq, IDX_HEADS * IDX_DIM), lambda b, i: (b * nq + i, 0)),
                  pl.BlockSpec((S, 128), lambda b, i: (b, 0)),
                  pl.BlockSpec((S, 128), lambda b, i: (b, 0)),
                  pl.BlockSpec((1, IDX_HEADS, tq), lambda b, i: (b, 0, i)),
                  pl.BlockSpec((tq, H * 128), lambda b, i: (b * nq + i, 0)),
                  pl.BlockSpec((S, 128), lambda b, i: (b, 0)),
                  pl.BlockSpec((128, S), lambda b, i: (0, b)),
                  pl.BlockSpec((1, H * 128), lambda b, i: (0, 0))],
        out_specs=pl.BlockSpec((tq, H * 128), lambda b, i: (b * nq + i, 0)),
        out_shape=jax.ShapeDtypeStruct((T, H * HEAD_DIM), BF16),
        scratch_shapes=[pltpu.VMEM((S, tq), I32), pltpu.VMEM((1, H * tq), F32), pltpu.VMEM((1, H * tq), F32),
                        pltpu.VMEM((128, H * tq), F32)],
        compiler_params=_cparams(("parallel", "arbitrary")),
        name="dsa_attention",
    )(iq, ika, ikb, wt, dq, dk, vt, g_c)


def _route_kernel(x_ref, g_ref, sc_ref, sh_ref, wr_ref, br_ref, tri_ref,
                  h_ref, hp_ref, eidx_ref, gate_ref, pos_ref, cnt_ref, run_s):
    @pl.when(pl.program_id(0) == 0)
    def _():
        run_s[...] = jnp.zeros(run_s.shape, F32)

    h = _rms(x_ref[...], g_ref[...]) * (1.0 + sc_ref[0]) + sh_ref[0]
    hb = h.astype(BF16)
    h_ref[...] = hb
    tm, D = h.shape
    bits = pltpu.bitcast(hb.astype(F32), jnp.uint32)
    hp_ref[...] = (bits[:, :D // 2] >> 16) | (bits[:, D // 2:] & jnp.uint32(0xFFFF0000))
    logits = _nt_dot(wr_ref[...], hb)
    scores = jax.nn.sigmoid(logits)
    biased = scores + br_ref[...]
    gsz = N_EXPERTS // N_GROUPS
    b3 = biased.reshape(N_GROUPS, gsz, tm)
    m1 = jnp.max(b3, axis=1, keepdims=True)
    n1 = jnp.sum((b3 == m1).astype(F32), axis=1, keepdims=True)
    m2 = jnp.max(jnp.where(b3 < m1, b3, NEG_INF), axis=1, keepdims=True)
    grp = (m1 + jnp.where(n1 >= 2.0, m1, m2)).reshape(N_GROUPS, tm)
    gi = lax.broadcasted_iota(I32, (N_GROUPS, 1), 0)
    grank = jnp.zeros((N_GROUPS, tm), F32)
    for g in range(N_GROUPS):
        rowv = grp[g:g + 1, :]
        grank = grank + jnp.where((rowv > grp) | ((rowv == grp) & (g < gi)), 1.0, 0.0)
    gmask = grank.reshape(N_GROUPS, 1, tm) < float(TOPK_GROUPS)
    masked = jnp.where(gmask, b3, NEG_INF).reshape(N_EXPERTS, tm)
    ei = lax.broadcasted_iota(I32, (N_EXPERTS, 1), 0)
    rank = jnp.zeros((N_EXPERTS, tm), F32)
    for e in range(N_EXPERTS):
        rowv = masked[e:e + 1, :]
        rank = rank + jnp.where((rowv > masked) | ((rowv == masked) & (e < ei)), 1.0, 0.0)
    sel = rank < float(TOP_K)
    selm = sel.astype(F32)
    gsum = jnp.sum(scores * selm, axis=0, keepdims=True)
    gate = scores * selm / gsum * ROUTED_SCALE
    within = jnp.dot(selm.astype(BF16), tri_ref[...], preferred_element_type=F32)
    posf = within + run_s[...]
    run_s[...] += jnp.sum(selm, axis=1, keepdims=True)
    cnt_ref[...] = jnp.broadcast_to(run_s[...], cnt_ref.shape)
    eif = ei.astype(F32)
    for k in range(TOP_K):
        onek = rank == float(k)
        eidx_ref[k:k + 1, :] = jnp.sum(jnp.where(onek, eif, 0.0), axis=0, keepdims=True).astype(I32)
        gate_ref[k:k + 1, :] = jnp.sum(jnp.where(onek, gate, 0.0), axis=0, keepdims=True)
        pos_ref[k:k + 1, :] = jnp.sum(jnp.where(onek, posf, 0.0), axis=0, keepdims=True).astype(I32)


def _route(x, g, sc, sh, w_router, b_router, S, tm=512):
    T, D = x.shape
    B = sc.shape[0]
    tm = min(tm, S)
    r = np.arange(tm)
    tri = jnp.asarray((r[:, None] < r[None, :]).astype(np.float32), BF16)
    full = lambda shape: pl.BlockSpec(shape, lambda i: (0,) * len(shape))
    return pl.pallas_call(
        _route_kernel,
        grid=(T // tm,),
        in_specs=[pl.BlockSpec((tm, D), lambda i: (i, 0)),
                  full((1, D)),
                  pl.BlockSpec((1, 1, D), lambda i: (i * tm // S, 0, 0)),
                  pl.BlockSpec((1, 1, D), lambda i: (i * tm // S, 0, 0)),
                  full((N_EXPERTS, D)),
                  full((N_EXPERTS, 1)),
                  full((tm, tm))],
        out_specs=[pl.BlockSpec((tm, D), lambda i: (i, 0)),
                   pl.BlockSpec((tm, D // 2), lambda i: (i, 0)),
                   pl.BlockSpec((TOP_K, tm), lambda i: (0, i)),
                   pl.BlockSpec((TOP_K, tm), lambda i: (0, i)),
                   pl.BlockSpec((TOP_K, tm), lambda i: (0, i)),
                   full((N_EXPERTS, LANES))],
        out_shape=[jax.ShapeDtypeStruct((T, D), BF16),
                   jax.ShapeDtypeStruct((T, D // 2), jnp.uint32),
                   jax.ShapeDtypeStruct((TOP_K, T), I32),
                   jax.ShapeDtypeStruct((TOP_K, T), F32),
                   jax.ShapeDtypeStruct((TOP_K, T), I32),
                   jax.ShapeDtypeStruct((N_EXPERTS, LANES), F32)],
        scratch_shapes=[pltpu.VMEM((N_EXPERTS, 1), F32)],
        compiler_params=_cparams(("arbitrary",)),
        name="ffn_norm_route",
    )(x, g.reshape(1, D), sc.reshape(B, 1, D), sh.reshape(B, 1, D),
      w_router.T.astype(BF16), b_router.reshape(N_EXPERTS, 1), tri)


IDX_ALIGN = 1024
IDX_WIN = 2 * IDX_ALIGN


def _moe_kernel(bexp_ref, bval_ref, bcs_ref, slot_hbm, h_hbm, wg_ref, wu_ref, wd_ref, y_hbm,
                xbuf0, xbuf1, x_s, mid_s, ybuf0, ybuf1, wg_s, wu_s, wd_s, idx_s, isem, gsem, ssem, *, n_tok):
    b = pl.program_id(0)
    nb = pl.num_programs(0)
    rows, D = x_s.shape
    half = D // 2
    DE = mid_s.shape[1]
    YS = D // LANES
    NW = 256
    xbufs, ybufs = (xbuf0, xbuf1), (ybuf0, ybuf1)
    n_real = TOP_K * n_tok

    def idx_copy(blk):
        start = pl.multiple_of((bcs_ref[blk] // IDX_ALIGN) * IDX_ALIGN, IDX_ALIGN)
        return pltpu.make_async_copy(slot_hbm.at[pl.ds(start, IDX_WIN)],
                                     idx_s.at[pl.ds(pl.multiple_of((blk & 3) * IDX_WIN, IDX_WIN), IDX_WIN)],
                                     isem.at[blk & 3])

    def block_rows(blk):
        inside = (blk >= 0) & (blk < nb)
        safe = jnp.clip(blk, 0, nb - 1)
        return (blk & 3) * IDX_WIN + bcs_ref[safe] % IDX_ALIGN, jnp.where(inside, bval_ref[safe], 0)

    def row_id(blk, base, valid, r):
        return jnp.where(r < valid, idx_s[base + r], n_real + (blk & 1) * rows + r)

    def gather_start(blk, base, valid, r, p):
        tok = row_id(blk, base, valid, r) & (n_tok - 1)
        pltpu.make_async_copy(h_hbm.at[pl.ds(pl.multiple_of(tok * 8, 8), 8), :],
                              xbufs[p].at[pl.ds(r * 8, 8), :], gsem.at[p]).start(priority=r % 2)

    def scatter_start(blk, base, valid, r, p):
        dst = row_id(blk, base, valid, r)
        pltpu.make_async_copy(ybufs[p].at[pl.ds(r * YS, YS), :],
                              y_hbm.at[pl.ds(pl.multiple_of(dst * YS, YS), YS), :], ssem.at[p]).start(priority=r % 2)

    def wait_gather(p):
        pltpu.make_async_copy(xbufs[p], xbufs[p], gsem.at[p]).wait()

    def wait_scatter(p):
        pltpu.make_async_copy(ybufs[p], ybufs[p], ssem.at[p]).wait()

    def compute_pieces(p):
        def unpack():
            for s in range(8):
                w = xbufs[p][pl.ds(s, rows, stride=8), :]
                x_s[:, 128 * s:128 * s + 128] = pltpu.bitcast(w << 16, F32).astype(BF16)
                x_s[:, half + 128 * s:half + 128 * s + 128] = pltpu.bitcast(w & jnp.uint32(0xFFFF0000), F32).astype(BF16)

        def gate_up(j):
            def piece():
                x = x_s[...]
                gt = jnp.dot(x, wg_s[:, NW * j:NW * j + NW], preferred_element_type=F32)
                up = jnp.dot(x, wu_s[:, NW * j:NW * j + NW], preferred_element_type=F32)
                mid_s[:, NW * j:NW * j + NW] = (gt * jax.nn.sigmoid(gt) * up).astype(BF16)
            return piece

        def down(n):
            def piece():
                y = jnp.dot(mid_s[...], wd_s[:, NW * n:NW * n + NW], preferred_element_type=F32)
                for s in range(NW // LANES):
                    ybufs[p][pl.ds(n * (NW // LANES) + s, rows, stride=YS), :] = y[:, LANES * s:LANES * s + LANES]
            return piece
        return [unpack] + [gate_up(j) for j in range(DE // NW)], [down(n) for n in range(D // NW)]

    def interleave(pieces, starts):
        n_slots = max(len(pieces), 1)
        per = -(-len(starts) // n_slots)
        for n in range(n_slots):
            if pieces:
                pieces[n]()
            for st in starts[n * per:(n + 1) * per]:
                st()

    def run_block(p, compute):
        first, second = compute_pieces(p) if compute else ([], [])
        gbase, gvalid = block_rows(b + 1)
        interleave(first, [functools.partial(gather_start, b + 1, gbase, gvalid, r, 1 - p) for r in range(rows)])
        wait_scatter(p)
        sbase, svalid = block_rows(b - 1)
        interleave(second, [functools.partial(scatter_start, b - 1, sbase, svalid, r, 1 - p) for r in range(rows)])

    @pl.when(b == 0)
    def _():
        ybuf0[...] = jnp.zeros(ybuf0.shape, F32)
        ybuf1[...] = jnp.zeros(ybuf1.shape, F32)
        pltpu.make_async_copy(ybuf0, y_hbm.at[pl.ds(n_real * YS, rows * YS), :], ssem.at[0]).start()
        idx_copy(0).start()
        idx_copy(0).wait()
        base0, valid0 = block_rows(b)
        for r in range(rows):
            gather_start(b, base0, valid0, r, 0)
        idx_copy(1).start()

    @pl.when(b + 2 < nb)
    def _():
        idx_copy(b + 2).start()

    @pl.when(b + 1 < nb)
    def _():
        idx_copy(b + 1).wait()

    @pl.when((b == 0) | (bexp_ref[b] != bexp_ref[jnp.maximum(b - 1, 0)]))
    def _():
        wg_s[...] = wg_ref[0, 0].astype(BF16)
        wu_s[...] = wu_ref[0, 0].astype(BF16)
        wd_s[...] = wd_ref[0, 0].astype(BF16)

    active = bval_ref[b] > 0
    for p in range(2):
        @pl.when((b & 1) == p)
        def _():
            wait_gather(p)

            @pl.when(active)
            def _():
                run_block(p, True)

            @pl.when(jnp.logical_not(active))
            def _():
                run_block(p, False)

            @pl.when(b == nb - 1)
            def _():
                sbase, svalid = block_rows(b)
                for r in range(rows):
                    scatter_start(b, sbase, svalid, r, p)
                wait_scatter(p)
                wait_scatter(1 - p)
                wait_gather(1 - p)


def _moe(hp, slot_ids, blk_exp, blk_val, blk_cs, w_gate, w_up, w_down, l):
    T = hp.shape[0] // 8
    D = w_gate.shape[2]
    assert T & (T - 1) == 0 and D == 2 * 8 * LANES
    DE = w_gate.shape[3]
    nb = blk_exp.shape[0]
    wspec = lambda shape: pl.BlockSpec((1, 1) + shape, lambda b, bexp, bval, bcs: (l, bexp[b], 0, 0))
    grid_spec = pltpu.PrefetchScalarGridSpec(
        num_scalar_prefetch=3,
        grid=(nb,),
        in_specs=[pl.BlockSpec(memory_space=pl.ANY), pl.BlockSpec(memory_space=pl.ANY),
                  wspec((D, DE)), wspec((D, DE)), wspec((DE, D))],
        out_specs=pl.BlockSpec(memory_space=pl.ANY),
        scratch_shapes=[pltpu.VMEM((MOE_ROWS * 8, LANES), jnp.uint32), pltpu.VMEM((MOE_ROWS * 8, LANES), jnp.uint32),
                        pltpu.VMEM((MOE_ROWS, D), BF16), pltpu.VMEM((MOE_ROWS, DE), BF16),
                        pltpu.VMEM((MOE_ROWS * D // LANES, LANES), F32), pltpu.VMEM((MOE_ROWS * D // LANES, LANES), F32),
                        pltpu.VMEM((D, DE), BF16), pltpu.VMEM((D, DE), BF16), pltpu.VMEM((DE, D), BF16),
                        pltpu.SMEM((4 * IDX_WIN,), I32),
                        pltpu.SemaphoreType.DMA((4,)), pltpu.SemaphoreType.DMA((2,)), pltpu.SemaphoreType.DMA((2,))],
    )
    return pl.pallas_call(
        functools.partial(_moe_kernel, n_tok=T),
        grid_spec=grid_spec,
        out_shape=jax.ShapeDtypeStruct(((TOP_K * T + 2 * MOE_ROWS) * D // LANES, LANES), F32),
        compiler_params=_cparams(("arbitrary",)),
        name="moe_experts",
    )(blk_exp, blk_val, blk_cs, slot_ids, hp, w_gate, w_up, w_down)


def _ffn_out_kernel(*refs, final):
    y_refs = refs[:TOP_K]
    gk_ref, h_ref, x_ref, gf_ref, wsg_ref, wsu_ref, wsd_ref, fg_ref, o_ref, r_s = refs[TOP_K:]
    tm, D = x_ref.shape
    ys = D // LANES
    acc = None
    for k in range(TOP_K):
        gk = jnp.broadcast_to(gk_ref[k][:, None, :], (tm, ys, LANES)).reshape(tm * ys, LANES)
        term = y_refs[k][...] * gk
        acc = term if acc is None else acc + term
    r_s[...] = acc
    routed = jnp.concatenate([r_s[pl.ds(s, tm, stride=ys), :] for s in range(ys)], axis=1)
    hb = h_ref[...]
    gt = jnp.dot(hb, wsg_ref[...], preferred_element_type=F32)
    up = jnp.dot(hb, wsu_ref[...], preferred_element_type=F32)
    mid = (gt * jax.nn.sigmoid(gt) * up).astype(BF16)
    shared = jnp.dot(mid, wsd_ref[...], preferred_element_type=F32)
    out = x_ref[...] + gf_ref[0] * (routed + shared)
    if final:
        out = _rms(out, fg_ref[...])
    o_ref[...] = out


def _ffn_out(y, gate3, h, x, g_f, wsg, wsu, wsd, final_g, final, S, tm=128):
    T, D = x.shape
    B = g_f.shape[0]
    DS = wsg.shape[1]
    tm = min(tm, S)
    nt = T // tm
    full = lambda shape: pl.BlockSpec(shape, lambda i: (0,) * len(shape))
    y_specs = [pl.BlockSpec((tm * D // LANES, LANES), functools.partial(lambda i, k: (k * nt + i, 0), k=k))
               for k in range(TOP_K)]
    return pl.pallas_call(
        functools.partial(_ffn_out_kernel, final=final),
        grid=(nt,),
        in_specs=y_specs + [
                  pl.BlockSpec((TOP_K, tm, 1), lambda i: (0, i, 0)),
                  pl.BlockSpec((tm, D), lambda i: (i, 0)),
                  pl.BlockSpec((tm, D), lambda i: (i, 0)),
                  pl.BlockSpec((1, 1, D), lambda i: (i * tm // S, 0, 0)),
                  full((D, DS)), full((D, DS)), full((DS, D)), full((1, D))],
        out_specs=pl.BlockSpec((tm, D), lambda i: (i, 0)),
        out_shape=jax.ShapeDtypeStruct((T, D), F32),
        scratch_shapes=[pltpu.VMEM((tm * D // LANES, LANES), F32)],
        compiler_params=_cparams(("parallel",)),
        name="ffn_combine",
    )(*([y] * TOP_K), gate3, h, x, g_f.reshape(B, 1, D), wsg, wsu, wsd, final_g.reshape(1, D))


def _layout_w_in(w):
    D = w.shape[0]
    sizes = (Q_LORA, KV_LORA, MLA_ROPE, 512, 512, 512, 512, HEAD_DIM, HEAD_DIM, IDX_HEADS * IDX_DIM, IDX_DIM, IDX_HEADS)
    offs = np.concatenate([[0], np.cumsum(sizes)])
    cq, ckv, kr, sbq, sbk, sbv, dsq, dsk, dsv, ixq, ixk, ixw = [w[:, offs[n]:offs[n + 1]] for n in range(12)]
    z = lambda n: jnp.zeros((D, n), w.dtype)
    out = jnp.concatenate([cq, sbq, sbk, sbv, dsq, ckv, dsk, dsv, ixq, kr, z(64), ixk, z(64), ixw, z(112), z(128)], axis=1)
    assert out.shape[1] == IN_PAD
    return out.astype(BF16)


def _layout_w_uq(w):
    K = w.shape[0]
    w3 = w.reshape(K, MLA_HEADS, MLA_NOPE + MLA_ROPE)
    w3 = jnp.concatenate([w3, jnp.zeros((K, MLA_HEADS, 64), w.dtype)], axis=2)
    return w3.reshape(K, MLA_HEADS * 256).astype(BF16)


def _layout_w_ukv(w):
    K = w.shape[0]
    w3 = w.reshape(K, MLA_HEADS, MLA_NOPE + HEAD_DIM)
    return jnp.concatenate([w3[:, :, :MLA_NOPE].reshape(K, -1), w3[:, :, MLA_NOPE:].reshape(K, -1)], axis=1).astype(BF16)


def _expert_blocks(eidx, pos, counts, T):
    M = T * TOP_K
    nb = M // MOE_ROWS + N_EXPERTS
    padded = (counts + MOE_ROWS - 1) // MOE_ROWS * MOE_ROWS
    pend = jnp.cumsum(padded)
    pstart = pend - padded
    cstart = jnp.cumsum(counts) - counts
    experts = jnp.arange(N_EXPERTS, dtype=I32)
    rank = jnp.sum(jnp.where(eidx[:, :, None] == experts, cstart, 0), axis=-1) + pos
    out_row = jnp.arange(TOP_K, dtype=I32)[:, None] * T + jnp.arange(T, dtype=I32)[None, :]
    _, slot_ids = lax.sort_key_val(rank.reshape(-1), out_row.reshape(-1))
    slot_ids = jnp.concatenate([slot_ids, jnp.zeros((IDX_WIN,), I32)])
    blk_start = jnp.arange(nb, dtype=I32) * MOE_ROWS
    blk_exp = jnp.minimum(jnp.sum((pend[None, :] <= blk_start[:, None]).astype(I32), axis=1), N_EXPERTS - 1)
    pick = lambda v: jnp.sum(jnp.where(blk_exp[:, None] == experts, v, 0), axis=-1)
    blk_val = jnp.clip(pick(pstart + counts) - blk_start, 0, MOE_ROWS).astype(I32)
    blk_cs = jnp.where(blk_val > 0, pick(cstart - pstart) + blk_start, 0).astype(I32)
    return slot_ids, blk_exp, blk_val, blk_cs


def _mixers(x, S, B, positions_tabs, mod, l, w):
    sh_m, sc_m, g_m = mod[0], mod[1], mod[2]
    proj = _nmm(x, 0, x.shape[1], w["norm_mix_g"][l], w["w_in"][l], S, sc=sc_m, sh=sh_m)
    q_raw = _nmm(proj, C_CQ // Q_LORA, Q_LORA, w["g_cq"][l], w["w_uq"][l], S)
    kv_raw = _nmm(proj, C_CKV // KV_LORA, KV_LORA, w["g_ckv"][l], w["w_ukv"][l], S)
    tabs64, tabs128 = positions_tabs
    (qm, km, vm, sbq, sbk, sbv, dq, dk, dv, iq, ika, ikb, iw) = _prep(proj, q_raw, kv_raw, tabs64, tabs128)
    g_out = w["g_out"][l].reshape(1, -1)
    o_a = _mla(qm, km, vm.T, g_out, B, S)
    o_b = _sb(sbq, sbk, sbv.T, g_out, B, S)
    wt = iw[:, :IDX_HEADS].reshape(B, S, IDX_HEADS).transpose(0, 2, 1)
    o_c = _dsa(iq, ika, ikb, wt, dq, dk, dv.T, g_out, B, S)
    o = jnp.concatenate([o_a, o_b, o_c], axis=1)
    return _mm_res(o, w["w_o"][l], x, g_m, S)


def _ffn(x, S, B, mod, l, w, final_g, final):
    T, D = x.shape
    sh_f, sc_f, g_f = mod[3], mod[4], mod[5]
    h, hp, eidx, gate, pos, cnt = _route(x, w["norm_ffn_g"][l], sc_f, sh_f, w["w_router"][l], w["b_router"][l], S)
    counts = cnt[:, 0].astype(I32)
    slot_ids, blk_exp, blk_val, blk_cs = _expert_blocks(eidx, pos, counts, T)
    y = _moe(hp.reshape(T * 8, LANES), slot_ids, blk_exp, blk_val, blk_cs, w["w_gate"], w["w_up"], w["w_down"], l)
    return _ffn_out(y, gate.reshape(TOP_K, T, 1), h, x, g_f, w["ws_gate"][l], w["ws_up"][l], w["ws_down"][l],
                    final_g, final, S)


def kernel(x, c, positions, norm_mix_g, norm_ffn_g, w_ada, b_ada, w_in, g_cq, g_ckv, w_uq, w_ukv, g_out, w_o,
           w_router, b_router, w_gate, w_up, w_down, ws_gate, ws_up, ws_down, final_g):
    B, S, D = x.shape
    L = w_in.shape[0]
    T = B * S
    w = dict(norm_mix_g=norm_mix_g, norm_ffn_g=norm_ffn_g, g_cq=g_cq, g_ckv=g_ckv, g_out=g_out,
             w_in=jax.vmap(_layout_w_in)(w_in), w_uq=jax.vmap(_layout_w_uq)(w_uq), w_ukv=jax.vmap(_layout_w_ukv)(w_ukv),
             w_o=w_o.astype(BF16), w_router=w_router, b_router=b_router, w_gate=w_gate, w_up=w_up, w_down=w_down,
             ws_gate=ws_gate.astype(BF16), ws_up=ws_up.astype(BF16), ws_down=ws_down.astype(BF16))
    mod_all = _ada(c, w_ada, b_ada).reshape(L, B, 6, D)
    posf = positions.astype(F32).reshape(T, 1)
    tabs = (_rope_tables(posf, 64), _rope_tables(posf, 128))
    xt = x.reshape(T, D)
    for l in range(L):
        mod = [mod_all[l, :, n, :] for n in range(6)]
        xt = _mixers(xt, S, B, tabs, mod, l, w)
        xt = _ffn(xt, S, B, mod, l, w, final_g, l == L - 1)
    return xt.reshape(B, S, D)
```

```python
import functools

import jax
import jax.numpy as jnp
import numpy as np
from jax import lax
from jax.experimental import pallas as pl
from jax.experimental.pallas import tpu as pltpu

F32 = jnp.float32
BF16 = jnp.bfloat16
I32 = jnp.int32

HEAD_DIM = 128
MLA_HEADS = 8
SB_HEADS = 4
DSA_HEADS = 4
N_MIX_HEADS = MLA_HEADS + SB_HEADS + DSA_HEADS
Q_LORA = 512
KV_LORA = 256
MLA_NOPE = 128
MLA_ROPE = 64
IDX_HEADS = 16
IDX_DIM = 64
IDX_SCALE = (IDX_HEADS * IDX_DIM) ** -0.5
TOPK_MAX = 256
ROPE_THETA = 10000.0
N_EXPERTS = 64
TOP_K = 8
N_GROUPS = 8
TOPK_GROUPS = 4
ROUTED_SCALE = 2.5
EPS = 1e-6

LANES = 128
VMEM_LIMIT = 56 * 1024 * 1024
NEG_INF = float("-inf")
INT_MIN = -(2 ** 31)

C_CQ, C_SBQ, C_SBK, C_SBV, C_DSQ = 0, 512, 1024, 1536, 2048
C_CKV, C_DSK, C_DSV, C_IXQ = 2560, 2816, 2944, 3072
C_KR, C_IXK, C_IXW, IN_PAD = 4096, 4224, 4352, 4608

MOE_ROWS = 256


def _cparams(sem):
    return pltpu.CompilerParams(dimension_semantics=sem, vmem_limit_bytes=VMEM_LIMIT)


def _nt_dot(a, b):
    return lax.dot_general(a, b, (((1,), (1,)), ((), ())), preferred_element_type=F32)


def _rms(x, g):
    return x * lax.rsqrt(jnp.mean(x * x, axis=-1, keepdims=True) + EPS) * g


def _ada_kernel(c_ref, w_ref, b_ref, o_ref):
    c = c_ref[...]
    a = (c * jax.nn.sigmoid(c)).astype(BF16)
    o_ref[0] = jnp.dot(a, w_ref[0].astype(BF16), preferred_element_type=F32) + b_ref[0]


def _ada(c, w_ada, b_ada):
    L, D, N = w_ada.shape
    B = c.shape[0]
    tn = 1024
    return pl.pallas_call(
        _ada_kernel,
        grid=(L, N // tn),
        in_specs=[pl.BlockSpec((B, D), lambda l, j: (0, 0)),
                  pl.BlockSpec((1, D, tn), lambda l, j: (l, 0, j)),
                  pl.BlockSpec((1, 1, tn), lambda l, j: (l, 0, j))],
        out_specs=pl.BlockSpec((1, B, tn), lambda l, j: (l, 0, j)),
        out_shape=jax.ShapeDtypeStruct((L, B, N), F32),
        compiler_params=_cparams(("parallel", "parallel")),
        name="ada_mod",
    )(c, w_ada, b_ada.reshape(L, 1, N))


def _rope_tab_kernel(pos_ref, inv_ref, sgn_ref, cos_ref, sin_ref):
    ang = pos_ref[...] * inv_ref[...]
    cos_ref[...] = jnp.cos(ang)
    sin_ref[...] = jnp.sin(ang) * sgn_ref[...]


def _rope_tables(posf, d):
    T = posf.shape[0]
    half = d // 2
    inv = ROPE_THETA ** (-jnp.arange(0, d, 2, dtype=F32) / d)
    reps = LANES // half
    inv_t = jnp.tile(inv, reps).reshape(1, LANES)
    sgn = jnp.tile(jnp.concatenate([-jnp.ones((half,), F32), jnp.ones((half,), F32)]), LANES // d).reshape(1, LANES)
    tm = min(T, 1024)
    return pl.pallas_call(
        _rope_tab_kernel,
        grid=(T // tm,),
        in_specs=[pl.BlockSpec((tm, 1), lambda i: (i, 0)),
                  pl.BlockSpec((1, LANES), lambda i: (0, 0)),
                  pl.BlockSpec((1, LANES), lambda i: (0, 0))],
        out_specs=[pl.BlockSpec((tm, LANES), lambda i: (i, 0))] * 2,
        out_shape=[jax.ShapeDtypeStruct((T, LANES), F32)] * 2,
        compiler_params=_cparams(("parallel",)),
        name=f"rope_tab{d}",
    )(posf, inv_t, sgn)


def _rope64(x, cos, sin_s, first_half):
    rot = jnp.where(first_half, pltpu.roll(x, 96, 1), pltpu.roll(x, 32, 1))
    return x * cos + rot * sin_s


def _rope128(x, cos, sin_s):
    return x * cos + pltpu.roll(x, 64, 1) * sin_s


def _nmm_kernel(*refs, modulate):
    if modulate:
        x_ref, g_ref, sc_ref, sh_ref, w_ref, o_ref, h_s = refs
    else:
        x_ref, g_ref, w_ref, o_ref, h_s = refs

    @pl.when(pl.program_id(1) == 0)
    def _():
        h = _rms(x_ref[...], g_ref[...])
        if modulate:
            h = h * (1.0 + sc_ref[0]) + sh_ref[0]
        h_s[...] = h.astype(BF16)

    o_ref[...] = jnp.dot(h_s[...], w_ref[...], preferred_element_type=F32).astype(o_ref.dtype)


def _nmm(x, col_blk, K, g, w, S, sc=None, sh=None, tm=1024, tn=512, out_dtype=F32):
    T = x.shape[0]
    N = w.shape[1]
    tm = min(tm, S)
    tn = min(tn, N)
    modulate = sc is not None
    in_specs = [pl.BlockSpec((tm, K), lambda i, j: (i, col_blk)),
                pl.BlockSpec((1, K), lambda i, j: (0, 0))]
    args = [x, g.reshape(1, K)]
    if modulate:
        B = sc.shape[0]
        in_specs += [pl.BlockSpec((1, 1, K), lambda i, j: (i * tm // S, 0, 0))] * 2
        args += [sc.reshape(B, 1, K), sh.reshape(B, 1, K)]
    in_specs.append(pl.BlockSpec((K, tn), lambda i, j: (0, j)))
    args.append(w)
    return pl.pallas_call(
        functools.partial(_nmm_kernel, modulate=modulate),
        grid=(T // tm, N // tn),
        in_specs=in_specs,
        out_specs=pl.BlockSpec((tm, tn), lambda i, j: (i, j)),
        out_shape=jax.ShapeDtypeStruct((T, N), out_dtype),
        scratch_shapes=[pltpu.VMEM((tm, K), BF16)],
        compiler_params=_cparams(("parallel", "arbitrary")),
        name="norm_matmul",
    )(*args)


def _mm_res_kernel(*refs):
    n = len(refs) - 4
    a_refs, (w_ref, r_ref, gt_ref, o_ref) = refs[:n], refs[n:]
    acc, k0 = None, 0
    for a_ref in a_refs:
        k = a_ref.shape[1]
        part = jnp.dot(a_ref[...], w_ref[k0:k0 + k, :], preferred_element_type=F32)
        acc = part if acc is None else acc + part
        k0 += k
    o_ref[...] = r_ref[...] + gt_ref[0] * acc


def _mm_res(parts, w, res, gate, S, tm=1024, tn=512):
    T = parts[0].shape[0]
    K, N = w.shape
    B = gate.shape[0]
    tm = min(tm, S)
    return pl.pallas_call(
        _mm_res_kernel,
        grid=(T // tm, N // tn),
        in_specs=[pl.BlockSpec((tm, a.shape[1]), lambda i, j: (i, 0)) for a in parts] + [
                  pl.BlockSpec((K, tn), lambda i, j: (0, j)),
                  pl.BlockSpec((tm, tn), lambda i, j: (i, j)),
                  pl.BlockSpec((1, 1, tn), lambda i, j: (i * tm // S, 0, j))],
        out_specs=pl.BlockSpec((tm, tn), lambda i, j: (i, j)),
        out_shape=jax.ShapeDtypeStruct((T, N), F32),
        compiler_params=_cparams(("parallel", "parallel")),
        name="out_proj_residual",
    )(*parts, w, res, gate.reshape(B, 1, N))


def _prep_kernel(p_ref, q_ref, kv_ref, c64_ref, s64_ref, c128_ref, s128_ref,
                 qm_ref, km_ref, vm_ref, sbq_ref, sbk_ref, sbv_ref,
                 dq_ref, dk_ref, dv_ref, iq_ref, ika_ref, ikb_ref, iw_ref):
    c64, s64 = c64_ref[...], s64_ref[...]
    c128, s128 = c128_ref[...], s128_ref[...]
    lane = lax.broadcasted_iota(I32, c64.shape, 1)
    first = (lane % 64) < 32

    def sl(ref, off, width=LANES):
        return ref[:, off:off + width]

    kr = _rope64(sl(p_ref, C_KR), c64, s64, first).astype(BF16)
    for h in range(MLA_HEADS):
        qm_ref[:, 256 * h:256 * h + 128] = sl(q_ref, 256 * h).astype(BF16)
        qm_ref[:, 256 * h + 128:256 * h + 256] = _rope64(sl(q_ref, 256 * h + 128), c64, s64, first).astype(BF16)
        km_ref[:, 256 * h:256 * h + 128] = sl(kv_ref, 128 * h).astype(BF16)
        km_ref[:, 256 * h + 128:256 * h + 256] = kr
    vm_ref[...] = kv_ref[:, MLA_HEADS * 128:].T.astype(BF16)
    sbq_ref[...] = sl(p_ref, C_SBQ, 512).astype(BF16)
    sbk_ref[...] = sl(p_ref, C_SBK, 512).astype(BF16)
    sbv_ref[...] = sl(p_ref, C_SBV, 512).T.astype(BF16)
    for h in range(DSA_HEADS):
        dq_ref[:, 128 * h:128 * h + 128] = _rope128(sl(p_ref, C_DSQ + 128 * h), c128, s128).astype(BF16)
    dk_ref[...] = _rope128(sl(p_ref, C_DSK), c128, s128).astype(BF16)
    dv_ref[...] = sl(p_ref, C_DSV).T.astype(BF16)
    for j in range(IDX_HEADS // 2):
        iq_ref[:, 128 * j:128 * j + 128] = _rope64(sl(p_ref, C_IXQ + 128 * j), c64, s64, first).astype(BF16)
    ik = _rope64(sl(p_ref, C_IXK), c64, s64, first)
    ika_ref[...] = ik.astype(BF16)
    ikb_ref[...] = pltpu.roll(ik, 64, 1).astype(BF16)
    iw_ref[...] = sl(p_ref, C_IXW)


def _prep(proj, q_raw, kv_raw, tabs64, tabs128, tm=256):
    T = proj.shape[0]
    row = lambda w: pl.BlockSpec((tm, w), lambda i: (i, 0))
    widths = [2048, 2048, 1024, 512, 512, 512, 512, 128, 128, 1024, 128, 128]
    transposed = (2, 5, 8)
    out_shape = [jax.ShapeDtypeStruct((w, T) if n in transposed else (T, w), BF16) for n, w in enumerate(widths)]
    out_specs = [pl.BlockSpec((w, tm), lambda i: (0, i)) if n in transposed else row(w) for n, w in enumerate(widths)]
    return pl.pallas_call(
        _prep_kernel,
        grid=(T // tm,),
        in_specs=[row(IN_PAD), row(2048), row(2048), row(LANES), row(LANES), row(LANES), row(LANES)],
        out_specs=out_specs + [row(LANES)],
        out_shape=out_shape + [jax.ShapeDtypeStruct((T, LANES), F32)],
        compiler_params=_cparams(("parallel",)),
        name="mixer_prep",
    )(proj, q_raw, kv_raw, *tabs64, *tabs128)


def _head_out(o, g):
    return _rms(o, g).astype(BF16)


LOG2E = 1.4426950408889634


def _mla_kernel(q_ref, k_ref, vt_ref, g_ref, o_ref, m_s, l_s, acc_s, *, t, nq, scale):
    i = pl.program_id(2)
    hps = m_s.shape[0]
    c = scale * LOG2E
    m_s[...] = jnp.full(m_s.shape, NEG_INF, F32)
    l_s[...] = jnp.zeros(l_s.shape, F32)
    acc_s[...] = jnp.zeros(acc_s.shape, F32)

    def step(cidx, masked):
        k0 = pl.multiple_of(cidx * t, t)
        for hh in range(hps):
            s = _nt_dot(k_ref[pl.ds(k0, t), 256 * hh:256 * hh + 256], q_ref[:, 256 * hh:256 * hh + 256])
            if masked:
                key = lax.broadcasted_iota(I32, (t, t), 0)
                qry = lax.broadcasted_iota(I32, (t, t), 1)
                s = jnp.where(key <= qry, s, NEG_INF)
            m_prev = m_s[hh]
            m_new = jnp.maximum(m_prev, jnp.max(s, axis=0, keepdims=True))
            alpha = jnp.exp2((m_prev - m_new) * c)
            p = jnp.exp2((s - m_new) * c)
            l_s[hh] = alpha * l_s[hh] + jnp.sum(p, axis=0, keepdims=True)
            acc_s[hh] = alpha * acc_s[hh] + jnp.dot(vt_ref[128 * hh:128 * hh + 128, pl.ds(k0, t)], p.astype(BF16),
                                                    preferred_element_type=F32)
            m_s[hh] = m_new

    for n in range(nq):
        @pl.when(i == n)
        def _():
            for cidx in range(n):
                step(cidx, False)
            step(n, True)

    for hh in range(hps):
        o_ref[:, 128 * hh:128 * hh + 128] = _head_out((acc_s[hh] / l_s[hh]).T, g_ref[:, 128 * hh:128 * hh + 128])


def _mla(qm, km, vmt, g_out, B, S, t=1024, hps=2):
    T = qm.shape[0]
    t = min(t, S)
    nq = S // t
    H = MLA_HEADS
    return pl.pallas_call(
        functools.partial(_mla_kernel, t=t, nq=nq, scale=(MLA_NOPE + MLA_ROPE) ** -0.5),
        grid=(B, H // hps, nq),
        in_specs=[pl.BlockSpec((t, 256 * hps), lambda b, h, i: (b * nq + i, h)),
                  pl.BlockSpec((S, 256 * hps), lambda b, h, i: (b, h)),
                  pl.BlockSpec((128 * hps, S), lambda b, h, i: (h, b)),
                  pl.BlockSpec((1, 128 * hps), lambda b, h, i: (0, h))],
        out_specs=pl.BlockSpec((t, 128 * hps), lambda b, h, i: (b * nq + i, h)),
        out_shape=jax.ShapeDtypeStruct((T, H * HEAD_DIM), BF16),
        scratch_shapes=[pltpu.VMEM((hps, 1, t), F32), pltpu.VMEM((hps, 1, t), F32), pltpu.VMEM((hps, 128, t), F32)],
        compiler_params=_cparams(("parallel", "parallel", "arbitrary")),
        name="mla_attention",
    )(qm, km, vmt, g_out)


def _sb_kernel(q_ref, k_ref, vt_ref, g_ref, tri_ref, o_ref, carry_s, acc_s, *, t, nq, scale):
    i = pl.program_id(2)
    hps = carry_s.shape[0]
    tri = tri_ref[...]
    carry_s[...] = jnp.zeros(carry_s.shape, F32)
    acc_s[...] = jnp.zeros(acc_s.shape, F32)

    def step(cidx, masked):
        k0 = pl.multiple_of(cidx * t, t)
        for hh in range(hps):
            z = _nt_dot(k_ref[pl.ds(k0, t), 128 * hh:128 * hh + 128], q_ref[:, 128 * hh:128 * hh + 128]) * scale
            sp = jnp.log(1.0 + jnp.exp(-jnp.abs(z)))
            log_beta = jnp.minimum(z, 0.0) - sp
            log_keep = jnp.minimum(-z, 0.0) - sp
            if masked:
                key = lax.broadcasted_iota(I32, (t, t), 0)
                qry = lax.broadcasted_iota(I32, (t, t), 1)
                strict = key < qry
                log_keep = jnp.where(strict, log_keep, 0.0)
            hi = log_keep.astype(BF16)
            lo = (log_keep - hi.astype(F32)).astype(BF16)
            suffix = jnp.dot(tri, hi, preferred_element_type=F32) + jnp.dot(tri, lo, preferred_element_type=F32)
            a = jnp.exp(log_beta + suffix + carry_s[hh])
            if masked:
                a = jnp.where(strict, a, 0.0)
            acc_s[hh] += jnp.dot(vt_ref[128 * hh:128 * hh + 128, pl.ds(k0, t)], a.astype(BF16),
                                 preferred_element_type=F32)
            carry_s[hh] += jnp.sum(log_keep, axis=0, keepdims=True)

    for n in range(nq):
        @pl.when(i == n)
        def _():
            step(n, True)
            for cidx in range(n - 1, -1, -1):
                step(cidx, False)

    for hh in range(hps):
        o_ref[:, 128 * hh:128 * hh + 128] = _head_out(acc_s[hh].T, g_ref[:, 128 * hh:128 * hh + 128])


def _sb(sbq, sbk, sbvt, g_out, B, S, t=512, hps=2):
    T = sbq.shape[0]
    t = min(t, S)
    nq = S // t
    H = SB_HEADS
    r = np.arange(t)
    tri = jnp.asarray((r[None, :] > r[:, None]).astype(np.float32), BF16)
    g0 = MLA_HEADS // hps
    return pl.pallas_call(
        functools.partial(_sb_kernel, t=t, nq=nq, scale=HEAD_DIM ** -0.5),
        grid=(B, H // hps, nq),
        in_specs=[pl.BlockSpec((t, 128 * hps), lambda b, h, i: (b * nq + i, h)),
                  pl.BlockSpec((S, 128 * hps), lambda b, h, i: (b, h)),
                  pl.BlockSpec((128 * hps, S), lambda b, h, i: (h, b)),
                  pl.BlockSpec((1, 128 * hps), lambda b, h, i: (0, g0 + h)),
                  pl.BlockSpec((t, t), lambda b, h, i: (0, 0))],
        out_specs=pl.BlockSpec((t, 128 * hps), lambda b, h, i: (b * nq + i, h)),
        out_shape=jax.ShapeDtypeStruct((T, H * HEAD_DIM), BF16),
        scratch_shapes=[pltpu.VMEM((hps, 1, t), F32), pltpu.VMEM((hps, 128, t), F32)],
        compiler_params=_cparams(("parallel", "parallel", "arbitrary")),
        name="stickbreak_attention",
    )(sbq, sbk, sbvt, g_out, tri)


def _dsa_kernel(iq_ref, ika_ref, ikb_ref, wt_ref, dq_ref, dk_ref, vt_ref, g_ref, o_ref,
                key_s, m_s, l_s, acc_s, *, tq, tk, n_sel, scale):
    i = pl.program_id(1)
    nch = (i * tq + tq + tk - 1) // tk
    qpos = i * tq + lax.broadcasted_iota(I32, (1, tq), 1)
    kiota = lax.broadcasted_iota(I32, (tk, 1), 0)
    wt = wt_ref[0]
    half = IDX_HEADS // 2

    def score_chunk(c, carry):
        k0 = pl.multiple_of(c * tk, tk)
        kk = jnp.concatenate([ika_ref[pl.ds(k0, tk), :], ikb_ref[pl.ds(k0, tk), :]], axis=0)
        score = jnp.zeros((tk, tq), F32)
        for j in range(half):
            r = jnp.maximum(_nt_dot(kk, iq_ref[:, 128 * j:128 * j + 128]), 0.0)
            score = score + r[:tk] * wt[2 * j:2 * j + 1, :] + r[tk:] * wt[2 * j + 1:2 * j + 2, :]
        score = score * IDX_SCALE
        score = jnp.where(k0 + kiota <= qpos, score, NEG_INF)
        bits = pltpu.bitcast(score, I32)
        key_s[pl.ds(k0, tk), :] = jnp.where(bits < 0, bits ^ jnp.int32(0x7FFFFFFF), bits)
        return carry

    lax.fori_loop(0, nch, score_chunk, 0)

    def count(pred):
        def body(c, acc):
            k0 = pl.multiple_of(c * tk, tk)
            hit = pred(key_s[pl.ds(k0, tk), :], k0 + kiota).astype(I32)
            return acc + jnp.sum(hit.reshape(tk // 8, 8, tq), axis=0)
        part = lax.fori_loop(0, nch, body, jnp.zeros((8, tq), I32))
        return jnp.sum(part, axis=0, keepdims=True)

    c0 = count(lambda k, idx: k >= 0)
    thr = jnp.where(c0 >= n_sel, jnp.int32(0), jnp.int32(INT_MIN))

    def vbit(b, thr):
        cand = thr + jnp.left_shift(jnp.int32(1), 30 - b)
        cnt = count(lambda k, idx: k >= cand)
        return jnp.where(cnt >= n_sel, cand, thr)

    thr = lax.fori_loop(0, 31, vbit, thr)
    n_ge = count(lambda k, idx: k >= thr)
    idx_bits = max(1, int(np.ceil(np.log2(key_s.shape[0]))))

    def tie_bound():
        need = n_sel - count(lambda k, idx: k > thr)

        def ibit(b, bound):
            cand = bound + jnp.left_shift(jnp.int32(1), idx_bits - 1 - b)
            cnt = count(lambda k, idx: (k == thr) & (idx < cand))
            return jnp.where(cnt < need, cand, bound)

        return lax.fori_loop(0, idx_bits, ibit, jnp.zeros((1, tq), I32))

    bound = lax.cond(jnp.max(n_ge) > n_sel, tie_bound, lambda: jnp.full((1, tq), 2 ** idx_bits, I32))

    qc = jnp.concatenate([dq_ref[:, 128 * h:128 * h + 128] for h in range(DSA_HEADS)], axis=0)
    sc2 = scale * LOG2E
    m_s[...] = jnp.full(m_s.shape, NEG_INF, F32)
    l_s[...] = jnp.zeros(l_s.shape, F32)
    acc_s[...] = jnp.zeros(acc_s.shape, F32)

    def attn_chunk(c, carry):
        k0 = pl.multiple_of(c * tk, tk)
        key = key_s[pl.ds(k0, tk), :]
        idx = k0 + kiota
        sel = ((key > thr) | ((key == thr) & (idx <= bound))) & (idx <= qpos)
        bias = jnp.where(sel, 0.0, NEG_INF)
        s = _nt_dot(dk_ref[pl.ds(k0, tk), :], qc) + jnp.concatenate([bias] * DSA_HEADS, axis=1)
        m_prev = m_s[...]
        m_new = jnp.maximum(m_prev, jnp.max(s, axis=0, keepdims=True))
        m_safe = jnp.where(m_new == NEG_INF, 0.0, m_new)
        alpha = jnp.exp2((m_prev - m_safe) * sc2)
        p = jnp.exp2((s - m_safe) * sc2)
        l_s[...] = alpha * l_s[...] + jnp.sum(p, axis=0, keepdims=True)
        acc_s[...] = alpha * acc_s[...] + jnp.dot(vt_ref[:, pl.ds(k0, tk)], p.astype(BF16),
                                                  preferred_element_type=F32)
        m_s[...] = m_new
        return carry

    lax.fori_loop(0, nch, attn_chunk, 0)
    ot = acc_s[...] / l_s[...]
    for h in range(DSA_HEADS):
        o_ref[:, 128 * h:128 * h + 128] = _head_out(ot[:, h * tq:(h + 1) * tq].T, g_ref[:, 128 * h:128 * h + 128])


def _dsa(iq, ika, ikb, wt, dq, dk, vt, g_out, B, S, tq=256, tk=256):
    T = iq.shape[0]
    tk = min(tk, S)
    nq = S // tq
    H = DSA_HEADS
    n_sel = min(TOPK_MAX, S // 4)
    g_c = g_out[:, (MLA_HEADS + SB_HEADS) * HEAD_DIM:]
    return pl.pallas_call(
        functools.partial(_dsa_kernel, tq=tq, tk=tk, n_sel=n_sel, scale=HEAD_DIM ** -0.5),
        grid=(B, nq),
        in_specs=[pl.BlockSpec((tq, IDX_HEADS * IDX_DIM), lambda b, i: (b * nq + i, 0)),
                  pl.BlockSpec((S, 128), lambda b, i: (b, 0)),
                  pl.BlockSpec((S, 128), lambda b, i: (b, 0)),
                  pl.BlockSpec((1, IDX_HEADS, tq), lambda b, i: (b, 0, i)),
                  pl.BlockSpec((tq, H * 128), lambda b, i: (b * nq + i, 0)),
                  pl.BlockSpec((S, 128), lambda b, i: (b, 0)),
                  pl.BlockSpec((128, S), lambda b, i: (0, b)),
                  pl.BlockSpec((1, H * 128), lambda b, i: (0, 0))],
        out_specs=pl.BlockSpec((tq, H * 128), lambda b, i: (b * nq + i, 0)),
        out_shape=jax.ShapeDtypeStruct((T, H * HEAD_DIM), BF16),
        scratch_shapes=[pltpu.VMEM((S, tq), I32), pltpu.VMEM((1, H * tq), F32), pltpu.VMEM((1, H * tq), F32),
                        pltpu.VMEM((128, H * tq), F32)],
        compiler_params=_cparams(("parallel", "arbitrary")),
        name="dsa_attention",
    )(iq, ika, ikb, wt, dq, dk, vt, g_c)


def _route_kernel(x_ref, g_ref, sc_ref, sh_ref, wr_ref, br_ref, tri_ref,
                  h_ref, hp_ref, eidx_ref, gate_ref, pos_ref, cnt_ref, run_s):
    @pl.when(pl.program_id(0) == 0)
    def _():
        run_s[...] = jnp.zeros(run_s.shape, F32)

    h = _rms(x_ref[...], g_ref[...]) * (1.0 + sc_ref[0]) + sh_ref[0]
    hb = h.astype(BF16)
    h_ref[...] = hb
    tm, D = h.shape
    bits = pltpu.bitcast(hb.astype(F32), jnp.uint32)
    hp_ref[...] = (bits[:, :D // 2] >> 16) | (bits[:, D // 2:] & jnp.uint32(0xFFFF0000))
    logits = _nt_dot(wr_ref[...], hb)
    scores = jax.nn.sigmoid(logits)
    biased = scores + br_ref[...]
    gsz = N_EXPERTS // N_GROUPS
    b3 = biased.reshape(N_GROUPS, gsz, tm)
    m1 = jnp.max(b3, axis=1, keepdims=True)
    n1 = jnp.sum((b3 == m1).astype(F32), axis=1, keepdims=True)
    m2 = jnp.max(jnp.where(b3 < m1, b3, NEG_INF), axis=1, keepdims=True)
    grp = (m1 + jnp.where(n1 >= 2.0, m1, m2)).reshape(N_GROUPS, tm)
    gi = lax.broadcasted_iota(I32, (N_GROUPS, 1), 0)
    grank = jnp.zeros((N_GROUPS, tm), F32)
    for g in range(N_GROUPS):
        rowv = grp[g:g + 1, :]
        grank = grank + jnp.where((rowv > grp) | ((rowv == grp) & (g < gi)), 1.0, 0.0)
    gmask = grank.reshape(N_GROUPS, 1, tm) < float(TOPK_GROUPS)
    masked = jnp.where(gmask, b3, NEG_INF).reshape(N_EXPERTS, tm)
    ei = lax.broadcasted_iota(I32, (N_EXPERTS, 1), 0)
    rank = jnp.zeros((N_EXPERTS, tm), F32)
    for e in range(N_EXPERTS):
        rowv = masked[e:e + 1, :]
        rank = rank + jnp.where((rowv > masked) | ((rowv == masked) & (e < ei)), 1.0, 0.0)
    sel = rank < float(TOP_K)
    selm = sel.astype(F32)
    gsum = jnp.sum(scores * selm, axis=0, keepdims=True)
    gate = scores * selm / gsum * ROUTED_SCALE
    within = jnp.dot(selm.astype(BF16), tri_ref[...], preferred_element_type=F32)
    posf = within + run_s[...]
    run_s[...] += jnp.sum(selm, axis=1, keepdims=True)
    cnt_ref[...] = jnp.broadcast_to(run_s[...], cnt_ref.shape)
    eif = ei.astype(F32)
    for k in range(TOP_K):
        onek = rank == float(k)
        eidx_ref[k:k + 1, :] = jnp.sum(jnp.where(onek, eif, 0.0), axis=0, keepdims=True).astype(I32)
        gate_ref[k:k + 1, :] = jnp.sum(jnp.where(onek, gate, 0.0), axis=0, keepdims=True)
        pos_ref[k:k + 1, :] = jnp.sum(jnp.where(onek, posf, 0.0), axis=0, keepdims=True).astype(I32)


def _route(x, g, sc, sh, w_router, b_router, S, tm=512):
    T, D = x.shape
    B = sc.shape[0]
    tm = min(tm, S)
    r = np.arange(tm)
    tri = jnp.asarray((r[:, None] < r[None, :]).astype(np.float32), BF16)
    full = lambda shape: pl.BlockSpec(shape, lambda i: (0,) * len(shape))
    return pl.pallas_call(
        _route_kernel,
        grid=(T // tm,),
        in_specs=[pl.BlockSpec((tm, D), lambda i: (i, 0)),
                  full((1, D)),
                  pl.BlockSpec((1, 1, D), lambda i: (i * tm // S, 0, 0)),
                  pl.BlockSpec((1, 1, D), lambda i: (i * tm // S, 0, 0)),
                  full((N_EXPERTS, D)),
                  full((N_EXPERTS, 1)),
                  full((tm, tm))],
        out_specs=[pl.BlockSpec((tm, D), lambda i: (i, 0)),
                   pl.BlockSpec((tm, D // 2), lambda i: (i, 0)),
                   pl.BlockSpec((TOP_K, tm), lambda i: (0, i)),
                   pl.BlockSpec((TOP_K, tm), lambda i: (0, i)),
                   pl.BlockSpec((TOP_K, tm), lambda i: (0, i)),
                   full((N_EXPERTS, LANES))],
        out_shape=[jax.ShapeDtypeStruct((T, D), BF16),
                   jax.ShapeDtypeStruct((T, D // 2), jnp.uint32),
                   jax.ShapeDtypeStruct((TOP_K, T), I32),
                   jax.ShapeDtypeStruct((TOP_K, T), F32),
                   jax.ShapeDtypeStruct((TOP_K, T), I32),
                   jax.ShapeDtypeStruct((N_EXPERTS, LANES), F32)],
        scratch_shapes=[pltpu.VMEM((N_EXPERTS, 1), F32)],
        compiler_params=_cparams(("arbitrary",)),
        name="ffn_norm_route",
    )(x, g.reshape(1, D), sc.reshape(B, 1, D), sh.reshape(B, 1, D),
      w_router.T.astype(BF16), b_router.reshape(N_EXPERTS, 1), tri)


IDX_ALIGN = 1024
IDX_WIN = 2 * IDX_ALIGN


def _moe_kernel(bexp_ref, bval_ref, bcs_ref, slot_hbm, h_hbm, wg_ref, wu_ref, wd_ref, y_hbm,
                xbuf0, xbuf1, x_s, mid_s, ybuf0, ybuf1, wg_s, wu_s, wd_s, idx_s, isem, gsem, ssem, *, n_tok):
    b = pl.program_id(0)
    nb = pl.num_programs(0)
    rows, D = x_s.shape
    half = D // 2
    DE = mid_s.shape[1]
    YS = D // LANES
    NW = 256
    xbufs, ybufs = (xbuf0, xbuf1), (ybuf0, ybuf1)
    n_real = TOP_K * n_tok

    def idx_copy(blk):
        start = pl.multiple_of((bcs_ref[blk] // IDX_ALIGN) * IDX_ALIGN, IDX_ALIGN)
        return pltpu.make_async_copy(slot_hbm.at[pl.ds(start, IDX_WIN)],
                                     idx_s.at[pl.ds(pl.multiple_of((blk & 3) * IDX_WIN, IDX_WIN), IDX_WIN)],
                                     isem.at[blk & 3])

    def block_rows(blk):
        inside = (blk >= 0) & (blk < nb)
        safe = jnp.clip(blk, 0, nb - 1)
        return (blk & 3) * IDX_WIN + bcs_ref[safe] % IDX_ALIGN, jnp.where(inside, bval_ref[safe], 0)

    def row_id(blk, base, valid, r):
        return jnp.where(r < valid, idx_s[base + r], n_real + (blk & 1) * rows + r)

    def gather_start(blk, base, valid, r, p):
        tok = row_id(blk, base, valid, r) & (n_tok - 1)
        pltpu.make_async_copy(h_hbm.at[pl.ds(pl.multiple_of(tok * 8, 8), 8), :],
                              xbufs[p].at[pl.ds(r * 8, 8), :], gsem.at[p]).start(priority=r % 2)

    def scatter_start(blk, base, valid, r, p):
        dst = row_id(blk, base, valid, r)
        pltpu.make_async_copy(ybufs[p].at[pl.ds(r * YS, YS), :],
                              y_hbm.at[pl.ds(pl.multiple_of(dst * YS, YS), YS), :], ssem.at[p]).start(priority=r % 2)

    def wait_gather(p):
        pltpu.make_async_copy(xbufs[p], xbufs[p], gsem.at[p]).wait()

    def wait_scatter(p):
        pltpu.make_async_copy(ybufs[p], ybufs[p], ssem.at[p]).wait()

    def compute_pieces(p):
        def unpack():
            for s in range(8):
                w = xbufs[p][pl.ds(s, rows, stride=8), :]
                x_s[:, 128 * s:128 * s + 128] = pltpu.bitcast(w << 16, F32).astype(BF16)
                x_s[:, half + 128 * s:half + 128 * s + 128] = pltpu.bitcast(w & jnp.uint32(0xFFFF0000), F32).astype(BF16)

        def gate_up(j):
            def piece():
                x = x_s[...]
                gt = jnp.dot(x, wg_s[:, NW * j:NW * j + NW], preferred_element_type=F32)
                up = jnp.dot(x, wu_s[:, NW * j:NW * j + NW], preferred_element_type=F32)
                mid_s[:, NW * j:NW * j + NW] = (gt * jax.nn.sigmoid(gt) * up).astype(BF16)
            return piece

        def down(n):
            def piece():
                y = jnp.dot(mid_s[...], wd_s[:, NW * n:NW * n + NW], preferred_element_type=F32)
                for s in range(NW // LANES):
                    ybufs[p][pl.ds(n * (NW // LANES) + s, rows, stride=YS), :] = y[:, LANES * s:LANES * s + LANES]
            return piece
        return [unpack] + [gate_up(j) for j in range(DE // NW)], [down(n) for n in range(D // NW)]

    def interleave(pieces, starts):
        n_slots = max(len(pieces), 1)
        per = -(-len(starts) // n_slots)
        for n in range(n_slots):
            if pieces:
                pieces[n]()
            for st in starts[n * per:(n + 1) * per]:
                st()

    def run_block(p, compute):
        first, second = compute_pieces(p) if compute else ([], [])
        gbase, gvalid = block_rows(b + 1)
        interleave(first, [functools.partial(gather_start, b + 1, gbase, gvalid, r, 1 - p) for r in range(rows)])
        wait_scatter(p)
        sbase, svalid = block_rows(b - 1)
        interleave(second, [functools.partial(scatter_start, b - 1, sbase, svalid, r, 1 - p) for r in range(rows)])

    @pl.when(b == 0)
    def _():
        ybuf0[...] = jnp.zeros(ybuf0.shape, F32)
        ybuf1[...] = jnp.zeros(ybuf1.shape, F32)
        pltpu.make_async_copy(ybuf0, y_hbm.at[pl.ds(n_real * YS, rows * YS), :], ssem.at[0]).start()
        idx_copy(0).start()
        idx_copy(0).wait()
        base0, valid0 = block_rows(b)
        for r in range(rows):
            gather_start(b, base0, valid0, r, 0)
        idx_copy(1).start()

    @pl.when(b + 2 < nb)
    def _():
        idx_copy(b + 2).start()

    @pl.when(b + 1 < nb)
    def _():
        idx_copy(b + 1).wait()

    @pl.when((b == 0) | (bexp_ref[b] != bexp_ref[jnp.maximum(b - 1, 0)]))
    def _():
        wg_s[...] = wg_ref[0, 0].astype(BF16)
        wu_s[...] = wu_ref[0, 0].astype(BF16)
        wd_s[...] = wd_ref[0, 0].astype(BF16)

    active = bval_ref[b] > 0
    for p in range(2):
        @pl.when((b & 1) == p)
        def _():
            wait_gather(p)

            @pl.when(active)
            def _():
                run_block(p, True)

            @pl.when(jnp.logical_not(active))
            def _():
                run_block(p, False)

            @pl.when(b == nb - 1)
            def _():
                sbase, svalid = block_rows(b)
                for r in range(rows):
                    scatter_start(b, sbase, svalid, r, p)
                wait_scatter(p)
                wait_scatter(1 - p)
                wait_gather(1 - p)


def _moe(hp, slot_ids, blk_exp, blk_val, blk_cs, w_gate, w_up, w_down, l):
    T = hp.shape[0] // 8
    D = w_gate.shape[2]
    assert T & (T - 1) == 0 and D == 2 * 8 * LANES
    DE = w_gate.shape[3]
    nb = blk_exp.shape[0]
    wspec = lambda shape: pl.BlockSpec((1, 1) + shape, lambda b, bexp, bval, bcs: (l, bexp[b], 0, 0))
    grid_spec = pltpu.PrefetchScalarGridSpec(
        num_scalar_prefetch=3,
        grid=(nb,),
        in_specs=[pl.BlockSpec(memory_space=pl.ANY), pl.BlockSpec(memory_space=pl.ANY),
                  wspec((D, DE)), wspec((D, DE)), wspec((DE, D))],
        out_specs=pl.BlockSpec(memory_space=pl.ANY),
        scratch_shapes=[pltpu.VMEM((MOE_ROWS * 8, LANES), jnp.uint32), pltpu.VMEM((MOE_ROWS * 8, LANES), jnp.uint32),
                        pltpu.VMEM((MOE_ROWS, D), BF16), pltpu.VMEM((MOE_ROWS, DE), BF16),
                        pltpu.VMEM((MOE_ROWS * D // LANES, LANES), F32), pltpu.VMEM((MOE_ROWS * D // LANES, LANES), F32),
                        pltpu.VMEM((D, DE), BF16), pltpu.VMEM((D, DE), BF16), pltpu.VMEM((DE, D), BF16),
                        pltpu.SMEM((4 * IDX_WIN,), I32),
                        pltpu.SemaphoreType.DMA((4,)), pltpu.SemaphoreType.DMA((2,)), pltpu.SemaphoreType.DMA((2,))],
    )
    return pl.pallas_call(
        functools.partial(_moe_kernel, n_tok=T),
        grid_spec=grid_spec,
        out_shape=jax.ShapeDtypeStruct(((TOP_K * T + 2 * MOE_ROWS) * D // LANES, LANES), F32),
        compiler_params=_cparams(("arbitrary",)),
        name="moe_experts",
    )(blk_exp, blk_val, blk_cs, slot_ids, hp, w_gate, w_up, w_down)


def _ffn_out_kernel(*refs, final):
    y_refs = refs[:TOP_K]
    gk_ref, h_ref, x_ref, gf_ref, wsg_ref, wsu_ref, wsd_ref, fg_ref, o_ref, r_s = refs[TOP_K:]
    tm, D = x_ref.shape
    ys = D // LANES
    acc = None
    for k in range(TOP_K):
        gk = jnp.broadcast_to(gk_ref[k][:, None, :], (tm, ys, LANES)).reshape(tm * ys, LANES)
        term = y_refs[k][...] * gk
        acc = term if acc is None else acc + term
    r_s[...] = acc
    routed = jnp.concatenate([r_s[pl.ds(s, tm, stride=ys), :] for s in range(ys)], axis=1)
    hb = h_ref[...]
    gt = jnp.dot(hb, wsg_ref[...], preferred_element_type=F32)
    up = jnp.dot(hb, wsu_ref[...], preferred_element_type=F32)
    mid = (gt * jax.nn.sigmoid(gt) * up).astype(BF16)
    shared = jnp.dot(mid, wsd_ref[...], preferred_element_type=F32)
    out = x_ref[...] + gf_ref[0] * (routed + shared)
    if final:
        out = _rms(out, fg_ref[...])
    o_ref[...] = out


def _ffn_out(y, gate3, h, x, g_f, wsg, wsu, wsd, final_g, final, S, tm=128):
    T, D = x.shape
    B = g_f.shape[0]
    DS = wsg.shape[1]
    tm = min(tm, S)
    nt = T // tm
    full = lambda shape: pl.BlockSpec(shape, lambda i: (0,) * len(shape))
    y_specs = [pl.BlockSpec((tm * D // LANES, LANES), functools.partial(lambda i, k: (k * nt + i, 0), k=k))
               for k in range(TOP_K)]
    return pl.pallas_call(
        functools.partial(_ffn_out_kernel, final=final),
        grid=(nt,),
        in_specs=y_specs + [
                  pl.BlockSpec((TOP_K, tm, 1), lambda i: (0, i, 0)),
                  pl.BlockSpec((tm, D), lambda i: (i, 0)),
                  pl.BlockSpec((tm, D), lambda i: (i, 0)),
                  pl.BlockSpec((1, 1, D), lambda i: (i * tm // S, 0, 0)),
                  full((D, DS)), full((D, DS)), full((DS, D)), full((1, D))],
        out_specs=pl.BlockSpec((tm, D), lambda i: (i, 0)),
        out_shape=jax.ShapeDtypeStruct((T, D), F32),
        scratch_shapes=[pltpu.VMEM((tm * D // LANES, LANES), F32)],
        compiler_params=_cparams(("parallel",)),
        name="ffn_combine",
    )(*([y] * TOP_K), gate3, h, x, g_f.reshape(B, 1, D), wsg, wsu, wsd, final_g.reshape(1, D))


def _layout_w_in(w):
    D = w.shape[0]
    sizes = (Q_LORA, KV_LORA, MLA_ROPE, 512, 512, 512, 512, HEAD_DIM, HEAD_DIM, IDX_HEADS * IDX_DIM, IDX_DIM, IDX_HEADS)
    offs = np.concatenate([[0], np.cumsum(sizes)])
    cq, ckv, kr, sbq, sbk, sbv, dsq, dsk, dsv, ixq, ixk, ixw = [w[:, offs[n]:offs[n + 1]] for n in range(12)]
    z = lambda n: jnp.zeros((D, n), w.dtype)
    out = jnp.concatenate([cq, sbq, sbk, sbv, dsq, ckv, dsk, dsv, ixq, kr, z(64), ixk, z(64), ixw, z(112), z(128)], axis=1)
    assert out.shape[1] == IN_PAD
    return out.astype(BF16)


def _layout_w_uq(w):
    K = w.shape[0]
    w3 = w.reshape(K, MLA_HEADS, MLA_NOPE + MLA_ROPE)
    w3 = jnp.concatenate([w3, jnp.zeros((K, MLA_HEADS, 64), w.dtype)], axis=2)
    return w3.reshape(K, MLA_HEADS * 256).astype(BF16)


def _layout_w_ukv(w):
    K = w.shape[0]
    w3 = w.reshape(K, MLA_HEADS, MLA_NOPE + HEAD_DIM)
    return jnp.concatenate([w3[:, :, :MLA_NOPE].reshape(K, -1), w3[:, :, MLA_NOPE:].reshape(K, -1)], axis=1).astype(BF16)


def _expert_blocks(eidx, pos, counts, T):
    M = T * TOP_K
    nb = M // MOE_ROWS + N_EXPERTS
    padded = (counts + MOE_ROWS - 1) // MOE_ROWS * MOE_ROWS
    pend = jnp.cumsum(padded)
    pstart = pend - padded
    cstart = jnp.cumsum(counts) - counts
    experts = jnp.arange(N_EXPERTS, dtype=I32)
    rank = jnp.sum(jnp.where(eidx[:, :, None] == experts, cstart, 0), axis=-1) + pos
    out_row = jnp.arange(TOP_K, dtype=I32)[:, None] * T + jnp.arange(T, dtype=I32)[None, :]
    _, slot_ids = lax.sort_key_val(rank.reshape(-1), out_row.reshape(-1))
    slot_ids = jnp.concatenate([slot_ids, jnp.zeros((IDX_WIN,), I32)])
    blk_start = jnp.arange(nb, dtype=I32) * MOE_ROWS
    blk_exp = jnp.minimum(jnp.sum((pend[None, :] <= blk_start[:, None]).astype(I32), axis=1), N_EXPERTS - 1)
    pick = lambda v: jnp.sum(jnp.where(blk_exp[:, None] == experts, v, 0), axis=-1)
    blk_val = jnp.clip(pick(pstart + counts) - blk_start, 0, MOE_ROWS).astype(I32)
    blk_cs = jnp.where(blk_val > 0, pick(cstart - pstart) + blk_start, 0).astype(I32)
    return slot_ids, blk_exp, blk_val, blk_cs


def _mixers(x, S, B, positions_tabs, mod, l, w):
    sh_m, sc_m, g_m = mod[0], mod[1], mod[2]
    proj = _nmm(x, 0, x.shape[1], w["norm_mix_g"][l], w["w_in"][l], S, sc=sc_m, sh=sh_m)
    q_raw = _nmm(proj, C_CQ // Q_LORA, Q_LORA, w["g_cq"][l], w["w_uq"][l], S)
    kv_raw = _nmm(proj, C_CKV // KV_LORA, KV_LORA, w["g_ckv"][l], w["w_ukv"][l], S)
    tabs64, tabs128 = positions_tabs
    (qm, km, vmt, sbq, sbk, sbvt, dq, dk, dvt, iq, ika, ikb, iw) = _prep(proj, q_raw, kv_raw, tabs64, tabs128)
    g_out = w["g_out"][l].reshape(1, -1)
    o_a = _mla(qm, km, vmt, g_out, B, S)
    o_b = _sb(sbq, sbk, sbvt, g_out, B, S)
    wt = iw[:, :IDX_HEADS].reshape(B, S, IDX_HEADS).transpose(0, 2, 1)
    o_c = _dsa(iq, ika, ikb, wt, dq, dk, dvt, g_out, B, S)
    return _mm_res((o_a, o_b, o_c), w["w_o"][l], x, g_m, S)


def _ffn(x, S, B, mod, l, w, final_g, final):
    T, D = x.shape
    sh_f, sc_f, g_f = mod[3], mod[4], mod[5]
    h, hp, eidx, gate, pos, cnt = _route(x, w["norm_ffn_g"][l], sc_f, sh_f, w["w_router"][l], w["b_router"][l], S)
    counts = cnt[:, 0].astype(I32)
    slot_ids, blk_exp, blk_val, blk_cs = _expert_blocks(eidx, pos, counts, T)
    y = _moe(hp.reshape(T * 8, LANES), slot_ids, blk_exp, blk_val, blk_cs, w["w_gate"], w["w_up"], w["w_down"], l)
    return _ffn_out(y, gate.reshape(TOP_K, T, 1), h, x, g_f, w["ws_gate"][l], w["ws_up"][l], w["ws_down"][l],
                    final_g, final, S)


def kernel(x, c, positions, norm_mix_g, norm_ffn_g, w_ada, b_ada, w_in, g_cq, g_ckv, w_uq, w_ukv, g_out, w_o,
           w_router, b_router, w_gate, w_up, w_down, ws_gate, ws_up, ws_down, final_g):
    B, S, D = x.shape
    L = w_in.shape[0]
    T = B * S
    w = dict(norm_mix_g=norm_mix_g, norm_ffn_g=norm_ffn_g, g_cq=g_cq, g_ckv=g_ckv, g_out=g_out,
             w_in=jax.vmap(_layout_w_in)(w_in), w_uq=jax.vmap(_layout_w_uq)(w_uq), w_ukv=jax.vmap(_layout_w_ukv)(w_ukv),
             w_o=w_o.astype(BF16), w_router=w_router, b_router=b_router, w_gate=w_gate, w_up=w_up, w_down=w_down,
             ws_gate=ws_gate.astype(BF16), ws_up=ws_up.astype(BF16), ws_down=ws_down.astype(BF16))
    mod_all = _ada(c, w_ada, b_ada).reshape(L, B, 6, D)
    posf = positions.astype(F32).reshape(T, 1)
    tabs = (_rope_tables(posf, 64), _rope_tables(posf, 128))
    xt = x.reshape(T, D)
    for l in range(L):
        mod = [mod_all[l, :, n, :] for n in range(6)]
        xt = _mixers(xt, S, B, tabs, mod, l, w)
        xt = _ffn(xt, S, B, mod, l, w, final_g, l == L - 1)
    return xt.reshape(B, S, D)
```

```python
import functools

import jax
import jax.numpy as jnp
import numpy as np
from jax import lax
from jax.experimental import pallas as pl
from jax.experimental.pallas import tpu as pltpu

F32 = jnp.float32
BF16 = jnp.bfloat16
I32 = jnp.int32

HEAD_DIM = 128
MLA_HEADS = 8
SB_HEADS = 4
DSA_HEADS = 4
N_MIX_HEADS = MLA_HEADS + SB_HEADS + DSA_HEADS
Q_LORA = 512
KV_LORA = 256
MLA_NOPE = 128
MLA_ROPE = 64
IDX_HEADS = 16
IDX_DIM = 64
IDX_SCALE = (IDX_HEADS * IDX_DIM) ** -0.5
TOPK_MAX = 256
ROPE_THETA = 10000.0
N_EXPERTS = 64
TOP_K = 8
N_GROUPS = 8
TOPK_GROUPS = 4
ROUTED_SCALE = 2.5
EPS = 1e-6

LANES = 128
VMEM_LIMIT = 56 * 1024 * 1024
NEG_INF = float("-inf")
INT_MIN = -(2 ** 31)

C_CQ, C_SBQ, C_SBK, C_SBV, C_DSQ = 0, 512, 1024, 1536, 2048
C_CKV, C_DSK, C_DSV, C_IXQ = 2560, 2816, 2944, 3072
C_KR, C_IXK, C_IXW, IN_PAD = 4096, 4224, 4352, 4608

MOE_ROWS = 256


def _cparams(sem):
    return pltpu.CompilerParams(dimension_semantics=sem, vmem_limit_bytes=VMEM_LIMIT)


def _nt_dot(a, b):
    return lax.dot_general(a, b, (((1,), (1,)), ((), ())), preferred_element_type=F32)


def _rms(x, g):
    return x * lax.rsqrt(jnp.mean(x * x, axis=-1, keepdims=True) + EPS) * g


def _ada_kernel(c_ref, w_ref, b_ref, o_ref):
    c = c_ref[...]
    a = (c * jax.nn.sigmoid(c)).astype(BF16)
    o_ref[0] = jnp.dot(a, w_ref[0].astype(BF16), preferred_element_type=F32) + b_ref[0]


def _ada(c, w_ada, b_ada):
    L, D, N = w_ada.shape
    B = c.shape[0]
    tn = 1024
    return pl.pallas_call(
        _ada_kernel,
        grid=(L, N // tn),
        in_specs=[pl.BlockSpec((B, D), lambda l, j: (0, 0)),
                  pl.BlockSpec((1, D, tn), lambda l, j: (l, 0, j)),
                  pl.BlockSpec((1, 1, tn), lambda l, j: (l, 0, j))],
        out_specs=pl.BlockSpec((1, B, tn), lambda l, j: (l, 0, j)),
        out_shape=jax.ShapeDtypeStruct((L, B, N), F32),
        compiler_params=_cparams(("parallel", "parallel")),
        name="ada_mod",
    )(c, w_ada, b_ada.reshape(L, 1, N))


def _rope_tab_kernel(pos_ref, inv_ref, sgn_ref, cos_ref, sin_ref):
    ang = pos_ref[...] * inv_ref[...]
    cos_ref[...] = jnp.cos(ang)
    sin_ref[...] = jnp.sin(ang) * sgn_ref[...]


def _rope_tables(posf, d):
    T = posf.shape[0]
    half = d // 2
    inv = ROPE_THETA ** (-jnp.arange(0, d, 2, dtype=F32) / d)
    reps = LANES // half
    inv_t = jnp.tile(inv, reps).reshape(1, LANES)
    sgn = jnp.tile(jnp.concatenate([-jnp.ones((half,), F32), jnp.ones((half,), F32)]), LANES // d).reshape(1, LANES)
    tm = min(T, 1024)
    return pl.pallas_call(
        _rope_tab_kernel,
        grid=(T // tm,),
        in_specs=[pl.BlockSpec((tm, 1), lambda i: (i, 0)),
                  pl.BlockSpec((1, LANES), lambda i: (0, 0)),
                  pl.BlockSpec((1, LANES), lambda i: (0, 0))],
        out_specs=[pl.BlockSpec((tm, LANES), lambda i: (i, 0))] * 2,
        out_shape=[jax.ShapeDtypeStruct((T, LANES), F32)] * 2,
        compiler_params=_cparams(("parallel",)),
        name=f"rope_tab{d}",
    )(posf, inv_t, sgn)


def _rope64(x, cos, sin_s, first_half):
    rot = jnp.where(first_half, pltpu.roll(x, 96, 1), pltpu.roll(x, 32, 1))
    return x * cos + rot * sin_s


def _rope128(x, cos, sin_s):
    return x * cos + pltpu.roll(x, 64, 1) * sin_s


def _nmm_kernel(*refs, modulate):
    if modulate:
        x_ref, g_ref, sc_ref, sh_ref, w_ref, o_ref, h_s = refs
    else:
        x_ref, g_ref, w_ref, o_ref, h_s = refs

    @pl.when(pl.program_id(1) == 0)
    def _():
        h = _rms(x_ref[...], g_ref[...])
        if modulate:
            h = h * (1.0 + sc_ref[0]) + sh_ref[0]
        h_s[...] = h.astype(BF16)

    o_ref[...] = jnp.dot(h_s[...], w_ref[...], preferred_element_type=F32).astype(o_ref.dtype)


def _nmm(x, col_blk, K, g, w, S, sc=None, sh=None, tm=1024, tn=512, out_dtype=F32):
    T = x.shape[0]
    N = w.shape[1]
    tm = min(tm, S)
    tn = min(tn, N)
    modulate = sc is not None
    in_specs = [pl.BlockSpec((tm, K), lambda i, j: (i, col_blk)),
                pl.BlockSpec((1, K), lambda i, j: (0, 0))]
    args = [x, g.reshape(1, K)]
    if modulate:
        B = sc.shape[0]
        in_specs += [pl.BlockSpec((1, 1, K), lambda i, j: (i * tm // S, 0, 0))] * 2
        args += [sc.reshape(B, 1, K), sh.reshape(B, 1, K)]
    in_specs.append(pl.BlockSpec((K, tn), lambda i, j: (0, j)))
    args.append(w)
    return pl.pallas_call(
        functools.partial(_nmm_kernel, modulate=modulate),
        grid=(T // tm, N // tn),
        in_specs=in_specs,
        out_specs=pl.BlockSpec((tm, tn), lambda i, j: (i, j)),
        out_shape=jax.ShapeDtypeStruct((T, N), out_dtype),
        scratch_shapes=[pltpu.VMEM((tm, K), BF16)],
        compiler_params=_cparams(("parallel", "arbitrary")),
        name="norm_matmul",
    )(*args)


def _mm_res_kernel(*refs):
    n = len(refs) - 4
    a_refs, (w_ref, r_ref, gt_ref, o_ref) = refs[:n], refs[n:]
    acc, k0 = None, 0
    for a_ref in a_refs:
        k = a_ref.shape[1]
        part = jnp.dot(a_ref[...], w_ref[k0:k0 + k, :], preferred_element_type=F32)
        acc = part if acc is None else acc + part
        k0 += k
    o_ref[...] = r_ref[...] + gt_ref[0] * acc


def _mm_res(parts, w, res, gate, S, tm=1024, tn=512):
    T = parts[0].shape[0]
    K, N = w.shape
    B = gate.shape[0]
    tm = min(tm, S)
    return pl.pallas_call(
        _mm_res_kernel,
        grid=(T // tm, N // tn),
        in_specs=[pl.BlockSpec((tm, a.shape[1]), lambda i, j: (i, 0)) for a in parts] + [
                  pl.BlockSpec((K, tn), lambda i, j: (0, j)),
                  pl.BlockSpec((tm, tn), lambda i, j: (i, j)),
                  pl.BlockSpec((1, 1, tn), lambda i, j: (i * tm // S, 0, j))],
        out_specs=pl.BlockSpec((tm, tn), lambda i, j: (i, j)),
        out_shape=jax.ShapeDtypeStruct((T, N), F32),
        compiler_params=_cparams(("parallel", "parallel")),
        name="out_proj_residual",
    )(*parts, w, res, gate.reshape(B, 1, N))


def _prep_kernel(p_ref, q_ref, kv_ref, c64_ref, s64_ref, c128_ref, s128_ref,
                 qm_ref, km_ref, vm_ref, sbq_ref, sbk_ref, sbv_ref,
                 dq_ref, dk_ref, dv_ref, iq_ref, ika_ref, ikb_ref, iw_ref):
    c64, s64 = c64_ref[...], s64_ref[...]
    c128, s128 = c128_ref[...], s128_ref[...]
    lane = lax.broadcasted_iota(I32, c64.shape, 1)
    first = (lane % 64) < 32

    def sl(ref, off, width=LANES):
        return ref[:, off:off + width]

    kr = _rope64(sl(p_ref, C_KR), c64, s64, first).astype(BF16)
    for h in range(MLA_HEADS):
        qm_ref[:, 256 * h:256 * h + 128] = sl(q_ref, 256 * h).astype(BF16)
        qm_ref[:, 256 * h + 128:256 * h + 256] = _rope64(sl(q_ref, 256 * h + 128), c64, s64, first).astype(BF16)
        km_ref[:, 256 * h:256 * h + 128] = sl(kv_ref, 128 * h).astype(BF16)
        km_ref[:, 256 * h + 128:256 * h + 256] = kr
    vm_ref[...] = kv_ref[:, MLA_HEADS * 128:].T.astype(BF16)
    sbq_ref[...] = sl(p_ref, C_SBQ, 512).astype(BF16)
    sbk_ref[...] = sl(p_ref, C_SBK, 512).astype(BF16)
    sbv_ref[...] = sl(p_ref, C_SBV, 512).T.astype(BF16)
    for h in range(DSA_HEADS):
        dq_ref[:, 128 * h:128 * h + 128] = _rope128(sl(p_ref, C_DSQ + 128 * h), c128, s128).astype(BF16)
    dk_ref[...] = _rope128(sl(p_ref, C_DSK), c128, s128).astype(BF16)
    dv_ref[...] = sl(p_ref, C_DSV).T.astype(BF16)
    for j in range(IDX_HEADS // 2):
        iq_ref[:, 128 * j:128 * j + 128] = _rope64(sl(p_ref, C_IXQ + 128 * j), c64, s64, first).astype(BF16)
    ik = _rope64(sl(p_ref, C_IXK), c64, s64, first)
    ika_ref[...] = ik.astype(BF16)
    ikb_ref[...] = pltpu.roll(ik, 64, 1).astype(BF16)
    iw_ref[...] = sl(p_ref, C_IXW)


def _prep(proj, q_raw, kv_raw, tabs64, tabs128, tm=256):
    T = proj.shape[0]
    row = lambda w: pl.BlockSpec((tm, w), lambda i: (i, 0))
    widths = [2048, 2048, 1024, 512, 512, 512, 512, 128, 128, 1024, 128, 128]
    transposed = (2, 5, 8)
    out_shape = [jax.ShapeDtypeStruct((w, T) if n in transposed else (T, w), BF16) for n, w in enumerate(widths)]
    out_specs = [pl.BlockSpec((w, tm), lambda i: (0, i)) if n in transposed else row(w) for n, w in enumerate(widths)]
    return pl.pallas_call(
        _prep_kernel,
        grid=(T // tm,),
        in_specs=[row(IN_PAD), row(2048), row(2048), row(LANES), row(LANES), row(LANES), row(LANES)],
        out_specs=out_specs + [row(LANES)],
        out_shape=out_shape + [jax.ShapeDtypeStruct((T, LANES), F32)],
        compiler_params=_cparams(("parallel",)),
        name="mixer_prep",
    )(proj, q_raw, kv_raw, *tabs64, *tabs128)


def _head_out(o, g):
    return _rms(o, g).astype(BF16)


LOG2E = 1.4426950408889634


def _mla_kernel(q_ref, k_ref, vt_ref, g_ref, o_ref, m_s, l_s, acc_s, *, t, nq, scale):
    i = pl.program_id(2)
    hps = m_s.shape[0]
    c = scale * LOG2E
    m_s[...] = jnp.full(m_s.shape, NEG_INF, F32)
    l_s[...] = jnp.zeros(l_s.shape, F32)
    acc_s[...] = jnp.zeros(acc_s.shape, F32)

    def step(cidx, masked):
        k0 = pl.multiple_of(cidx * t, t)
        for hh in range(hps):
            s = _nt_dot(k_ref[pl.ds(k0, t), 256 * hh:256 * hh + 256], q_ref[:, 256 * hh:256 * hh + 256])
            if masked:
                key = lax.broadcasted_iota(I32, (t, t), 0)
                qry = lax.broadcasted_iota(I32, (t, t), 1)
                s = jnp.where(key <= qry, s, NEG_INF)
            m_prev = m_s[hh]
            m_new = jnp.maximum(m_prev, jnp.max(s, axis=0, keepdims=True))
            alpha = jnp.exp2((m_prev - m_new) * c)
            p = jnp.exp2((s - m_new) * c)
            l_s[hh] = alpha * l_s[hh] + jnp.sum(p, axis=0, keepdims=True)
            acc_s[hh] = alpha * acc_s[hh] + jnp.dot(vt_ref[128 * hh:128 * hh + 128, pl.ds(k0, t)], p.astype(BF16),
                                                    preferred_element_type=F32)
            m_s[hh] = m_new

    for n in range(nq):
        @pl.when(i == n)
        def _():
            for cidx in range(n):
                step(cidx, False)
            step(n, True)

    for hh in range(hps):
        o_ref[:, 128 * hh:128 * hh + 128] = _head_out((acc_s[hh] / l_s[hh]).T, g_ref[:, 128 * hh:128 * hh + 128])


def _mla(qm, km, vmt, g_out, B, S, t=1024, hps=2):
    T = qm.shape[0]
    t = min(t, S)
    nq = S // t
    H = MLA_HEADS
    return pl.pallas_call(
        functools.partial(_mla_kernel, t=t, nq=nq, scale=(MLA_NOPE + MLA_ROPE) ** -0.5),
        grid=(B, H // hps, nq),
        in_specs=[pl.BlockSpec((t, 256 * hps), lambda b, h, i: (b * nq + i, h)),
                  pl.BlockSpec((S, 256 * hps), lambda b, h, i: (b, h)),
                  pl.BlockSpec((128 * hps, S), lambda b, h, i: (h, b)),
                  pl.BlockSpec((1, 128 * hps), lambda b, h, i: (0, h))],
        out_specs=pl.BlockSpec((t, 128 * hps), lambda b, h, i: (b * nq + i, h)),
        out_shape=jax.ShapeDtypeStruct((T, H * HEAD_DIM), BF16),
        scratch_shapes=[pltpu.VMEM((hps, 1, t), F32), pltpu.VMEM((hps, 1, t), F32), pltpu.VMEM((hps, 128, t), F32)],
        compiler_params=_cparams(("parallel", "parallel", "arbitrary")),
        name="mla_attention",
    )(qm, km, vmt, g_out)


def _sb_kernel(q_ref, k_ref, vt_ref, g_ref, tri_ref, o_ref, carry_s, acc_s, *, t, nq, scale):
    i = pl.program_id(2)
    hps = carry_s.shape[0]
    tri = tri_ref[...]
    carry_s[...] = jnp.zeros(carry_s.shape, F32)
    acc_s[...] = jnp.zeros(acc_s.shape, F32)

    def step(cidx, masked):
        k0 = pl.multiple_of(cidx * t, t)
        for hh in range(hps):
            z = _nt_dot(k_ref[pl.ds(k0, t), 128 * hh:128 * hh + 128], q_ref[:, 128 * hh:128 * hh + 128]) * scale
            sp = jnp.log(1.0 + jnp.exp(-jnp.abs(z)))
            log_beta = jnp.minimum(z, 0.0) - sp
            log_keep = jnp.minimum(-z, 0.0) - sp
            if masked:
                key = lax.broadcasted_iota(I32, (t, t), 0)
                qry = lax.broadcasted_iota(I32, (t, t), 1)
                strict = key < qry
                log_keep = jnp.where(strict, log_keep, 0.0)
            hi = log_keep.astype(BF16)
            lo = (log_keep - hi.astype(F32)).astype(BF16)
            suffix = jnp.dot(tri, hi, preferred_element_type=F32) + jnp.dot(tri, lo, preferred_element_type=F32)
            a = jnp.exp(log_beta + suffix + carry_s[hh])
            if masked:
                a = jnp.where(strict, a, 0.0)
            acc_s[hh] += jnp.dot(vt_ref[128 * hh:128 * hh + 128, pl.ds(k0, t)], a.astype(BF16),
                                 preferred_element_type=F32)
            carry_s[hh] += jnp.sum(log_keep, axis=0, keepdims=True)

    for n in range(nq):
        @pl.when(i == n)
        def _():
            step(n, True)
            for cidx in range(n - 1, -1, -1):
                step(cidx, False)

    for hh in range(hps):
        o_ref[:, 128 * hh:128 * hh + 128] = _head_out(acc_s[hh].T, g_ref[:, 128 * hh:128 * hh + 128])


def _sb(sbq, sbk, sbvt, g_out, B, S, t=512, hps=2):
    T = sbq.shape[0]
    t = min(t, S)
    nq = S // t
    H = SB_HEADS
    r = np.arange(t)
    tri = jnp.asarray((r[None, :] > r[:, None]).astype(np.float32), BF16)
    g0 = MLA_HEADS // hps
    return pl.pallas_call(
        functools.partial(_sb_kernel, t=t, nq=nq, scale=HEAD_DIM ** -0.5),
        grid=(B, H // hps, nq),
        in_specs=[pl.BlockSpec((t, 128 * hps), lambda b, h, i: (b * nq + i, h)),
                  pl.BlockSpec((S, 128 * hps), lambda b, h, i: (b, h)),
                  pl.BlockSpec((128 * hps, S), lambda b, h, i: (h, b)),
                  pl.BlockSpec((1, 128 * hps), lambda b, h, i: (0, g0 + h)),
                  pl.BlockSpec((t, t), lambda b, h, i: (0, 0))],
        out_specs=pl.BlockSpec((t, 128 * hps), lambda b, h, i: (b * nq + i, h)),
        out_shape=jax.ShapeDtypeStruct((T, H * HEAD_DIM), BF16),
        scratch_shapes=[pltpu.VMEM((hps, 1, t), F32), pltpu.VMEM((hps, 128, t), F32)],
        compiler_params=_cparams(("parallel", "parallel", "arbitrary")),
        name="stickbreak_attention",
    )(sbq, sbk, sbvt, g_out, tri)


def _dsa_kernel(iq_ref, ika_ref, ikb_ref, wt_ref, dq_ref, dk_ref, vt_ref, g_ref, o_ref,
                key_s, m_s, l_s, acc_s, *, tq, tk, n_sel, scale):
    i = pl.program_id(1)
    nch = (i * tq + tq + tk - 1) // tk
    qpos = i * tq + lax.broadcasted_iota(I32, (1, tq), 1)
    kiota = lax.broadcasted_iota(I32, (tk, 1), 0)
    wt = wt_ref[0]
    half = IDX_HEADS // 2

    def score_chunk(c, carry):
        k0 = pl.multiple_of(c * tk, tk)
        kk = jnp.concatenate([ika_ref[pl.ds(k0, tk), :], ikb_ref[pl.ds(k0, tk), :]], axis=0)
        score = jnp.zeros((tk, tq), F32)
        for j in range(half):
            r = jnp.maximum(_nt_dot(kk, iq_ref[:, 128 * j:128 * j + 128]), 0.0)
            score = score + r[:tk] * wt[2 * j:2 * j + 1, :] + r[tk:] * wt[2 * j + 1:2 * j + 2, :]
        score = score * IDX_SCALE
        score = jnp.where(k0 + kiota <= qpos, score, NEG_INF)
        bits = pltpu.bitcast(score, I32)
        key_s[pl.ds(k0, tk), :] = jnp.where(bits < 0, bits ^ jnp.int32(0x7FFFFFFF), bits)
        return carry

    lax.fori_loop(0, nch, score_chunk, 0)

    def count(pred):
        def body(c, acc):
            k0 = pl.multiple_of(c * tk, tk)
            hit = pred(key_s[pl.ds(k0, tk), :], k0 + kiota).astype(I32)
            return acc + jnp.sum(hit.reshape(tk // 8, 8, tq), axis=0)
        part = lax.fori_loop(0, nch, body, jnp.zeros((8, tq), I32))
        return jnp.sum(part, axis=0, keepdims=True)

    c0 = count(lambda k, idx: k >= 0)
    thr = jnp.where(c0 >= n_sel, jnp.int32(0), jnp.int32(INT_MIN))

    def vbit(b, thr):
        cand = thr + jnp.left_shift(jnp.int32(1), 30 - b)
        cnt = count(lambda k, idx: k >= cand)
        return jnp.where(cnt >= n_sel, cand, thr)

    thr = lax.fori_loop(0, 31, vbit, thr)
    n_ge = count(lambda k, idx: k >= thr)
    idx_bits = max(1, int(np.ceil(np.log2(key_s.shape[0]))))

    def tie_bound():
        need = n_sel - count(lambda k, idx: k > thr)

        def ibit(b, bound):
            cand = bound + jnp.left_shift(jnp.int32(1), idx_bits - 1 - b)
            cnt = count(lambda k, idx: (k == thr) & (idx < cand))
            return jnp.where(cnt < need, cand, bound)

        return lax.fori_loop(0, idx_bits, ibit, jnp.zeros((1, tq), I32))

    bound = lax.cond(jnp.max(n_ge) > n_sel, tie_bound, lambda: jnp.full((1, tq), 2 ** idx_bits, I32))

    qc = jnp.concatenate([dq_ref[:, 128 * h:128 * h + 128] for h in range(DSA_HEADS)], axis=0)
    sc2 = scale * LOG2E
    m_s[...] = jnp.full(m_s.shape, NEG_INF, F32)
    l_s[...] = jnp.zeros(l_s.shape, F32)
    acc_s[...] = jnp.zeros(acc_s.shape, F32)

    def attn_chunk(c, carry):
        k0 = pl.multiple_of(c * tk, tk)
        key = key_s[pl.ds(k0, tk), :]
        idx = k0 + kiota
        sel = ((key > thr) | ((key == thr) & (idx <= bound))) & (idx <= qpos)
        bias = jnp.where(sel, 0.0, NEG_INF)
        s = _nt_dot(dk_ref[pl.ds(k0, tk), :], qc) + jnp.concatenate([bias] * DSA_HEADS, axis=1)
        m_prev = m_s[...]
        m_new = jnp.maximum(m_prev, jnp.max(s, axis=0, keepdims=True))
        m_safe = jnp.where(m_new == NEG_INF, 0.0, m_new)
        alpha = jnp.exp2((m_prev - m_safe) * sc2)
        p = jnp.exp2((s - m_safe) * sc2)
        l_s[...] = alpha * l_s[...] + jnp.sum(p, axis=0, keepdims=True)
        acc_s[...] = alpha * acc_s[...] + jnp.dot(vt_ref[:, pl.ds(k0, tk)], p.astype(BF16),
                                                  preferred_element_type=F32)
        m_s[...] = m_new
        return carry

    lax.fori_loop(0, nch, attn_chunk, 0)
    ot = acc_s[...] / l_s[...]
    for h in range(DSA_HEADS):
        o_ref[:, 128 * h:128 * h + 128] = _head_out(ot[:, h * tq:(h + 1) * tq].T, g_ref[:, 128 * h:128 * h + 128])


def _dsa(iq, ika, ikb, wt, dq, dk, vt, g_out, B, S, tq=256, tk=256):
    T = iq.shape[0]
    tk = min(tk, S)
    nq = S // tq
    H = DSA_HEADS
    n_sel = min(TOPK_MAX, S // 4)
    g_c = g_out[:, (MLA_HEADS + SB_HEADS) * HEAD_DIM:]
    return pl.pallas_call(
        functools.partial(_dsa_kernel, tq=tq, tk=tk, n_sel=n_sel, scale=HEAD_DIM ** -0.5),
        grid=(B, nq),
        in_specs=[pl.BlockSpec((tq, IDX_HEADS * IDX_DIM), lambda b, i: (b * nq + i, 0)),
                  pl.BlockSpec((S, 128), lambda b, i: (b, 0)),
                  pl.BlockSpec((S, 128), lambda b, i: (b, 0)),
                  pl.BlockSpec((1, IDX_HEADS, tq), lambda b, i: (b, 0, i)),
                  pl.BlockSpec((tq, H * 128), lambda b, i: (b * nq + i, 0)),
                  pl.BlockSpec((S, 128), lambda b, i: (b, 0)),
                  pl.BlockSpec((128, S), lambda b, i: (0, b)),
                  pl.BlockSpec((1, H * 128), lambda b, i: (0, 0))],
        out_specs=pl.BlockSpec((tq, H * 128), lambda b, i: (b * nq + i, 0)),
        out_shape=jax.ShapeDtypeStruct((T, H * HEAD_DIM), BF16),
        scratch_shapes=[pltpu.VMEM((S, tq), I32), pltpu.VMEM((1, H * tq), F32), pltpu.VMEM((1, H * tq), F32),
                        pltpu.VMEM((128, H * tq), F32)],
        compiler_params=_cparams(("parallel", "arbitrary")),
        name="dsa_attention",
    )(iq, ika, ikb, wt, dq, dk, vt, g_c)


def _route_kernel(x_ref, g_ref, sc_ref, sh_ref, wr_ref, br_ref, tri_ref,
                  h_ref, hp_ref, eidx_ref, gate_ref, pos_ref, cnt_ref, run_s):
    @pl.when(pl.program_id(0) == 0)
    def _():
        run_s[...] = jnp.zeros(run_s.shape, F32)

    h = _rms(x_ref[...], g_ref[...]) * (1.0 + sc_ref[0]) + sh_ref[0]
    hb = h.astype(BF16)
    h_ref[...] = hb
    tm, D = h.shape
    bits = pltpu.bitcast(hb.astype(F32), jnp.uint32)
    hp_ref[...] = (bits[:, :D // 2] >> 16) | (bits[:, D // 2:] & jnp.uint32(0xFFFF0000))
    logits = _nt_dot(wr_ref[...], hb)
    scores = jax.nn.sigmoid(logits)
    biased = scores + br_ref[...]
    gsz = N_EXPERTS // N_GROUPS
    b3 = biased.reshape(N_GROUPS, gsz, tm)
    m1 = jnp.max(b3, axis=1, keepdims=True)
    n1 = jnp.sum((b3 == m1).astype(F32), axis=1, keepdims=True)
    m2 = jnp.max(jnp.where(b3 < m1, b3, NEG_INF), axis=1, keepdims=True)
    grp = (m1 + jnp.where(n1 >= 2.0, m1, m2)).reshape(N_GROUPS, tm)
    gi = lax.broadcasted_iota(I32, (N_GROUPS, 1), 0)
    grank = jnp.zeros((N_GROUPS, tm), F32)
    for g in range(N_GROUPS):
        rowv = grp[g:g + 1, :]
        grank = grank + jnp.where((rowv > grp) | ((rowv == grp) & (g < gi)), 1.0, 0.0)
    gmask = grank.reshape(N_GROUPS, 1, tm) < float(TOPK_GROUPS)
    masked = jnp.where(gmask, b3, NEG_INF).reshape(N_EXPERTS, tm)
    ei = lax.broadcasted_iota(I32, (N_EXPERTS, 1), 0)
    rank = jnp.zeros((N_EXPERTS, tm), F32)
    for e in range(N_EXPERTS):
        rowv = masked[e:e + 1, :]
        rank = rank + jnp.where((rowv > masked) | ((rowv == masked) & (e < ei)), 1.0, 0.0)
    sel = rank < float(TOP_K)
    selm = sel.astype(F32)
    gsum = jnp.sum(scores * selm, axis=0, keepdims=True)
    gate = scores * selm / gsum * ROUTED_SCALE
    within = jnp.dot(selm.astype(BF16), tri_ref[...], preferred_element_type=F32)
    posf = within + run_s[...]
    run_s[...] += jnp.sum(selm, axis=1, keepdims=True)
    cnt_ref[...] = jnp.broadcast_to(run_s[...], cnt_ref.shape)
    eif = ei.astype(F32)
    for k in range(TOP_K):
        onek = rank == float(k)
        eidx_ref[k:k + 1, :] = jnp.sum(jnp.where(onek, eif, 0.0), axis=0, keepdims=True).astype(I32)
        gate_ref[k:k + 1, :] = jnp.sum(jnp.where(onek, gate, 0.0), axis=0, keepdims=True)
        pos_ref[k:k + 1, :] = jnp.sum(jnp.where(onek, posf, 0.0), axis=0, keepdims=True).astype(I32)


def _route(x, g, sc, sh, w_router, b_router, S, tm=512):
    T, D = x.shape
    B = sc.shape[0]
    tm = min(tm, S)
    r = np.arange(tm)
    tri = jnp.asarray((r[:, None] < r[None, :]).astype(np.float32), BF16)
    full = lambda shape: pl.BlockSpec(shape, lambda i: (0,) * len(shape))
    return pl.pallas_call(
        _route_kernel,
        grid=(T // tm,),
        in_specs=[pl.BlockSpec((tm, D), lambda i: (i, 0)),
                  full((1, D)),
                  pl.BlockSpec((1, 1, D), lambda i: (i * tm // S, 0, 0)),
                  pl.BlockSpec((1, 1, D), lambda i: (i * tm // S, 0, 0)),
                  full((N_EXPERTS, D)),
                  full((N_EXPERTS, 1)),
                  full((tm, tm))],
        out_specs=[pl.BlockSpec((tm, D), lambda i: (i, 0)),
                   pl.BlockSpec((tm, D // 2), lambda i: (i, 0)),
                   pl.BlockSpec((TOP_K, tm), lambda i: (0, i)),
                   pl.BlockSpec((TOP_K, tm), lambda i: (0, i)),
                   pl.BlockSpec((TOP_K, tm), lambda i: (0, i)),
                   full((N_EXPERTS, LANES))],
        out_shape=[jax.ShapeDtypeStruct((T, D), BF16),
                   jax.ShapeDtypeStruct((T, D // 2), jnp.uint32),
                   jax.ShapeDtypeStruct((TOP_K, T), I32),
                   jax.ShapeDtypeStruct((TOP_K, T), F32),
                   jax.ShapeDtypeStruct((TOP_K, T), I32),
                   jax.ShapeDtypeStruct((N_EXPERTS, LANES), F32)],
        scratch_shapes=[pltpu.VMEM((N_EXPERTS, 1), F32)],
        compiler_params=_cparams(("arbitrary",)),
        name="ffn_norm_route",
    )(x, g.reshape(1, D), sc.reshape(B, 1, D), sh.reshape(B, 1, D),
      w_router.T.astype(BF16), b_router.reshape(N_EXPERTS, 1), tri)


IDX_ALIGN = 1024
IDX_WIN = 2 * IDX_ALIGN


def _moe_kernel(bexp_ref, bval_ref, bcs_ref, slot_hbm, h_hbm, wg_ref, wu_ref, wd_ref, y_hbm,
                xbuf0, xbuf1, x_s, mid_s, ybuf0, ybuf1, wg_s, wu_s, wd_s, idx_s, isem, gsem, ssem, *, n_tok):
    b = pl.program_id(0)
    nb = pl.num_programs(0)
    rows, D = x_s.shape
    half = D // 2
    DE = mid_s.shape[1]
    YS = D // LANES
    NW = 256
    xbufs, ybufs = (xbuf0, xbuf1), (ybuf0, ybuf1)
    n_real = TOP_K * n_tok

    def idx_copy(blk):
        start = pl.multiple_of((bcs_ref[blk] // IDX_ALIGN) * IDX_ALIGN, IDX_ALIGN)
        return pltpu.make_async_copy(slot_hbm.at[pl.ds(start, IDX_WIN)],
                                     idx_s.at[pl.ds(pl.multiple_of((blk & 3) * IDX_WIN, IDX_WIN), IDX_WIN)],
                                     isem.at[blk & 3])

    def block_rows(blk):
        inside = (blk >= 0) & (blk < nb)
        safe = jnp.clip(blk, 0, nb - 1)
        return (blk & 3) * IDX_WIN + bcs_ref[safe] % IDX_ALIGN, jnp.where(inside, bval_ref[safe], 0)

    def row_id(blk, base, valid, r):
        return jnp.where(r < valid, idx_s[base + r], n_real + (blk & 1) * rows + r)

    def gather_start(blk, base, valid, r, p):
        tok = row_id(blk, base, valid, r) & (n_tok - 1)
        pltpu.make_async_copy(h_hbm.at[pl.ds(pl.multiple_of(tok * 8, 8), 8), :],
                              xbufs[p].at[pl.ds(r * 8, 8), :], gsem.at[p]).start(priority=r % 2)

    def scatter_start(blk, base, valid, r, p):
        dst = row_id(blk, base, valid, r)
        pltpu.make_async_copy(ybufs[p].at[pl.ds(r * YS, YS), :],
                              y_hbm.at[pl.ds(pl.multiple_of(dst * YS, YS), YS), :], ssem.at[p]).start(priority=r % 2)

    def wait_gather(p):
        pltpu.make_async_copy(xbufs[p], xbufs[p], gsem.at[p]).wait()

    def wait_scatter(p):
        pltpu.make_async_copy(ybufs[p], ybufs[p], ssem.at[p]).wait()

    def compute_pieces(p):
        def unpack():
            for s in range(8):
                w = xbufs[p][pl.ds(s, rows, stride=8), :]
                x_s[:, 128 * s:128 * s + 128] = pltpu.bitcast(w << 16, F32).astype(BF16)
                x_s[:, half + 128 * s:half + 128 * s + 128] = pltpu.bitcast(w & jnp.uint32(0xFFFF0000), F32).astype(BF16)

        def gate_up(j):
            def piece():
                x = x_s[...]
                gt = jnp.dot(x, wg_s[:, NW * j:NW * j + NW], preferred_element_type=F32)
                up = jnp.dot(x, wu_s[:, NW * j:NW * j + NW], preferred_element_type=F32)
                mid_s[:, NW * j:NW * j + NW] = (gt * jax.nn.sigmoid(gt) * up).astype(BF16)
            return piece

        def down(n):
            def piece():
                y = jnp.dot(mid_s[...], wd_s[:, NW * n:NW * n + NW], preferred_element_type=F32)
                for s in range(NW // LANES):
                    ybufs[p][pl.ds(n * (NW // LANES) + s, rows, stride=YS), :] = y[:, LANES * s:LANES * s + LANES]
            return piece
        return [unpack] + [gate_up(j) for j in range(DE // NW)], [down(n) for n in range(D // NW)]

    def interleave(pieces, starts):
        n_slots = max(len(pieces), 1)
        per = -(-len(starts) // n_slots)
        for n in range(n_slots):
            if pieces:
                pieces[n]()
            for st in starts[n * per:(n + 1) * per]:
                st()

    def run_block(p, compute):
        first, second = compute_pieces(p) if compute else ([], [])
        gbase, gvalid = block_rows(b + 1)
        interleave(first, [functools.partial(gather_start, b + 1, gbase, gvalid, r, 1 - p) for r in range(rows)])
        wait_scatter(p)
        sbase, svalid = block_rows(b - 1)
        interleave(second, [functools.partial(scatter_start, b - 1, sbase, svalid, r, 1 - p) for r in range(rows)])

    @pl.when(b == 0)
    def _():
        ybuf0[...] = jnp.zeros(ybuf0.shape, F32)
        ybuf1[...] = jnp.zeros(ybuf1.shape, F32)
        pltpu.make_async_copy(ybuf0, y_hbm.at[pl.ds(n_real * YS, rows * YS), :], ssem.at[0]).start()
        idx_copy(0).start()
        idx_copy(0).wait()
        base0, valid0 = block_rows(b)
        for r in range(rows):
            gather_start(b, base0, valid0, r, 0)
        idx_copy(1).start()

    @pl.when(b + 2 < nb)
    def _():
        idx_copy(b + 2).start()

    @pl.when(b + 1 < nb)
    def _():
        idx_copy(b + 1).wait()

    @pl.when((b == 0) | (bexp_ref[b] != bexp_ref[jnp.maximum(b - 1, 0)]))
    def _():
        wg_s[...] = wg_ref[0, 0].astype(BF16)
        wu_s[...] = wu_ref[0, 0].astype(BF16)
        wd_s[...] = wd_ref[0, 0].astype(BF16)

    active = bval_ref[b] > 0
    for p in range(2):
        @pl.when((b & 1) == p)
        def _():
            wait_gather(p)

            @pl.when(active)
            def _():
                run_block(p, True)

            @pl.when(jnp.logical_not(active))
            def _():
                run_block(p, False)

            @pl.when(b == nb - 1)
            def _():
                sbase, svalid = block_rows(b)
                for r in range(rows):
                    scatter_start(b, sbase, svalid, r, p)
                wait_scatter(p)
                wait_scatter(1 - p)
                wait_gather(1 - p)


def _moe(hp, slot_ids, blk_exp, blk_val, blk_cs, w_gate, w_up, w_down, l):
    T = hp.shape[0] // 8
    D = w_gate.shape[2]
    assert T & (T - 1) == 0 and D == 2 * 8 * LANES
    DE = w_gate.shape[3]
    nb = blk_exp.shape[0]
    wspec = lambda shape: pl.BlockSpec((1, 1) + shape, lambda b, bexp, bval, bcs: (l, bexp[b], 0, 0))
    grid_spec = pltpu.PrefetchScalarGridSpec(
        num_scalar_prefetch=3,
        grid=(nb,),
        in_specs=[pl.BlockSpec(memory_space=pl.ANY), pl.BlockSpec(memory_space=pl.ANY),
                  wspec((D, DE)), wspec((D, DE)), wspec((DE, D))],
        out_specs=pl.BlockSpec(memory_space=pl.ANY),
        scratch_shapes=[pltpu.VMEM((MOE_ROWS * 8, LANES), jnp.uint32), pltpu.VMEM((MOE_ROWS * 8, LANES), jnp.uint32),
                        pltpu.VMEM((MOE_ROWS, D), BF16), pltpu.VMEM((MOE_ROWS, DE), BF16),
                        pltpu.VMEM((MOE_ROWS * D // LANES, LANES), F32), pltpu.VMEM((MOE_ROWS * D // LANES, LANES), F32),
                        pltpu.VMEM((D, DE), BF16), pltpu.VMEM((D, DE), BF16), pltpu.VMEM((DE, D), BF16),
                        pltpu.SMEM((4 * IDX_WIN,), I32),
                        pltpu.SemaphoreType.DMA((4,)), pltpu.SemaphoreType.DMA((2,)), pltpu.SemaphoreType.DMA((2,))],
    )
    return pl.pallas_call(
        functools.partial(_moe_kernel, n_tok=T),
        grid_spec=grid_spec,
        out_shape=jax.ShapeDtypeStruct(((TOP_K * T + 2 * MOE_ROWS) * D // LANES, LANES), F32),
        compiler_params=_cparams(("arbitrary",)),
        name="moe_experts",
    )(blk_exp, blk_val, blk_cs, slot_ids, hp, w_gate, w_up, w_down)


def _ffn_out_kernel(*refs, final):
    y_refs = refs[:TOP_K]
    gk_ref, h_ref, x_ref, gf_ref, wsg_ref, wsu_ref, wsd_ref, fg_ref, o_ref, r_s = refs[TOP_K:]
    tm, D = x_ref.shape
    ys = D // LANES
    acc = None
    for k in range(TOP_K):
        gk = jnp.broadcast_to(gk_ref[k][:, None, :], (tm, ys, LANES)).reshape(tm * ys, LANES)
        term = y_refs[k][...] * gk
        acc = term if acc is None else acc + term
    r_s[...] = acc
    routed = jnp.concatenate([r_s[pl.ds(s, tm, stride=ys), :] for s in range(ys)], axis=1)
    hb = h_ref[...]
    gt = jnp.dot(hb, wsg_ref[...], preferred_element_type=F32)
    up = jnp.dot(hb, wsu_ref[...], preferred_element_type=F32)
    mid = (gt * jax.nn.sigmoid(gt) * up).astype(BF16)
    shared = jnp.dot(mid, wsd_ref[...], preferred_element_type=F32)
    out = x_ref[...] + gf_ref[0] * (routed + shared)
    if final:
        out = _rms(out, fg_ref[...])
    o_ref[...] = out


def _ffn_out(y, gate3, h, x, g_f, wsg, wsu, wsd, final_g, final, S, tm=128):
    T, D = x.shape
    B = g_f.shape[0]
    DS = wsg.shape[1]
    tm = min(tm, S)
    nt = T // tm
    full = lambda shape: pl.BlockSpec(shape, lambda i: (0,) * len(shape))
    y_specs = [pl.BlockSpec((tm * D // LANES, LANES), functools.partial(lambda i, k: (k * nt + i, 0), k=k))
               for k in range(TOP_K)]
    return pl.pallas_call(
        functools.partial(_ffn_out_kernel, final=final),
        grid=(nt,),
        in_specs=y_specs + [
                  pl.BlockSpec((TOP_K, tm, 1), lambda i: (0, i, 0)),
                  pl.BlockSpec((tm, D), lambda i: (i, 0)),
                  pl.BlockSpec((tm, D), lambda i: (i, 0)),
                  pl.BlockSpec((1, 1, D), lambda i: (i * tm // S, 0, 0)),
                  full((D, DS)), full((D, DS)), full((DS, D)), full((1, D))],
        out_specs=pl.BlockSpec((tm, D), lambda i: (i, 0)),
        out_shape=jax.ShapeDtypeStruct((T, D), F32),
        scratch_shapes=[pltpu.VMEM((tm * D // LANES, LANES), F32)],
        compiler_params=_cparams(("parallel",)),
        name="ffn_combine",
    )(*([y] * TOP_K), gate3, h, x, g_f.reshape(B, 1, D), wsg, wsu, wsd, final_g.reshape(1, D))


def _layout_w_in(w):
    D = w.shape[0]
    sizes = (Q_LORA, KV_LORA, MLA_ROPE, 512, 512, 512, 512, HEAD_DIM, HEAD_DIM, IDX_HEADS * IDX_DIM, IDX_DIM, IDX_HEADS)
    offs = np.concatenate([[0], np.cumsum(sizes)])
    cq, ckv, kr, sbq, sbk, sbv, dsq, dsk, dsv, ixq, ixk, ixw = [w[:, offs[n]:offs[n + 1]] for n in range(12)]
    z = lambda n: jnp.zeros((D, n), w.dtype)
    out = jnp.concatenate([cq, sbq, sbk, sbv, dsq, ckv, dsk, dsv, ixq, kr, z(64), ixk, z(64), ixw, z(112), z(128)], axis=1)
    assert out.shape[1] == IN_PAD
    return out.astype(BF16)


def _layout_w_uq(w):
    K = w.shape[0]
    w3 = w.reshape(K, MLA_HEADS, MLA_NOPE + MLA_ROPE)
    w3 = jnp.concatenate([w3, jnp.zeros((K, MLA_HEADS, 64), w.dtype)], axis=2)
    return w3.reshape(K, MLA_HEADS * 256).astype(BF16)


def _layout_w_ukv(w):
    K = w.shape[0]
    w3 = w.reshape(K, MLA_HEADS, MLA_NOPE + HEAD_DIM)
    return jnp.concatenate([w3[:, :, :MLA_NOPE].reshape(K, -1), w3[:, :, MLA_NOPE:].reshape(K, -1)], axis=1).astype(BF16)


def _expert_blocks(eidx, pos, counts, T):
    M = T * TOP_K
    nb = M // MOE_ROWS + N_EXPERTS
    padded = (counts + MOE_ROWS - 1) // MOE_ROWS * MOE_ROWS
    pend = jnp.cumsum(padded)
    pstart = pend - padded
    cstart = jnp.cumsum(counts) - counts
    experts = jnp.arange(N_EXPERTS, dtype=I32)
    order = eidx * T + pos
    out_row = jnp.arange(TOP_K, dtype=I32)[:, None] * T + jnp.arange(T, dtype=I32)[None, :]
    _, slot_ids = lax.sort_key_val(order.reshape(-1), out_row.reshape(-1))
    slot_ids = jnp.concatenate([slot_ids, jnp.zeros((IDX_WIN,), I32)])
    blk_start = jnp.arange(nb, dtype=I32) * MOE_ROWS
    blk_exp = jnp.minimum(jnp.sum((pend[None, :] <= blk_start[:, None]).astype(I32), axis=1), N_EXPERTS - 1)
    pick = lambda v: jnp.sum(jnp.where(blk_exp[:, None] == experts, v, 0), axis=-1)
    blk_val = jnp.clip(pick(pstart + counts) - blk_start, 0, MOE_ROWS).astype(I32)
    blk_cs = jnp.where(blk_val > 0, pick(cstart - pstart) + blk_start, 0).astype(I32)
    return slot_ids, blk_exp, blk_val, blk_cs


def _mixers(x, S, B, positions_tabs, mod, l, w):
    sh_m, sc_m, g_m = mod[0], mod[1], mod[2]
    proj = _nmm(x, 0, x.shape[1], w["norm_mix_g"][l], w["w_in"][l], S, sc=sc_m, sh=sh_m)
    q_raw = _nmm(proj, C_CQ // Q_LORA, Q_LORA, w["g_cq"][l], w["w_uq"][l], S)
    kv_raw = _nmm(proj, C_CKV // KV_LORA, KV_LORA, w["g_ckv"][l], w["w_ukv"][l], S)
    tabs64, tabs128 = positions_tabs
    (qm, km, vmt, sbq, sbk, sbvt, dq, dk, dvt, iq, ika, ikb, iw) = _prep(proj, q_raw, kv_raw, tabs64, tabs128)
    g_out = w["g_out"][l].reshape(1, -1)
    o_a = _mla(qm, km, vmt, g_out, B, S)
    o_b = _sb(sbq, sbk, sbvt, g_out, B, S)
    wt = iw[:, :IDX_HEADS].reshape(B, S, IDX_HEADS).transpose(0, 2, 1)
    o_c = _dsa(iq, ika, ikb, wt, dq, dk, dvt, g_out, B, S)
    return _mm_res((o_a, o_b, o_c), w["w_o"][l], x, g_m, S)


def _ffn(x, S, B, mod, l, w, final_g, final):
    T, D = x.shape
    sh_f, sc_f, g_f = mod[3], mod[4], mod[5]
    h, hp, eidx, gate, pos, cnt = _route(x, w["norm_ffn_g"][l], sc_f, sh_f, w["w_router"][l], w["b_router"][l], S)
    counts = cnt[:, 0].astype(I32)
    slot_ids, blk_exp, blk_val, blk_cs = _expert_blocks(eidx, pos, counts, T)
    y = _moe(hp.reshape(T * 8, LANES), slot_ids, blk_exp, blk_val, blk_cs, w["w_gate"], w["w_up"], w["w_down"], l)
    return _ffn_out(y, gate.reshape(TOP_K, T, 1), h, x, g_f, w["ws_gate"][l], w["ws_up"][l], w["ws_down"][l],
                    final_g, final, S)


def kernel(x, c, positions, norm_mix_g, norm_ffn_g, w_ada, b_ada, w_in, g_cq, g_ckv, w_uq, w_ukv, g_out, w_o,
           w_router, b_router, w_gate, w_up, w_down, ws_gate, ws_up, ws_down, final_g):
    B, S, D = x.shape
    L = w_in.shape[0]
    T = B * S
    w = dict(norm_mix_g=norm_mix_g, norm_ffn_g=norm_ffn_g, g_cq=g_cq, g_ckv=g_ckv, g_out=g_out,
             w_in=jax.vmap(_layout_w_in)(w_in), w_uq=jax.vmap(_layout_w_uq)(w_uq), w_ukv=jax.vmap(_layout_w_ukv)(w_ukv),
             w_o=w_o.astype(BF16), w_router=w_router, b_router=b_router, w_gate=w_gate, w_up=w_up, w_down=w_down,
             ws_gate=ws_gate.astype(BF16), ws_up=ws_up.astype(BF16), ws_down=ws_down.astype(BF16))
    mod_all = _ada(c, w_ada, b_ada).reshape(L, B, 6, D)
    posf = positions.astype(F32).reshape(T, 1)
    tabs = (_rope_tables(posf, 64), _rope_tables(posf, 128))
    xt = x.reshape(T, D)
    for l in range(L):
        mod = [mod_all[l, :, n, :] for n in range(6)]
        xt = _mixers(xt, S, B, tabs, mod, l, w)
        xt = _ffn(xt, S, B, mod, l, w, final_g, l == L - 1)
    return xt.reshape(B, S, D)
```

```python
import functools

import jax
import jax.numpy as jnp
import numpy as np
from jax import lax
from jax.experimental import pallas as pl
from jax.experimental.pallas import tpu as pltpu

F32 = jnp.float32
BF16 = jnp.bfloat16
I32 = jnp.int32

HEAD_DIM = 128
MLA_HEADS = 8
SB_HEADS = 4
DSA_HEADS = 4
N_MIX_HEADS = MLA_HEADS + SB_HEADS + DSA_HEADS
Q_LORA = 512
KV_LORA = 256
MLA_NOPE = 128
MLA_ROPE = 64
IDX_HEADS = 16
IDX_DIM = 64
IDX_SCALE = (IDX_HEADS * IDX_DIM) ** -0.5
TOPK_MAX = 256
ROPE_THETA = 10000.0
N_EXPERTS = 64
TOP_K = 8
N_GROUPS = 8
TOPK_GROUPS = 4
ROUTED_SCALE = 2.5
EPS = 1e-6

LANES = 128
VMEM_LIMIT = 56 * 1024 * 1024
NEG_INF = float("-inf")
INT_MIN = -(2 ** 31)

C_CQ, C_SBQ, C_SBK, C_SBV, C_DSQ = 0, 512, 1024, 1536, 2048
C_CKV, C_DSK, C_DSV, C_IXQ = 2560, 2816, 2944, 3072
C_KR, C_IXK, C_IXW, IN_PAD = 4096, 4224, 4352, 4608

MOE_ROWS = 256


def _cparams(sem):
    return pltpu.CompilerParams(dimension_semantics=sem, vmem_limit_bytes=VMEM_LIMIT)


def _nt_dot(a, b):
    return lax.dot_general(a, b, (((1,), (1,)), ((), ())), preferred_element_type=F32)


def _rms(x, g):
    return x * lax.rsqrt(jnp.mean(x * x, axis=-1, keepdims=True) + EPS) * g


def _ada_kernel(c_ref, w_ref, b_ref, o_ref):
    c = c_ref[...]
    a = (c * jax.nn.sigmoid(c)).astype(BF16)
    o_ref[0] = jnp.dot(a, w_ref[0].astype(BF16), preferred_element_type=F32) + b_ref[0]


def _ada(c, w_ada, b_ada):
    L, D, N = w_ada.shape
    B = c.shape[0]
    tn = 1024
    return pl.pallas_call(
        _ada_kernel,
        grid=(L, N // tn),
        in_specs=[pl.BlockSpec((B, D), lambda l, j: (0, 0)),
                  pl.BlockSpec((1, D, tn), lambda l, j: (l, 0, j)),
                  pl.BlockSpec((1, 1, tn), lambda l, j: (l, 0, j))],
        out_specs=pl.BlockSpec((1, B, tn), lambda l, j: (l, 0, j)),
        out_shape=jax.ShapeDtypeStruct((L, B, N), F32),
        compiler_params=_cparams(("parallel", "parallel")),
        name="ada_mod",
    )(c, w_ada, b_ada.reshape(L, 1, N))


def _rope_tab_kernel(pos_ref, inv_ref, sgn_ref, cos_ref, sin_ref):
    ang = pos_ref[...] * inv_ref[...]
    cos_ref[...] = jnp.cos(ang)
    sin_ref[...] = jnp.sin(ang) * sgn_ref[...]


def _rope_tables(posf, d):
    T = posf.shape[0]
    half = d // 2
    inv = ROPE_THETA ** (-jnp.arange(0, d, 2, dtype=F32) / d)
    reps = LANES // half
    inv_t = jnp.tile(inv, reps).reshape(1, LANES)
    sgn = jnp.tile(jnp.concatenate([-jnp.ones((half,), F32), jnp.ones((half,), F32)]), LANES // d).reshape(1, LANES)
    tm = min(T, 1024)
    return pl.pallas_call(
        _rope_tab_kernel,
        grid=(T // tm,),
        in_specs=[pl.BlockSpec((tm, 1), lambda i: (i, 0)),
                  pl.BlockSpec((1, LANES), lambda i: (0, 0)),
                  pl.BlockSpec((1, LANES), lambda i: (0, 0))],
        out_specs=[pl.BlockSpec((tm, LANES), lambda i: (i, 0))] * 2,
        out_shape=[jax.ShapeDtypeStruct((T, LANES), F32)] * 2,
        compiler_params=_cparams(("parallel",)),
        name=f"rope_tab{d}",
    )(posf, inv_t, sgn)


def _rope64(x, cos, sin_s, first_half):
    rot = jnp.where(first_half, pltpu.roll(x, 96, 1), pltpu.roll(x, 32, 1))
    return x * cos + rot * sin_s


def _rope128(x, cos, sin_s):
    return x * cos + pltpu.roll(x, 64, 1) * sin_s


def _nmm_kernel(*refs, modulate):
    if modulate:
        x_ref, g_ref, sc_ref, sh_ref, w_ref, o_ref, h_s = refs
    else:
        x_ref, g_ref, w_ref, o_ref, h_s = refs

    @pl.when(pl.program_id(1) == 0)
    def _():
        h = _rms(x_ref[...], g_ref[...])
        if modulate:
            h = h * (1.0 + sc_ref[0]) + sh_ref[0]
        h_s[...] = h.astype(BF16)

    o_ref[...] = jnp.dot(h_s[...], w_ref[...], preferred_element_type=F32).astype(o_ref.dtype)


def _nmm(x, col_blk, K, g, w, S, sc=None, sh=None, tm=1024, tn=512, out_dtype=F32):
    T = x.shape[0]
    N = w.shape[1]
    tm = min(tm, S)
    tn = min(tn, N)
    modulate = sc is not None
    in_specs = [pl.BlockSpec((tm, K), lambda i, j: (i, col_blk)),
                pl.BlockSpec((1, K), lambda i, j: (0, 0))]
    args = [x, g.reshape(1, K)]
    if modulate:
        B = sc.shape[0]
        in_specs += [pl.BlockSpec((1, 1, K), lambda i, j: (i * tm // S, 0, 0))] * 2
        args += [sc.reshape(B, 1, K), sh.reshape(B, 1, K)]
    in_specs.append(pl.BlockSpec((K, tn), lambda i, j: (0, j)))
    args.append(w)
    return pl.pallas_call(
        functools.partial(_nmm_kernel, modulate=modulate),
        grid=(T // tm, N // tn),
        in_specs=in_specs,
        out_specs=pl.BlockSpec((tm, tn), lambda i, j: (i, j)),
        out_shape=jax.ShapeDtypeStruct((T, N), out_dtype),
        scratch_shapes=[pltpu.VMEM((tm, K), BF16)],
        compiler_params=_cparams(("parallel", "arbitrary")),
        name="norm_matmul",
    )(*args)


def _mm_res_kernel(*refs):
    n = len(refs) - 4
    a_refs, (w_ref, r_ref, gt_ref, o_ref) = refs[:n], refs[n:]
    acc, k0 = None, 0
    for a_ref in a_refs:
        k = a_ref.shape[1]
        part = jnp.dot(a_ref[...], w_ref[k0:k0 + k, :], preferred_element_type=F32)
        acc = part if acc is None else acc + part
        k0 += k
    o_ref[...] = r_ref[...] + gt_ref[0] * acc


def _mm_res(parts, w, res, gate, S, tm=1024, tn=512):
    T = parts[0].shape[0]
    K, N = w.shape
    B = gate.shape[0]
    tm = min(tm, S)
    return pl.pallas_call(
        _mm_res_kernel,
        grid=(T // tm, N // tn),
        in_specs=[pl.BlockSpec((tm, a.shape[1]), lambda i, j: (i, 0)) for a in parts] + [
                  pl.BlockSpec((K, tn), lambda i, j: (0, j)),
                  pl.BlockSpec((tm, tn), lambda i, j: (i, j)),
                  pl.BlockSpec((1, 1, tn), lambda i, j: (i * tm // S, 0, j))],
        out_specs=pl.BlockSpec((tm, tn), lambda i, j: (i, j)),
        out_shape=jax.ShapeDtypeStruct((T, N), F32),
        compiler_params=_cparams(("parallel", "parallel")),
        name="out_proj_residual",
    )(*parts, w, res, gate.reshape(B, 1, N))


def _prep_kernel(p_ref, q_ref, kv_ref, c64_ref, s64_ref, c128_ref, s128_ref,
                 qm_ref, km_ref, vm_ref, sbq_ref, sbk_ref, sbv_ref,
                 dq_ref, dk_ref, dv_ref, iq_ref, ika_ref, ikb_ref, iw_ref):
    c64, s64 = c64_ref[...], s64_ref[...]
    c128, s128 = c128_ref[...], s128_ref[...]
    lane = lax.broadcasted_iota(I32, c64.shape, 1)
    first = (lane % 64) < 32

    def sl(ref, off, width=LANES):
        return ref[:, off:off + width]

    kr = _rope64(sl(p_ref, C_KR), c64, s64, first).astype(BF16)
    for h in range(MLA_HEADS):
        qm_ref[:, 256 * h:256 * h + 128] = sl(q_ref, 256 * h).astype(BF16)
        qm_ref[:, 256 * h + 128:256 * h + 256] = _rope64(sl(q_ref, 256 * h + 128), c64, s64, first).astype(BF16)
        km_ref[:, 256 * h:256 * h + 128] = sl(kv_ref, 128 * h).astype(BF16)
        km_ref[:, 256 * h + 128:256 * h + 256] = kr
    vm_ref[...] = kv_ref[:, MLA_HEADS * 128:].T.astype(BF16)
    sbq_ref[...] = sl(p_ref, C_SBQ, 512).astype(BF16)
    sbk_ref[...] = sl(p_ref, C_SBK, 512).astype(BF16)
    sbv_ref[...] = sl(p_ref, C_SBV, 512).T.astype(BF16)
    for h in range(DSA_HEADS):
        dq_ref[:, 128 * h:128 * h + 128] = _rope128(sl(p_ref, C_DSQ + 128 * h), c128, s128).astype(BF16)
    dk_ref[...] = _rope128(sl(p_ref, C_DSK), c128, s128).astype(BF16)
    dv_ref[...] = sl(p_ref, C_DSV).T.astype(BF16)
    for j in range(IDX_HEADS // 2):
        iq_ref[:, 128 * j:128 * j + 128] = _rope64(sl(p_ref, C_IXQ + 128 * j), c64, s64, first).astype(BF16)
    ik = _rope64(sl(p_ref, C_IXK), c64, s64, first)
    ika_ref[...] = ik.astype(BF16)
    ikb_ref[...] = pltpu.roll(ik, 64, 1).astype(BF16)
    iw_ref[...] = sl(p_ref, C_IXW)


def _prep(proj, q_raw, kv_raw, tabs64, tabs128, tm=256):
    T = proj.shape[0]
    row = lambda w: pl.BlockSpec((tm, w), lambda i: (i, 0))
    widths = [2048, 2048, 1024, 512, 512, 512, 512, 128, 128, 1024, 128, 128]
    transposed = (2, 5, 8)
    out_shape = [jax.ShapeDtypeStruct((w, T) if n in transposed else (T, w), BF16) for n, w in enumerate(widths)]
    out_specs = [pl.BlockSpec((w, tm), lambda i: (0, i)) if n in transposed else row(w) for n, w in enumerate(widths)]
    return pl.pallas_call(
        _prep_kernel,
        grid=(T // tm,),
        in_specs=[row(IN_PAD), row(2048), row(2048), row(LANES), row(LANES), row(LANES), row(LANES)],
        out_specs=out_specs + [row(LANES)],
        out_shape=out_shape + [jax.ShapeDtypeStruct((T, LANES), F32)],
        compiler_params=_cparams(("parallel",)),
        name="mixer_prep",
    )(proj, q_raw, kv_raw, *tabs64, *tabs128)


def _head_out(o, g):
    return _rms(o, g).astype(BF16)


LOG2E = 1.4426950408889634


def _mla_kernel(q_ref, k_ref, vt_ref, g_ref, o_ref, m_s, l_s, acc_s, *, t, nq, scale):
    i = pl.program_id(2)
    hps = m_s.shape[0]
    c = scale * LOG2E
    m_s[...] = jnp.full(m_s.shape, NEG_INF, F32)
    l_s[...] = jnp.zeros(l_s.shape, F32)
    acc_s[...] = jnp.zeros(acc_s.shape, F32)

    def step(cidx, masked):
        k0 = pl.multiple_of(cidx * t, t)
        for hh in range(hps):
            s = _nt_dot(k_ref[pl.ds(k0, t), 256 * hh:256 * hh + 256], q_ref[:, 256 * hh:256 * hh + 256])
            if masked:
                key = lax.broadcasted_iota(I32, (t, t), 0)
                qry = lax.broadcasted_iota(I32, (t, t), 1)
                s = jnp.where(key <= qry, s, NEG_INF)
            m_prev = m_s[hh]
            m_new = jnp.maximum(m_prev, jnp.max(s, axis=0, keepdims=True))
            alpha = jnp.exp2((m_prev - m_new) * c)
            p = jnp.exp2((s - m_new) * c)
            l_s[hh] = alpha * l_s[hh] + jnp.sum(p, axis=0, keepdims=True)
            acc_s[hh] = alpha * acc_s[hh] + jnp.dot(vt_ref[128 * hh:128 * hh + 128, pl.ds(k0, t)], p.astype(BF16),
                                                    preferred_element_type=F32)
            m_s[hh] = m_new

    for n in range(nq):
        @pl.when(i == n)
        def _():
            for cidx in range(n):
                step(cidx, False)
            step(n, True)

    for hh in range(hps):
        o_ref[:, 128 * hh:128 * hh + 128] = _head_out((acc_s[hh] / l_s[hh]).T, g_ref[:, 128 * hh:128 * hh + 128])


def _mla(qm, km, vmt, g_out, B, S, t=1024, hps=2):
    T = qm.shape[0]
    t = min(t, S)
    nq = S // t
    H = MLA_HEADS
    return pl.pallas_call(
        functools.partial(_mla_kernel, t=t, nq=nq, scale=(MLA_NOPE + MLA_ROPE) ** -0.5),
        grid=(B, H // hps, nq),
        in_specs=[pl.BlockSpec((t, 256 * hps), lambda b, h, i: (b * nq + i, h)),
                  pl.BlockSpec((S, 256 * hps), lambda b, h, i: (b, h)),
                  pl.BlockSpec((128 * hps, S), lambda b, h, i: (h, b)),
                  pl.BlockSpec((1, 128 * hps), lambda b, h, i: (0, h))],
        out_specs=pl.BlockSpec((t, 128 * hps), lambda b, h, i: (b * nq + i, h)),
        out_shape=jax.ShapeDtypeStruct((T, H * HEAD_DIM), BF16),
        scratch_shapes=[pltpu.VMEM((hps, 1, t), F32), pltpu.VMEM((hps, 1, t), F32), pltpu.VMEM((hps, 128, t), F32)],
        compiler_params=_cparams(("parallel", "parallel", "arbitrary")),
        name="mla_attention",
    )(qm, km, vmt, g_out)


def _sb_kernel(q_ref, k_ref, vt_ref, g_ref, tri_ref, o_ref, carry_s, acc_s, *, t, nq, scale):
    i = pl.program_id(2)
    hps = carry_s.shape[0]
    tri = tri_ref[...]
    carry_s[...] = jnp.zeros(carry_s.shape, F32)
    acc_s[...] = jnp.zeros(acc_s.shape, F32)

    def step(cidx, masked):
        k0 = pl.multiple_of(cidx * t, t)
        for hh in range(hps):
            z = _nt_dot(k_ref[pl.ds(k0, t), 128 * hh:128 * hh + 128], q_ref[:, 128 * hh:128 * hh + 128]) * (scale * LOG2E)
            sp = jnp.log2(1.0 + jnp.exp2(jnp.minimum(z, -z)))
            log_beta = jnp.minimum(z, 0.0) - sp
            log_keep = jnp.minimum(-z, 0.0) - sp
            if masked:
                key = lax.broadcasted_iota(I32, (t, t), 0)
                qry = lax.broadcasted_iota(I32, (t, t), 1)
                strict = key < qry
                log_keep = jnp.where(strict, log_keep, 0.0)
            hi = log_keep.astype(BF16)
            lo = (log_keep - hi.astype(F32)).astype(BF16)
            suffix = jnp.dot(tri, hi, preferred_element_type=F32) + jnp.dot(tri, lo, preferred_element_type=F32)
            a = jnp.exp2(log_beta + suffix + carry_s[hh])
            if masked:
                a = jnp.where(strict, a, 0.0)
            acc_s[hh] += jnp.dot(vt_ref[128 * hh:128 * hh + 128, pl.ds(k0, t)], a.astype(BF16),
                                 preferred_element_type=F32)
            carry_s[hh] += jnp.sum(log_keep, axis=0, keepdims=True)

    for n in range(nq):
        @pl.when(i == n)
        def _():
            step(n, True)
            for cidx in range(n - 1, -1, -1):
                step(cidx, False)

    for hh in range(hps):
        o_ref[:, 128 * hh:128 * hh + 128] = _head_out(acc_s[hh].T, g_ref[:, 128 * hh:128 * hh + 128])


def _sb(sbq, sbk, sbvt, g_out, B, S, t=512, hps=2):
    T = sbq.shape[0]
    t = min(t, S)
    nq = S // t
    H = SB_HEADS
    r = np.arange(t)
    tri = jnp.asarray((r[None, :] > r[:, None]).astype(np.float32), BF16)
    g0 = MLA_HEADS // hps
    return pl.pallas_call(
        functools.partial(_sb_kernel, t=t, nq=nq, scale=HEAD_DIM ** -0.5),
        grid=(B, H // hps, nq),
        in_specs=[pl.BlockSpec((t, 128 * hps), lambda b, h, i: (b * nq + i, h)),
                  pl.BlockSpec((S, 128 * hps), lambda b, h, i: (b, h)),
                  pl.BlockSpec((128 * hps, S), lambda b, h, i: (h, b)),
                  pl.BlockSpec((1, 128 * hps), lambda b, h, i: (0, g0 + h)),
                  pl.BlockSpec((t, t), lambda b, h, i: (0, 0))],
        out_specs=pl.BlockSpec((t, 128 * hps), lambda b, h, i: (b * nq + i, h)),
        out_shape=jax.ShapeDtypeStruct((T, H * HEAD_DIM), BF16),
        scratch_shapes=[pltpu.VMEM((hps, 1, t), F32), pltpu.VMEM((hps, 128, t), F32)],
        compiler_params=_cparams(("parallel", "parallel", "arbitrary")),
        name="stickbreak_attention",
    )(sbq, sbk, sbvt, g_out, tri)


def _dsa_kernel(iq_ref, ika_ref, ikb_ref, wt_ref, dq_ref, dk_ref, vt_ref, g_ref, o_ref,
                key_s, m_s, l_s, acc_s, *, tq, tk, n_sel, scale):
    i = pl.program_id(1)
    nch = (i * tq + tq + tk - 1) // tk
    qpos = i * tq + lax.broadcasted_iota(I32, (1, tq), 1)
    kiota = lax.broadcasted_iota(I32, (tk, 1), 0)
    wt = wt_ref[0]
    half = IDX_HEADS // 2

    def score_chunk(c, carry):
        k0 = pl.multiple_of(c * tk, tk)
        kk = jnp.concatenate([ika_ref[pl.ds(k0, tk), :], ikb_ref[pl.ds(k0, tk), :]], axis=0)
        score = jnp.zeros((tk, tq), F32)
        for j in range(half):
            r = jnp.maximum(_nt_dot(kk, iq_ref[:, 128 * j:128 * j + 128]), 0.0)
            score = score + r[:tk] * wt[2 * j:2 * j + 1, :] + r[tk:] * wt[2 * j + 1:2 * j + 2, :]
        score = score * IDX_SCALE
        score = jnp.where(k0 + kiota <= qpos, score, NEG_INF)
        bits = pltpu.bitcast(score, I32)
        key_s[pl.ds(k0, tk), :] = jnp.where(bits < 0, bits ^ jnp.int32(0x7FFFFFFF), bits)
        return carry

    lax.fori_loop(0, nch, score_chunk, 0)

    def count(pred):
        def body(c, acc):
            k0 = pl.multiple_of(c * tk, tk)
            hit = pred(key_s[pl.ds(k0, tk), :], k0 + kiota).astype(I32)
            return acc + jnp.sum(hit.reshape(tk // 8, 8, tq), axis=0)
        part = lax.fori_loop(0, nch, body, jnp.zeros((8, tq), I32))
        return jnp.sum(part, axis=0, keepdims=True)

    c0 = count(lambda k, idx: k >= 0)
    thr = jnp.where(c0 >= n_sel, jnp.int32(0), jnp.int32(INT_MIN))

    def vbit(b, thr):
        cand = thr + jnp.left_shift(jnp.int32(1), 30 - b)
        cnt = count(lambda k, idx: k >= cand)
        return jnp.where(cnt >= n_sel, cand, thr)

    thr = lax.fori_loop(0, 31, vbit, thr)
    n_ge = count(lambda k, idx: k >= thr)
    idx_bits = max(1, int(np.ceil(np.log2(key_s.shape[0]))))

    def tie_bound():
        need = n_sel - count(lambda k, idx: k > thr)

        def ibit(b, bound):
            cand = bound + jnp.left_shift(jnp.int32(1), idx_bits - 1 - b)
            cnt = count(lambda k, idx: (k == thr) & (idx < cand))
            return jnp.where(cnt < need, cand, bound)

        return lax.fori_loop(0, idx_bits, ibit, jnp.zeros((1, tq), I32))

    bound = lax.cond(jnp.max(n_ge) > n_sel, tie_bound, lambda: jnp.full((1, tq), 2 ** idx_bits, I32))

    qc = jnp.concatenate([dq_ref[:, 128 * h:128 * h + 128] for h in range(DSA_HEADS)], axis=0)
    sc2 = scale * LOG2E
    m_s[...] = jnp.full(m_s.shape, NEG_INF, F32)
    l_s[...] = jnp.zeros(l_s.shape, F32)
    acc_s[...] = jnp.zeros(acc_s.shape, F32)

    def attn_chunk(c, carry):
        k0 = pl.multiple_of(c * tk, tk)
        key = key_s[pl.ds(k0, tk), :]
        idx = k0 + kiota
        sel = ((key > thr) | ((key == thr) & (idx <= bound))) & (idx <= qpos)
        bias = jnp.where(sel, 0.0, NEG_INF)
        s = _nt_dot(dk_ref[pl.ds(k0, tk), :], qc) + jnp.concatenate([bias] * DSA_HEADS, axis=1)
        m_prev = m_s[...]
        m_new = jnp.maximum(m_prev, jnp.max(s, axis=0, keepdims=True))
        m_safe = jnp.where(m_new == NEG_INF, 0.0, m_new)
        alpha = jnp.exp2((m_prev - m_safe) * sc2)
        p = jnp.exp2((s - m_safe) * sc2)
        l_s[...] = alpha * l_s[...] + jnp.sum(p, axis=0, keepdims=True)
        acc_s[...] = alpha * acc_s[...] + jnp.dot(vt_ref[:, pl.ds(k0, tk)], p.astype(BF16),
                                                  preferred_element_type=F32)
        m_s[...] = m_new
        return carry

    lax.fori_loop(0, nch, attn_chunk, 0)
    ot = acc_s[...] / l_s[...]
    for h in range(DSA_HEADS):
        o_ref[:, 128 * h:128 * h + 128] = _head_out(ot[:, h * tq:(h + 1) * tq].T, g_ref[:, 128 * h:128 * h + 128])


def _dsa(iq, ika, ikb, wt, dq, dk, vt, g_out, B, S, tq=256, tk=256):
    T = iq.shape[0]
    tk = min(tk, S)
    nq = S // tq
    H = DSA_HEADS
    n_sel = min(TOPK_MAX, S // 4)
    g_c = g_out[:, (MLA_HEADS + SB_HEADS) * HEAD_DIM:]
    return pl.pallas_call(
        functools.partial(_dsa_kernel, tq=tq, tk=tk, n_sel=n_sel, scale=HEAD_DIM ** -0.5),
        grid=(B, nq),
        in_specs=[pl.BlockSpec((tq, IDX_HEADS * IDX_DIM), lambda b, i: (b * nq + i, 0)),
                  pl.BlockSpec((S, 128), lambda b, i: (b, 0)),
                  pl.BlockSpec((S, 128), lambda b, i: (b, 0)),
                  pl.BlockSpec((1, IDX_HEADS, tq), lambda b, i: (b, 0, i)),
                  pl.BlockSpec((tq, H * 128), lambda b, i: (b * nq + i, 0)),
                  pl.BlockSpec((S, 128), lambda b, i: (b, 0)),
                  pl.BlockSpec((128, S), lambda b, i: (0, b)),
                  pl.BlockSpec((1, H * 128), lambda b, i: (0, 0))],
        out_specs=pl.BlockSpec((tq, H * 128), lambda b, i: (b * nq + i, 0)),
        out_shape=jax.ShapeDtypeStruct((T, H * HEAD_DIM), BF16),
        scratch_shapes=[pltpu.VMEM((S, tq), I32), pltpu.VMEM((1, H * tq), F32), pltpu.VMEM((1, H * tq), F32),
                        pltpu.VMEM((128, H * tq), F32)],
        compiler_params=_cparams(("parallel", "arbitrary")),
        name="dsa_attention",
    )(iq, ika, ikb, wt, dq, dk, vt, g_c)


def _route_kernel(x_ref, g_ref, sc_ref, sh_ref, wr_ref, br_ref, tri_ref,
                  h_ref, hp_ref, eidx_ref, gate_ref, pos_ref, cnt_ref, run_s):
    @pl.when(pl.program_id(0) == 0)
    def _():
        run_s[...] = jnp.zeros(run_s.shape, F32)

    h = _rms(x_ref[...], g_ref[...]) * (1.0 + sc_ref[0]) + sh_ref[0]
    hb = h.astype(BF16)
    h_ref[...] = hb
    tm, D = h.shape
    bits = pltpu.bitcast(hb.astype(F32), jnp.uint32)
    hp_ref[...] = (bits[:, :D // 2] >> 16) | (bits[:, D // 2:] & jnp.uint32(0xFFFF0000))
    logits = _nt_dot(wr_ref[...], hb)
    scores = jax.nn.sigmoid(logits)
    biased = scores + br_ref[...]
    gsz = N_EXPERTS // N_GROUPS
    b3 = biased.reshape(N_GROUPS, gsz, tm)
    m1 = jnp.max(b3, axis=1, keepdims=True)
    n1 = jnp.sum((b3 == m1).astype(F32), axis=1, keepdims=True)
    m2 = jnp.max(jnp.where(b3 < m1, b3, NEG_INF), axis=1, keepdims=True)
    grp = (m1 + jnp.where(n1 >= 2.0, m1, m2)).reshape(N_GROUPS, tm)
    gi = lax.broadcasted_iota(I32, (N_GROUPS, 1), 0)
    grank = jnp.zeros((N_GROUPS, tm), F32)
    for g in range(N_GROUPS):
        rowv = grp[g:g + 1, :]
        grank = grank + jnp.where((rowv > grp) | ((rowv == grp) & (g < gi)), 1.0, 0.0)
    gmask = grank.reshape(N_GROUPS, 1, tm) < float(TOPK_GROUPS)
    masked = jnp.where(gmask, b3, NEG_INF).reshape(N_EXPERTS, tm)
    ei = lax.broadcasted_iota(I32, (N_EXPERTS, 1), 0)
    rank = jnp.zeros((N_EXPERTS, tm), F32)
    for e in range(N_EXPERTS):
        rowv = masked[e:e + 1, :]
        rank = rank + jnp.where((rowv > masked) | ((rowv == masked) & (e < ei)), 1.0, 0.0)
    sel = rank < float(TOP_K)
    selm = sel.astype(F32)
    gsum = jnp.sum(scores * selm, axis=0, keepdims=True)
    gate = scores * selm / gsum * ROUTED_SCALE
    within = jnp.dot(selm.astype(BF16), tri_ref[...], preferred_element_type=F32)
    posf = within + run_s[...]
    run_s[...] += jnp.sum(selm, axis=1, keepdims=True)
    cnt_ref[...] = jnp.broadcast_to(run_s[...], cnt_ref.shape)
    eif = ei.astype(F32)
    for k in range(TOP_K):
        onek = rank == float(k)
        eidx_ref[k:k + 1, :] = jnp.sum(jnp.where(onek, eif, 0.0), axis=0, keepdims=True).astype(I32)
        gate_ref[k:k + 1, :] = jnp.sum(jnp.where(onek, gate, 0.0), axis=0, keepdims=True)
        pos_ref[k:k + 1, :] = jnp.sum(jnp.where(onek, posf, 0.0), axis=0, keepdims=True).astype(I32)


def _route(x, g, sc, sh, w_router, b_router, S, tm=512):
    T, D = x.shape
    B = sc.shape[0]
    tm = min(tm, S)
    r = np.arange(tm)
    tri = jnp.asarray((r[:, None] < r[None, :]).astype(np.float32), BF16)
    full = lambda shape: pl.BlockSpec(shape, lambda i: (0,) * len(shape))
    return pl.pallas_call(
        _route_kernel,
        grid=(T // tm,),
        in_specs=[pl.BlockSpec((tm, D), lambda i: (i, 0)),
                  full((1, D)),
                  pl.BlockSpec((1, 1, D), lambda i: (i * tm // S, 0, 0)),
                  pl.BlockSpec((1, 1, D), lambda i: (i * tm // S, 0, 0)),
                  full((N_EXPERTS, D)),
                  full((N_EXPERTS, 1)),
                  full((tm, tm))],
        out_specs=[pl.BlockSpec((tm, D), lambda i: (i, 0)),
                   pl.BlockSpec((tm, D // 2), lambda i: (i, 0)),
                   pl.BlockSpec((TOP_K, tm), lambda i: (0, i)),
                   pl.BlockSpec((TOP_K, tm), lambda i: (0, i)),
                   pl.BlockSpec((TOP_K, tm), lambda i: (0, i)),
                   full((N_EXPERTS, LANES))],
        out_shape=[jax.ShapeDtypeStruct((T, D), BF16),
                   jax.ShapeDtypeStruct((T, D // 2), jnp.uint32),
                   jax.ShapeDtypeStruct((TOP_K, T), I32),
                   jax.ShapeDtypeStruct((TOP_K, T), F32),
                   jax.ShapeDtypeStruct((TOP_K, T), I32),
                   jax.ShapeDtypeStruct((N_EXPERTS, LANES), F32)],
        scratch_shapes=[pltpu.VMEM((N_EXPERTS, 1), F32)],
        compiler_params=_cparams(("arbitrary",)),
        name="ffn_norm_route",
    )(x, g.reshape(1, D), sc.reshape(B, 1, D), sh.reshape(B, 1, D),
      w_router.T.astype(BF16), b_router.reshape(N_EXPERTS, 1), tri)


IDX_ALIGN = 1024
IDX_WIN = 2 * IDX_ALIGN


def _moe_kernel(bexp_ref, bval_ref, bcs_ref, slot_hbm, h_hbm, wg_ref, wu_ref, wd_ref, y_hbm,
                xbuf0, xbuf1, x_s, mid_s, ybuf0, ybuf1, wg_s, wu_s, wd_s, idx_s, isem, gsem, ssem, *, n_tok):
    b = pl.program_id(0)
    nb = pl.num_programs(0)
    rows, D = x_s.shape
    half = D // 2
    DE = mid_s.shape[1]
    YS = D // LANES
    NW = 256
    xbufs, ybufs = (xbuf0, xbuf1), (ybuf0, ybuf1)
    n_real = TOP_K * n_tok

    def idx_copy(blk):
        start = pl.multiple_of((bcs_ref[blk] // IDX_ALIGN) * IDX_ALIGN, IDX_ALIGN)
        return pltpu.make_async_copy(slot_hbm.at[pl.ds(start, IDX_WIN)],
                                     idx_s.at[pl.ds(pl.multiple_of((blk & 3) * IDX_WIN, IDX_WIN), IDX_WIN)],
                                     isem.at[blk & 3])

    def block_rows(blk):
        inside = (blk >= 0) & (blk < nb)
        safe = jnp.clip(blk, 0, nb - 1)
        return (blk & 3) * IDX_WIN + bcs_ref[safe] % IDX_ALIGN, jnp.where(inside, bval_ref[safe], 0)

    def row_id(blk, base, valid, r):
        return jnp.where(r < valid, idx_s[base + r], n_real + (blk & 1) * rows + r)

    def gather_start(blk, base, valid, r, p):
        tok = row_id(blk, base, valid, r) & (n_tok - 1)
        pltpu.make_async_copy(h_hbm.at[pl.ds(pl.multiple_of(tok * 8, 8), 8), :],
                              xbufs[p].at[pl.ds(r * 8, 8), :], gsem.at[p]).start(priority=r % 2)

    def scatter_start(blk, base, valid, r, p):
        dst = row_id(blk, base, valid, r)
        pltpu.make_async_copy(ybufs[p].at[pl.ds(r * YS, YS), :],
                              y_hbm.at[pl.ds(pl.multiple_of(dst * YS, YS), YS), :], ssem.at[p]).start(priority=r % 2)

    def wait_gather(p):
        pltpu.make_async_copy(xbufs[p], xbufs[p], gsem.at[p]).wait()

    def wait_scatter(p):
        pltpu.make_async_copy(ybufs[p], ybufs[p], ssem.at[p]).wait()

    def compute_pieces(p):
        def unpack():
            for s in range(8):
                w = xbufs[p][pl.ds(s, rows, stride=8), :]
                x_s[:, 128 * s:128 * s + 128] = pltpu.bitcast(w << 16, F32).astype(BF16)
                x_s[:, half + 128 * s:half + 128 * s + 128] = pltpu.bitcast(w & jnp.uint32(0xFFFF0000), F32).astype(BF16)

        def gate_up(j):
            def piece():
                x = x_s[...]
                gt = jnp.dot(x, wg_s[:, NW * j:NW * j + NW], preferred_element_type=F32)
                up = jnp.dot(x, wu_s[:, NW * j:NW * j + NW], preferred_element_type=F32)
                mid_s[:, NW * j:NW * j + NW] = (gt * jax.nn.sigmoid(gt) * up).astype(BF16)
            return piece

        def down(n):
            def piece():
                y = jnp.dot(mid_s[...], wd_s[:, NW * n:NW * n + NW], preferred_element_type=F32)
                for s in range(NW // LANES):
                    ybufs[p][pl.ds(n * (NW // LANES) + s, rows, stride=YS), :] = y[:, LANES * s:LANES * s + LANES]
            return piece
        return [unpack] + [gate_up(j) for j in range(DE // NW)], [down(n) for n in range(D // NW)]

    def interleave(pieces, starts):
        n_slots = max(len(pieces), 1)
        per = -(-len(starts) // n_slots)
        for n in range(n_slots):
            if pieces:
                pieces[n]()
            for st in starts[n * per:(n + 1) * per]:
                st()

    def run_block(p, compute):
        first, second = compute_pieces(p) if compute else ([], [])
        gbase, gvalid = block_rows(b + 1)
        interleave(first, [functools.partial(gather_start, b + 1, gbase, gvalid, r, 1 - p) for r in range(rows)])
        wait_scatter(p)
        sbase, svalid = block_rows(b - 1)
        interleave(second, [functools.partial(scatter_start, b - 1, sbase, svalid, r, 1 - p) for r in range(rows)])

    @pl.when(b == 0)
    def _():
        ybuf0[...] = jnp.zeros(ybuf0.shape, F32)
        ybuf1[...] = jnp.zeros(ybuf1.shape, F32)
        pltpu.make_async_copy(ybuf0, y_hbm.at[pl.ds(n_real * YS, rows * YS), :], ssem.at[0]).start()
        idx_copy(0).start()
        idx_copy(0).wait()
        base0, valid0 = block_rows(b)
        for r in range(rows):
            gather_start(b, base0, valid0, r, 0)
        idx_copy(1).start()

    @pl.when(b + 2 < nb)
    def _():
        idx_copy(b + 2).start()

    @pl.when(b + 1 < nb)
    def _():
        idx_copy(b + 1).wait()

    @pl.when((b == 0) | (bexp_ref[b] != bexp_ref[jnp.maximum(b - 1, 0)]))
    def _():
        wg_s[...] = wg_ref[0, 0].astype(BF16)
        wu_s[...] = wu_ref[0, 0].astype(BF16)
        wd_s[...] = wd_ref[0, 0].astype(BF16)

    active = bval_ref[b] > 0
    for p in range(2):
        @pl.when((b & 1) == p)
        def _():
            wait_gather(p)

            @pl.when(active)
            def _():
                run_block(p, True)

            @pl.when(jnp.logical_not(active))
            def _():
                run_block(p, False)

            @pl.when(b == nb - 1)
            def _():
                sbase, svalid = block_rows(b)
                for r in range(rows):
                    scatter_start(b, sbase, svalid, r, p)
                wait_scatter(p)
                wait_scatter(1 - p)
                wait_gather(1 - p)


def _moe(hp, slot_ids, blk_exp, blk_val, blk_cs, w_gate, w_up, w_down, l):
    T = hp.shape[0] // 8
    D = w_gate.shape[2]
    assert T & (T - 1) == 0 and D == 2 * 8 * LANES
    DE = w_gate.shape[3]
    nb = blk_exp.shape[0]
    wspec = lambda shape: pl.BlockSpec((1, 1) + shape, lambda b, bexp, bval, bcs: (l, bexp[b], 0, 0))
    grid_spec = pltpu.PrefetchScalarGridSpec(
        num_scalar_prefetch=3,
        grid=(nb,),
        in_specs=[pl.BlockSpec(memory_space=pl.ANY), pl.BlockSpec(memory_space=pl.ANY),
                  wspec((D, DE)), wspec((D, DE)), wspec((DE, D))],
        out_specs=pl.BlockSpec(memory_space=pl.ANY),
        scratch_shapes=[pltpu.VMEM((MOE_ROWS * 8, LANES), jnp.uint32), pltpu.VMEM((MOE_ROWS * 8, LANES), jnp.uint32),
                        pltpu.VMEM((MOE_ROWS, D), BF16), pltpu.VMEM((MOE_ROWS, DE), BF16),
                        pltpu.VMEM((MOE_ROWS * D // LANES, LANES), F32), pltpu.VMEM((MOE_ROWS * D // LANES, LANES), F32),
                        pltpu.VMEM((D, DE), BF16), pltpu.VMEM((D, DE), BF16), pltpu.VMEM((DE, D), BF16),
                        pltpu.SMEM((4 * IDX_WIN,), I32),
                        pltpu.SemaphoreType.DMA((4,)), pltpu.SemaphoreType.DMA((2,)), pltpu.SemaphoreType.DMA((2,))],
    )
    return pl.pallas_call(
        functools.partial(_moe_kernel, n_tok=T),
        grid_spec=grid_spec,
        out_shape=jax.ShapeDtypeStruct(((TOP_K * T + 2 * MOE_ROWS) * D // LANES, LANES), F32),
        compiler_params=_cparams(("arbitrary",)),
        name="moe_experts",
    )(blk_exp, blk_val, blk_cs, slot_ids, hp, w_gate, w_up, w_down)


def _ffn_out_kernel(*refs, final):
    y_refs = refs[:TOP_K]
    gk_ref, h_ref, x_ref, gf_ref, wsg_ref, wsu_ref, wsd_ref, fg_ref, o_ref, r_s = refs[TOP_K:]
    tm, D = x_ref.shape
    ys = D // LANES
    acc = None
    for k in range(TOP_K):
        gk = jnp.broadcast_to(gk_ref[k][:, None, :], (tm, ys, LANES)).reshape(tm * ys, LANES)
        term = y_refs[k][...] * gk
        acc = term if acc is None else acc + term
    r_s[...] = acc
    routed = jnp.concatenate([r_s[pl.ds(s, tm, stride=ys), :] for s in range(ys)], axis=1)
    hb = h_ref[...]
    gt = jnp.dot(hb, wsg_ref[...], preferred_element_type=F32)
    up = jnp.dot(hb, wsu_ref[...], preferred_element_type=F32)
    mid = (gt * jax.nn.sigmoid(gt) * up).astype(BF16)
    shared = jnp.dot(mid, wsd_ref[...], preferred_element_type=F32)
    out = x_ref[...] + gf_ref[0] * (routed + shared)
    if final:
        out = _rms(out, fg_ref[...])
    o_ref[...] = out


def _ffn_out(y, gate3, h, x, g_f, wsg, wsu, wsd, final_g, final, S, tm=128):
    T, D = x.shape
    B = g_f.shape[0]
    DS = wsg.shape[1]
    tm = min(tm, S)
    nt = T // tm
    full = lambda shape: pl.BlockSpec(shape, lambda i: (0,) * len(shape))
    y_specs = [pl.BlockSpec((tm * D // LANES, LANES), functools.partial(lambda i, k: (k * nt + i, 0), k=k))
               for k in range(TOP_K)]
    return pl.pallas_call(
        functools.partial(_ffn_out_kernel, final=final),
        grid=(nt,),
        in_specs=y_specs + [
                  pl.BlockSpec((TOP_K, tm, 1), lambda i: (0, i, 0)),
                  pl.BlockSpec((tm, D), lambda i: (i, 0)),
                  pl.BlockSpec((tm, D), lambda i: (i, 0)),
                  pl.BlockSpec((1, 1, D), lambda i: (i * tm // S, 0, 0)),
                  full((D, DS)), full((D, DS)), full((DS, D)), full((1, D))],
        out_specs=pl.BlockSpec((tm, D), lambda i: (i, 0)),
        out_shape=jax.ShapeDtypeStruct((T, D), F32),
        scratch_shapes=[pltpu.VMEM((tm * D // LANES, LANES), F32)],
        compiler_params=_cparams(("parallel",)),
        name="ffn_combine",
    )(*([y] * TOP_K), gate3, h, x, g_f.reshape(B, 1, D), wsg, wsu, wsd, final_g.reshape(1, D))


def _layout_w_in(w):
    D = w.shape[0]
    sizes = (Q_LORA, KV_LORA, MLA_ROPE, 512, 512, 512, 512, HEAD_DIM, HEAD_DIM, IDX_HEADS * IDX_DIM, IDX_DIM, IDX_HEADS)
    offs = np.concatenate([[0], np.cumsum(sizes)])
    cq, ckv, kr, sbq, sbk, sbv, dsq, dsk, dsv, ixq, ixk, ixw = [w[:, offs[n]:offs[n + 1]] for n in range(12)]
    z = lambda n: jnp.zeros((D, n), w.dtype)
    out = jnp.concatenate([cq, sbq, sbk, sbv, dsq, ckv, dsk, dsv, ixq, kr, z(64), ixk, z(64), ixw, z(112), z(128)], axis=1)
    assert out.shape[1] == IN_PAD
    return out.astype(BF16)


def _layout_w_uq(w):
    K = w.shape[0]
    w3 = w.reshape(K, MLA_HEADS, MLA_NOPE + MLA_ROPE)
    w3 = jnp.concatenate([w3, jnp.zeros((K, MLA_HEADS, 64), w.dtype)], axis=2)
    return w3.reshape(K, MLA_HEADS * 256).astype(BF16)


def _layout_w_ukv(w):
    K = w.shape[0]
    w3 = w.reshape(K, MLA_HEADS, MLA_NOPE + HEAD_DIM)
    return jnp.concatenate([w3[:, :, :MLA_NOPE].reshape(K, -1), w3[:, :, MLA_NOPE:].reshape(K, -1)], axis=1).astype(BF16)


def _expert_blocks(eidx, pos, counts, T):
    M = T * TOP_K
    nb = M // MOE_ROWS + N_EXPERTS
    padded = (counts + MOE_ROWS - 1) // MOE_ROWS * MOE_ROWS
    pend = jnp.cumsum(padded)
    pstart = pend - padded
    cstart = jnp.cumsum(counts) - counts
    experts = jnp.arange(N_EXPERTS, dtype=I32)
    order = eidx * T + pos
    out_row = jnp.arange(TOP_K, dtype=I32)[:, None] * T + jnp.arange(T, dtype=I32)[None, :]
    _, slot_ids = lax.sort_key_val(order.reshape(-1), out_row.reshape(-1))
    slot_ids = jnp.concatenate([slot_ids, jnp.zeros((IDX_WIN,), I32)])
    blk_start = jnp.arange(nb, dtype=I32) * MOE_ROWS
    blk_exp = jnp.minimum(jnp.sum((pend[None, :] <= blk_start[:, None]).astype(I32), axis=1), N_EXPERTS - 1)
    pick = lambda v: jnp.sum(jnp.where(blk_exp[:, None] == experts, v, 0), axis=-1)
    blk_val = jnp.clip(pick(pstart + counts) - blk_start, 0, MOE_ROWS).astype(I32)
    blk_cs = jnp.where(blk_val > 0, pick(cstart - pstart) + blk_start, 0).astype(I32)
    return slot_ids, blk_exp, blk_val, blk_cs


def _mixers(x, S, B, positions_tabs, mod, l, w):
    sh_m, sc_m, g_m = mod[0], mod[1], mod[2]
    proj = _nmm(x, 0, x.shape[1], w["norm_mix_g"][l], w["w_in"][l], S, sc=sc_m, sh=sh_m)
    q_raw = _nmm(proj, C_CQ // Q_LORA, Q_LORA, w["g_cq"][l], w["w_uq"][l], S)
    kv_raw = _nmm(proj, C_CKV // KV_LORA, KV_LORA, w["g_ckv"][l], w["w_ukv"][l], S)
    tabs64, tabs128 = positions_tabs
    (qm, km, vmt, sbq, sbk, sbvt, dq, dk, dvt, iq, ika, ikb, iw) = _prep(proj, q_raw, kv_raw, tabs64, tabs128)
    g_out = w["g_out"][l].reshape(1, -1)
    o_a = _mla(qm, km, vmt, g_out, B, S)
    o_b = _sb(sbq, sbk, sbvt, g_out, B, S)
    wt = iw[:, :IDX_HEADS].reshape(B, S, IDX_HEADS).transpose(0, 2, 1)
    o_c = _dsa(iq, ika, ikb, wt, dq, dk, dvt, g_out, B, S)
    return _mm_res((o_a, o_b, o_c), w["w_o"][l], x, g_m, S)


def _ffn(x, S, B, mod, l, w, final_g, final):
    T, D = x.shape
    sh_f, sc_f, g_f = mod[3], mod[4], mod[5]
    h, hp, eidx, gate, pos, cnt = _route(x, w["norm_ffn_g"][l], sc_f, sh_f, w["w_router"][l], w["b_router"][l], S)
    counts = cnt[:, 0].astype(I32)
    slot_ids, blk_exp, blk_val, blk_cs = _expert_blocks(eidx, pos, counts, T)
    y = _moe(hp.reshape(T * 8, LANES), slot_ids, blk_exp, blk_val, blk_cs, w["w_gate"], w["w_up"], w["w_down"], l)
    return _ffn_out(y, gate.reshape(TOP_K, T, 1), h, x, g_f, w["ws_gate"][l], w["ws_up"][l], w["ws_down"][l],
                    final_g, final, S)


def kernel(x, c, positions, norm_mix_g, norm_ffn_g, w_ada, b_ada, w_in, g_cq, g_ckv, w_uq, w_ukv, g_out, w_o,
           w_router, b_router, w_gate, w_up, w_down, ws_gate, ws_up, ws_down, final_g):
    B, S, D = x.shape
    L = w_in.shape[0]
    T = B * S
    w = dict(norm_mix_g=norm_mix_g, norm_ffn_g=norm_ffn_g, g_cq=g_cq, g_ckv=g_ckv, g_out=g_out,
             w_in=jax.vmap(_layout_w_in)(w_in), w_uq=jax.vmap(_layout_w_uq)(w_uq), w_ukv=jax.vmap(_layout_w_ukv)(w_ukv),
             w_o=w_o.astype(BF16), w_router=w_router, b_router=b_router, w_gate=w_gate, w_up=w_up, w_down=w_down,
             ws_gate=ws_gate.astype(BF16), ws_up=ws_up.astype(BF16), ws_down=ws_down.astype(BF16))
    mod_all = _ada(c, w_ada, b_ada).reshape(L, B, 6, D)
    posf = positions.astype(F32).reshape(T, 1)
    tabs = (_rope_tables(posf, 64), _rope_tables(posf, 128))
    xt = x.reshape(T, D)
    for l in range(L):
        mod = [mod_all[l, :, n, :] for n in range(6)]
        xt = _mixers(xt, S, B, tabs, mod, l, w)
        xt = _ffn(xt, S, B, mod, l, w, final_g, l == L - 1)
    return xt.reshape(B, S, D)
```
